```python
import math
import jax, jax.numpy as jnp
from jax import lax
import numpy as np

D_MODEL = 1024
BATCH = 4
SEQ = 4096
DEPTH = 2

D_A = D_MODEL
CONV_A_WIDTH = 3
D_R = (D_MODEL * 5) // 4
N_RG_BLOCKS = 10
RG_BLOCK = D_R // N_RG_BLOCKS
CONV_R_WIDTH = 4
RG_C = 8.0
SPLITS = (D_A, 2 * D_A, 3 * D_A, 3 * D_A + D_R, 3 * D_A + 2 * D_R, 3 * D_A + 2 * D_R + D_MODEL)
D_IN = 3 * D_A + 2 * D_R + 2 * D_MODEL
N_EXPERTS = 16
N_GROUPS = 4
EXPERTS_PER_GROUP = N_EXPERTS // N_GROUPS
TOP_K = 2
D_FF_EXPERT = D_MODEL // 2
DN_ALPHA = (2.0 * DEPTH) ** 0.25
DN_BETA = (8.0 * DEPTH) ** -0.25
LN_EPS = 1e-5

kernel_name = 'hybrid_conv_rglru_grouped_moe_deepnorm'


def layer_norm(x, g, b):
    xf = x.astype(jnp.float32)
    mu = jnp.mean(xf, axis=-1, keepdims=True)
    var = jnp.mean(jnp.square(xf - mu), axis=-1, keepdims=True)
    return ((xf - mu) * lax.rsqrt(var + LN_EPS) * g.astype(jnp.float32) + b.astype(jnp.float32)).astype(x.dtype)


def causal_dwconv(u, w):
    k_width = w.shape[0]
    s = u.shape[1]
    up = jnp.pad(u, ((0, 0), (k_width - 1, 0), (0, 0)))
    y = up[:, 0:s] * w[0]
    for k in range(1, k_width):
        y = y + up[:, k:k + s] * w[k]
    return y


def linear_recurrence(a, b):
    def combine(left, right):
        a_l, b_l = left
        a_r, b_r = right
        return a_r * a_l, a_r * b_l + b_r
    _, h = lax.associative_scan(combine, (a, b), axis=1)
    return h


def block_diag(u, w, bias):
    bsz, s, _ = u.shape
    ub = u.reshape(bsz, s, N_RG_BLOCKS, RG_BLOCK)
    return (jnp.einsum('bshi,hij->bshj', ub, w) + bias).reshape(bsz, s, D_R)


def shared_group_router(xt, w_router, b_router):
    logits = xt.astype(jnp.float32) @ w_router.astype(jnp.float32) + b_router.astype(jnp.float32)
    probs = jax.nn.softmax(logits, axis=-1)
    probs_g = probs.reshape(-1, N_GROUPS, EXPERTS_PER_GROUP)
    group_score = jnp.sum(lax.top_k(probs_g, TOP_K)[0], axis=-1)
    g_sel = jnp.argmax(group_score, axis=-1)
    in_group = jnp.take_along_axis(probs_g, g_sel[:, None, None], axis=1)[:, 0]
    vals, local = lax.top_k(in_group, TOP_K)
    vals = vals / jnp.sum(vals, axis=-1, keepdims=True)
    idx = g_sel[:, None] * EXPERTS_PER_GROUP + local
    return jnp.sum(jax.nn.one_hot(idx, N_EXPERTS, dtype=jnp.float32) * vals[..., None], axis=1)


def grouped_moe(x, w_router, b_router, w_gate, w_up, w_down):
    bsz, s, d = x.shape
    xt = x.reshape(bsz * s, d)
    gates = shared_group_router(xt, w_router, b_router).astype(x.dtype)
    out = jnp.zeros_like(xt)
    for e in range(N_EXPERTS):
        h = jax.nn.silu(xt @ w_gate[e]) * (xt @ w_up[e])
        out = out + gates[:, e:e + 1] * (h @ w_down[e])
    return out.reshape(bsz, s, d)


def setup_inputs(seed: int = 0) -> dict:
    key = jax.random.key(seed)
    ks = jax.random.split(key, 24)

    def nrm(k, shape, scale):
        return jax.random.normal(k, shape, jnp.float32) * scale

    u = jax.random.uniform(ks[10], (DEPTH, D_R), jnp.float32, minval=0.9, maxval=0.999)
    p = u ** (1.0 / RG_C)
    rg_lambda = jnp.log(p) - jnp.log1p(-p)
    return {
        'x': nrm(ks[0], (BATCH, SEQ, D_MODEL), 1.0),
        'w_in': nrm(ks[1], (DEPTH, D_MODEL, D_IN), D_MODEL ** -0.5),
        'conv_a_w': nrm(ks[2], (DEPTH, CONV_A_WIDTH, D_A), CONV_A_WIDTH ** -0.5),
        'w_branch_a': nrm(ks[3], (DEPTH, D_A, D_MODEL), DN_BETA * D_A ** -0.5),
        'conv_r_w': nrm(ks[4], (DEPTH, CONV_R_WIDTH, D_R), CONV_R_WIDTH ** -0.5),
        'conv_r_b': nrm(ks[5], (DEPTH, D_R), 0.01),
        'w_rg_a': nrm(ks[6], (DEPTH, N_RG_BLOCKS, RG_BLOCK, RG_BLOCK), RG_BLOCK ** -0.5),
        'b_rg_a': nrm(ks[7], (DEPTH, N_RG_BLOCKS, RG_BLOCK), 0.01),
        'w_rg_x': nrm(ks[8], (DEPTH, N_RG_BLOCKS, RG_BLOCK, RG_BLOCK), RG_BLOCK ** -0.5),
        'b_rg_x': nrm(ks[9], (DEPTH, N_RG_BLOCKS, RG_BLOCK), 0.01),
        'rg_lambda': rg_lambda,
        'w_branch_r': nrm(ks[11], (DEPTH, D_R, D_MODEL), DN_BETA * D_R ** -0.5),
        'w_out': nrm(ks[12], (DEPTH, D_MODEL, D_MODEL), DN_BETA * D_MODEL ** -0.5),
        'ln1_g': 1.0 + nrm(ks[13], (DEPTH, D_MODEL), 0.02),
        'ln1_b': nrm(ks[14], (DEPTH, D_MODEL), 0.02),
        'w_router': nrm(ks[15], (D_MODEL, N_EXPERTS), D_MODEL ** -0.5),
        'b_router': nrm(ks[16], (N_EXPERTS,), 0.01),
        'w_exp_gate': nrm(ks[17], (DEPTH, N_EXPERTS, D_MODEL, D_FF_EXPERT), D_MODEL ** -0.5),
        'w_exp_up': nrm(ks[18], (DEPTH, N_EXPERTS, D_MODEL, D_FF_EXPERT), D_MODEL ** -0.5),
        'w_exp_down': nrm(ks[19], (DEPTH, N_EXPERTS, D_FF_EXPERT, D_MODEL), DN_BETA * D_FF_EXPERT ** -0.5),
        'ln2_g': 1.0 + nrm(ks[20], (DEPTH, D_MODEL), 0.02),
        'ln2_b': nrm(ks[21], (DEPTH, D_MODEL), 0.02),
    }


def reference(x, w_in, conv_a_w, w_branch_a, conv_r_w, conv_r_b, w_rg_a, b_rg_a, w_rg_x, b_rg_x,
              rg_lambda, w_branch_r, w_out, ln1_g, ln1_b, w_router, b_router,
              w_exp_gate, w_exp_up, w_exp_down, ln2_g, ln2_b):
    for l in range(DEPTH):
        z = x @ w_in[l]
        h_a, b_a, c_a, x_r, y_r, g_a, g_r = jnp.split(z, SPLITS, axis=-1)
        ya = (b_a * causal_dwconv(c_a * h_a, conv_a_w[l])) @ w_branch_a[l]
        xc = causal_dwconv(x_r, conv_r_w[l]) + conv_r_b[l]
        r_gate = jax.nn.sigmoid(block_diag(xc, w_rg_a[l], b_rg_a[l]))
        i_gate = jax.nn.sigmoid(block_diag(xc, w_rg_x[l], b_rg_x[l]))
        log_a = -RG_C * r_gate.astype(jnp.float32) * jax.nn.softplus(-rg_lambda[l].astype(jnp.float32))
        a_t = jnp.exp(log_a)
        b_t = jnp.sqrt(-jnp.expm1(2.0 * log_a)) * (i_gate * xc).astype(jnp.float32)
        h_r = linear_recurrence(a_t, b_t).astype(x.dtype)
        yr = (h_r * jax.nn.gelu(y_r)) @ w_branch_r[l]
        merged = jax.nn.sigmoid(g_a) * ya + jax.nn.sigmoid(g_r) * yr
        x = layer_norm(DN_ALPHA * x + merged @ w_out[l], ln1_g[l], ln1_b[l])
        moe = grouped_moe(x, w_router, b_router, w_exp_gate[l], w_exp_up[l], w_exp_down[l])
        x = layer_norm(DN_ALPHA * x + moe, ln2_g[l], ln2_b[l])
    return x
```

```python
import functools

import jax
import jax.numpy as jnp
from jax import lax
from jax.experimental import pallas as pl
from jax.experimental.pallas import tpu as pltpu

D_MODEL = 1024
D_A = 1024
D_R = 1280
N_RG_BLOCKS = 10
RG_BLOCK = 128
RG_C = 8.0
D_IN = 3 * D_A + 2 * D_R + 2 * D_MODEL
N_EXPERTS = 16
N_GROUPS = 4
EXPERTS_PER_GROUP = 4
N_PAIRS = 6
N_CLASSES = N_GROUPS * N_PAIRS
D_FF = 512
LN_EPS = 1e-5

OFF_HA, OFF_BA, OFF_CA = 0, D_A, 2 * D_A
OFF_XR = 3 * D_A
OFF_YR = OFF_XR + D_R
OFF_GA = OFF_YR + D_R
OFF_GR = OFF_GA + D_MODEL

SUBLANES = 8
LANES = 128
EXT = LANES
TT = 256
TM = 256
VMEM_LIMIT = 56 * 1024 * 1024

_BF = jnp.bfloat16
_F32 = jnp.float32


def _sigmoid(v):
    return 0.5 * jnp.tanh(0.5 * v) + 0.5


def _gelu_tanh(v):
    c = 0.7978845608028654
    return 0.5 * v * (1.0 + jnp.tanh(c * (v + 0.044715 * (v * v * v))))


def _layer_norm(y, g, b):
    mu = jnp.mean(y, axis=-1, keepdims=True)
    yc = y - mu
    var = jnp.mean(yc * yc, axis=-1, keepdims=True)
    return yc * lax.rsqrt(var + LN_EPS) * g + b


def _first_index_of(vals, target):
    idx = jnp.full(target.shape, float(len(vals) - 1), _F32)
    for j in range(len(vals) - 2, -1, -1):
        idx = jnp.where(vals[j] == target, float(j), idx)
    return idx


def _route(logits_t):
    m = jnp.max(logits_t, axis=0, keepdims=True)
    e = jnp.exp(logits_t - m)
    p = e / jnp.sum(e, axis=0, keepdims=True)
    neg = jnp.full((1, logits_t.shape[1]), -jnp.inf, _F32)
    scores, m1s, i1s, m2s, i2s = [], [], [], [], []
    for g in range(N_GROUPS):
        v = [p[g * EXPERTS_PER_GROUP + j:g * EXPERTS_PER_GROUP + j + 1, :] for j in range(EXPERTS_PER_GROUP)]
        m1 = jnp.maximum(jnp.maximum(v[0], v[1]), jnp.maximum(v[2], v[3]))
        i1 = _first_index_of(v, m1)
        w = [jnp.where(i1 == float(j), neg, v[j]) for j in range(EXPERTS_PER_GROUP)]
        m2 = jnp.maximum(jnp.maximum(w[0], w[1]), jnp.maximum(w[2], w[3]))
        i2 = _first_index_of(w, m2)
        scores.append(m1 + m2)
        m1s.append(m1); i1s.append(i1); m2s.append(m2); i2s.append(i2)
    best = jnp.maximum(jnp.maximum(scores[0], scores[1]), jnp.maximum(scores[2], scores[3]))
    gsel = _first_index_of(scores, best)

    def pick(xs):
        out = xs[N_GROUPS - 1]
        for g in range(N_GROUPS - 2, -1, -1):
            out = jnp.where(gsel == float(g), xs[g], out)
        return out

    m1, i1, m2, i2 = pick(m1s), pick(i1s), pick(m2s), pick(i2s)
    den = m1 + m2
    gate1, gate2 = m1 / den, m2 / den
    first_is_lo = i1 < i2
    lo = jnp.where(first_is_lo, i1, i2)
    hi = jnp.where(first_is_lo, i2, i1)
    g_lo = jnp.where(first_is_lo, gate1, gate2)
    g_hi = jnp.where(first_is_lo, gate2, gate1)
    pair = lo * (7.0 - lo) * 0.5 + (hi - lo - 1.0)
    cls = gsel * float(N_PAIRS) + pair
    return cls, g_lo, g_hi


def _mix_kernel(alpha,
                x_ref, win_ref, caw_ref, wa_ref, crw_ref, crb_ref, wrg_ref, brga_ref, brgx_ref,
                lam_ref, wr_ref, wo_ref, g1_ref, b1_ref, wrt_hi_ref, wrt_lo_ref, brt_ref,
                xe_ref, route_ref,
                ubuf, xrbuf, abuf, bbuf, hbuf, hcarry):
    t = pl.program_id(1)
    tt = x_ref.shape[0]

    @pl.when(t == 0)
    def _():
        ubuf[pl.ds(0, SUBLANES), :] = jnp.zeros((SUBLANES, D_A), _F32)
        xrbuf[pl.ds(0, SUBLANES), :] = jnp.zeros((SUBLANES, D_R), _F32)
        hcarry[...] = jnp.zeros_like(hcarry)

    @pl.when(t > 0)
    def _():
        ubuf[pl.ds(0, SUBLANES), :] = ubuf[pl.ds(tt, SUBLANES), :]
        xrbuf[pl.ds(0, SUBLANES), :] = xrbuf[pl.ds(tt, SUBLANES), :]

    x = x_ref[...]
    xb = x.astype(_BF)

    def proj(off, width):
        return jnp.dot(xb, win_ref[:, off:off + width], preferred_element_type=_F32)

    ubuf[pl.ds(SUBLANES, tt), :] = proj(OFF_CA, D_A) * proj(OFF_HA, D_A)
    conv_a = (ubuf[pl.ds(SUBLANES - 2, tt), :] * caw_ref[0:1, :]
              + ubuf[pl.ds(SUBLANES - 1, tt), :] * caw_ref[1:2, :]
              + ubuf[pl.ds(SUBLANES, tt), :] * caw_ref[2:3, :])
    va = (proj(OFF_BA, D_A) * conv_a).astype(_BF)
    ya = jnp.dot(va, wa_ref[...], preferred_element_type=_F32)

    xrbuf[pl.ds(SUBLANES, tt), :] = proj(OFF_XR, D_R)
    xc = (xrbuf[pl.ds(SUBLANES - 3, tt), :] * crw_ref[0:1, :]
          + xrbuf[pl.ds(SUBLANES - 2, tt), :] * crw_ref[1:2, :]
          + xrbuf[pl.ds(SUBLANES - 1, tt), :] * crw_ref[2:3, :]
          + xrbuf[pl.ds(SUBLANES, tt), :] * crw_ref[3:4, :]
          + crb_ref[...])
    xcb = xc.astype(_BF)
    z = -lam_ref[...]
    softplus_neg_lam = jnp.maximum(z, 0.0) + jnp.log1p(jnp.exp(-jnp.abs(z)))
    row = lax.broadcasted_iota(jnp.int32, (tt // SUBLANES, SUBLANES, RG_BLOCK), 1)
    for h in range(N_RG_BLOCKS):
        sl = slice(h * RG_BLOCK, (h + 1) * RG_BLOCK)
        gh = jnp.dot(xcb[:, sl], wrg_ref[h], preferred_element_type=_F32)
        r_gate = _sigmoid(gh[:, :RG_BLOCK] + brga_ref[:, sl])
        i_gate = _sigmoid(gh[:, RG_BLOCK:] + brgx_ref[:, sl])
        log_a = (-RG_C) * r_gate * softplus_neg_lam[:, sl]
        a_t = jnp.exp(log_a)
        b_t = jnp.sqrt(1.0 - a_t * a_t) * (i_gate * xc[:, sl])
        a3 = a_t.reshape(tt // SUBLANES, SUBLANES, RG_BLOCK)
        b3 = b_t.reshape(tt // SUBLANES, SUBLANES, RG_BLOCK)
        for k in (1, 2, 4):
            keep = row >= k
            b3 = jnp.where(keep, a3 * pltpu.roll(b3, k, 1) + b3, b3)
            a3 = jnp.where(keep, a3 * pltpu.roll(a3, k, 1), a3)
        abuf[:, sl] = a3.reshape(tt, RG_BLOCK)
        bbuf[:, sl] = b3.reshape(tt, RG_BLOCK)

    def carry_groups(g, h_prev):
        o = pl.multiple_of(g * SUBLANES, SUBLANES)
        hg = abuf[pl.ds(o, SUBLANES), :] * h_prev + bbuf[pl.ds(o, SUBLANES), :]
        hbuf[pl.ds(o, SUBLANES), :] = hg
        return hg[SUBLANES - 1:SUBLANES, :]

    hcarry[...] = lax.fori_loop(0, tt // SUBLANES, carry_groups, hcarry[...])
    vr = (hbuf[...] * _gelu_tanh(proj(OFF_YR, D_R))).astype(_BF)
    yr = jnp.dot(vr, wr_ref[...], preferred_element_type=_F32)

    merged = _sigmoid(proj(OFF_GA, D_MODEL)) * ya + _sigmoid(proj(OFF_GR, D_MODEL)) * yr
    o = jnp.dot(merged.astype(_BF), wo_ref[...], preferred_element_type=_F32)
    x1 = _layer_norm(alpha * x + o, g1_ref[...], b1_ref[...])

    x1_hi = x1.astype(_BF)
    x1_lo = (x1 - x1_hi.astype(_F32)).astype(_BF)
    nt_dims = (((1,), (1,)), ((), ()))
    logits_t = (lax.dot_general(wrt_hi_ref[...], x1_hi, nt_dims, preferred_element_type=_F32)
                + lax.dot_general(wrt_hi_ref[...], x1_lo, nt_dims, preferred_element_type=_F32)
                + lax.dot_general(wrt_lo_ref[...], x1_hi, nt_dims, preferred_element_type=_F32)
                + brt_ref[...])
    cls, g_lo, g_hi = _route(logits_t)
    rsel = lax.broadcasted_iota(jnp.int32, (SUBLANES, tt), 0)
    route_ref[...] = jnp.where(rsel == 0, cls, jnp.where(rsel == 1, g_lo, jnp.where(rsel == 2, g_hi, 0.0)))
    esel = lax.broadcasted_iota(jnp.int32, (EXT, tt), 0)
    ext_t = jnp.where(esel == 0, cls, jnp.where(esel == 1, g_lo, jnp.where(esel == 2, g_hi, 0.0)))
    xe_ref[:, :D_MODEL] = x1
    xe_ref[:, D_MODEL:] = ext_t.T


def _mix_call(alpha, x2d, n_seq, w):
    n_tok = x2d.shape[0]
    seq = n_tok // n_seq
    nt = seq // TT

    def const(shape):
        nd = len(shape)
        return pl.BlockSpec(shape, lambda b, t, _nd=nd: (0,) * _nd, pipeline_mode=pl.Buffered(1))

    in_specs = [
        pl.BlockSpec((TT, D_MODEL), lambda b, t: (b * nt + t, 0)),
        const((D_MODEL, D_IN)),
        const((3, D_A)),
        const((D_A, D_MODEL)),
        const((4, D_R)),
        const((1, D_R)),
        const((N_RG_BLOCKS, RG_BLOCK, 2 * RG_BLOCK)),
        const((1, D_R)),
        const((1, D_R)),
        const((1, D_R)),
        const((D_R, D_MODEL)),
        const((D_MODEL, D_MODEL)),
        const((1, D_MODEL)),
        const((1, D_MODEL)),
        const((N_EXPERTS, D_MODEL)),
        const((N_EXPERTS, D_MODEL)),
        const((N_EXPERTS, 1)),
    ]
    out_specs = [
        pl.BlockSpec((TT, D_MODEL + EXT), lambda b, t: (b * nt + t, 0)),
        pl.BlockSpec((SUBLANES, TT), lambda b, t: (0, b * nt + t)),
    ]
    out_shape = [
        jax.ShapeDtypeStruct((n_tok, D_MODEL + EXT), _F32),
        jax.ShapeDtypeStruct((SUBLANES, n_tok), _F32),
    ]
    scratch = [
        pltpu.VMEM((TT + SUBLANES, D_A), _F32),
        pltpu.VMEM((TT + SUBLANES, D_R), _F32),
        pltpu.VMEM((TT, D_R), _F32),
        pltpu.VMEM((TT, D_R), _F32),
        pltpu.VMEM((TT, D_R), _F32),
        pltpu.VMEM((1, D_R), _F32),
    ]
    return pl.pallas_call(
        functools.partial(_mix_kernel, alpha),
        grid=(n_seq, nt),
        in_specs=in_specs,
        out_specs=out_specs,
        out_shape=out_shape,
        scratch_shapes=scratch,
        compiler_params=pltpu.CompilerParams(
            dimension_semantics=("arbitrary", "arbitrary"), vmem_limit_bytes=VMEM_LIMIT),
        name="mix",
    )(x2d, *w)


def _moe_kernel(alpha, n_tiles,
                elo_ref, ehi_ref, nv_ref,
                inv_cur, inv_nxt, xe_hbm,
                wg_lo, wu_lo, wd_lo, wg_hi, wu_hi, wd_hi, g2_ref, b2_ref,
                out_hbm,
                gbuf, obuf, gsem, ssem, pend):
    i = pl.program_id(0)
    slot = i % 2
    nv = nv_ref[i]

    def gather_copy(tok, r, s):
        return pltpu.make_async_copy(xe_hbm.at[pl.ds(tok, 1), :], gbuf.at[s, pl.ds(r, 1), :], gsem.at[s])

    def scatter_copy(tok, r, s):
        return pltpu.make_async_copy(obuf.at[s, pl.ds(r, 1), :], out_hbm.at[pl.ds(tok, 1), :], ssem.at[s])

    def start_gather(idx_ref, s):
        def body(r, c):
            gather_copy(idx_ref[0, 0, r], r, s).start()
            return c
        lax.fori_loop(0, TM, body, 0)

    def wait_scatter(s):
        def body(r, c):
            scatter_copy(0, 0, s).wait()
            return c
        lax.fori_loop(0, pend[s], body, 0)
        pend[s] = 0

    @pl.when(i == 0)
    def _():
        pend[0] = 0
        pend[1] = 0
        start_gather(inv_cur, 0)

    nxt = jnp.minimum(i + 1, n_tiles - 1)

    @pl.when(jnp.logical_and(i + 1 < n_tiles, nv_ref[nxt] > 0))
    def _():
        start_gather(inv_nxt, 1 - slot)

    @pl.when(nv > 0)
    def _():
        pltpu.make_async_copy(xe_hbm.at[pl.ds(0, TM), :], gbuf.at[slot], gsem.at[slot]).wait()
        wait_scatter(slot)
        xg = gbuf[slot]
        xf = xg[:, :D_MODEL]
        gate_lo = xg[:, D_MODEL + 1:D_MODEL + 2]
        gate_hi = xg[:, D_MODEL + 2:D_MODEL + 3]
        xb = xf.astype(_BF)

        def expert(wg, wu, wd, gate):
            a = jnp.dot(xb, wg[...], preferred_element_type=_F32)
            u = jnp.dot(xb, wu[...], preferred_element_type=_F32)
            hgt = ((a * _sigmoid(a)) * u * gate).astype(_BF)
            return jnp.dot(hgt, wd[...], preferred_element_type=_F32)

        moe = expert(wg_lo, wu_lo, wd_lo, gate_lo) + expert(wg_hi, wu_hi, wd_hi, gate_hi)
        obuf[slot] = _layer_norm(alpha * xf + moe, g2_ref[...], b2_ref[...])

        def body(r, c):
            scatter_copy(inv_cur[0, 0, r], r, slot).start()
            return c
        lax.fori_loop(0, nv, body, 0)
        pend[slot] = nv

    @pl.when(i == n_tiles - 1)
    def _():
        wait_scatter(0)
        wait_scatter(1)


def _moe_call(alpha, xe, plan, w):
    elo, ehi, nvalid, inv = plan
    n_tok = xe.shape[0]
    n_tiles = elo.shape[0]
    wg, wu, wd, g2, b2 = w

    def wspec(shape, which):
        def imap(i, elo_r, ehi_r, nv_r):
            return ((elo_r, ehi_r)[which][i], 0, 0)
        return pl.BlockSpec((None,) + shape, imap)

    grid_spec = pltpu.PrefetchScalarGridSpec(
        num_scalar_prefetch=3,
        grid=(n_tiles,),
        in_specs=[
            pl.BlockSpec((1, 1, TM), lambda i, *_: (i, 0, 0), memory_space=pltpu.SMEM),
            pl.BlockSpec((1, 1, TM), lambda i, *_: (jnp.minimum(i + 1, n_tiles - 1), 0, 0),
                         memory_space=pltpu.SMEM),
            pl.BlockSpec(memory_space=pl.ANY),
            wspec((D_MODEL, D_FF), 0), wspec((D_MODEL, D_FF), 0), wspec((D_FF, D_MODEL), 0),
            wspec((D_MODEL, D_FF), 1), wspec((D_MODEL, D_FF), 1), wspec((D_FF, D_MODEL), 1),
            pl.BlockSpec((1, D_MODEL), lambda i, *_: (0, 0)),
            pl.BlockSpec((1, D_MODEL), lambda i, *_: (0, 0)),
        ],
        out_specs=pl.BlockSpec(memory_space=pl.ANY),
        scratch_shapes=[
            pltpu.VMEM((2, TM, D_MODEL + EXT), _F32),
            pltpu.VMEM((2, TM, D_MODEL), _F32),
            pltpu.SemaphoreType.DMA((2,)),
            pltpu.SemaphoreType.DMA((2,)),
            pltpu.SMEM((2,), jnp.int32),
        ],
    )
    inv3 = inv.reshape(n_tiles, 1, TM)
    return pl.pallas_call(
        functools.partial(_moe_kernel, alpha, n_tiles),
        grid_spec=grid_spec,
        out_shape=jax.ShapeDtypeStruct((n_tok, D_MODEL), _F32),
        compiler_params=pltpu.CompilerParams(
            dimension_semantics=("arbitrary",), vmem_limit_bytes=VMEM_LIMIT),
        name="moe",
    )(elo, ehi, nvalid, inv3, inv3, xe, wg, wu, wd, wg, wu, wd, g2, b2)


_PAIR_LO = (0, 0, 0, 1, 1, 2)
_PAIR_HI = (1, 2, 3, 2, 3, 3)


def _plan(cls, n_tiles):
    n_tok = cls.shape[0]
    onehot = (cls[:, None] == jnp.arange(N_CLASSES, dtype=jnp.int32)[None, :]).astype(jnp.int32)
    csum = jnp.cumsum(onehot, axis=0)
    counts = csum[-1]
    rank = jnp.sum(csum * onehot, axis=1) - 1
    tiles = (counts + TM - 1) // TM
    tile_end = jnp.cumsum(tiles)
    tile_start = tile_end - tiles
    pos = jnp.sum(onehot * tile_start[None, :], axis=1) * TM + rank
    inv = jnp.zeros((n_tiles * TM,), jnp.int32).at[pos].set(jnp.arange(n_tok, dtype=jnp.int32))
    ti = jnp.arange(n_tiles, dtype=jnp.int32)
    n_used = tile_end[-1]
    tsel = jnp.minimum(ti, n_used - 1)
    tcls = jnp.sum((tile_end[None, :] <= tsel[:, None]).astype(jnp.int32), axis=1)
    nvalid = jnp.clip(counts[tcls] - (ti - tile_start[tcls]) * TM, 0, TM)
    nvalid = jnp.where(ti < n_used, nvalid, 0).astype(jnp.int32)
    grp = tcls // N_PAIRS
    pair = tcls % N_PAIRS
    elo = grp * EXPERTS_PER_GROUP + jnp.asarray(_PAIR_LO, jnp.int32)[pair]
    ehi = grp * EXPERTS_PER_GROUP + jnp.asarray(_PAIR_HI, jnp.int32)[pair]
    return elo.astype(jnp.int32), ehi.astype(jnp.int32), nvalid, inv


def kernel(x, w_in, conv_a_w, w_branch_a, conv_r_w, conv_r_b, w_rg_a, b_rg_a, w_rg_x, b_rg_x, rg_lambda, w_branch_r, w_out, ln1_g, ln1_b, w_router, b_router, w_exp_gate, w_exp_up, w_exp_down, ln2_g, ln2_b):
    n_seq, seq, _ = x.shape
    depth = w_in.shape[0]
    n_tok = n_seq * seq
    assert seq % TT == 0 and n_tok % TM == 0
    n_tiles = n_tok // TM + N_CLASSES
    alpha = (2.0 * depth) ** 0.25

    wrt = w_router.T.astype(_F32)
    wrt_hi = wrt.astype(_BF)
    wrt_lo = (wrt - wrt_hi.astype(_F32)).astype(_BF)
    brt = b_router.astype(_F32).reshape(N_EXPERTS, 1)

    h = x.reshape(n_tok, D_MODEL)
    for l in range(depth):
        mix_w = (
            w_in[l].astype(_BF),
            conv_a_w[l],
            w_branch_a[l].astype(_BF),
            conv_r_w[l],
            conv_r_b[l].reshape(1, D_R),
            jnp.concatenate([w_rg_a[l], w_rg_x[l]], axis=-1).astype(_BF),
            b_rg_a[l].reshape(1, D_R),
            b_rg_x[l].reshape(1, D_R),
            rg_lambda[l].reshape(1, D_R),
            w_branch_r[l].astype(_BF),
            w_out[l].astype(_BF),
            ln1_g[l].reshape(1, D_MODEL),
            ln1_b[l].reshape(1, D_MODEL),
            wrt_hi, wrt_lo, brt,
        )
        xe, route = _mix_call(alpha, h, n_seq, mix_w)
        plan = _plan(route[0].astype(jnp.int32), n_tiles)
        moe_w = (
            w_exp_gate[l].astype(_BF), w_exp_up[l].astype(_BF), w_exp_down[l].astype(_BF),
            ln2_g[l].reshape(1, D_MODEL), ln2_b[l].reshape(1, D_MODEL),
        )
        h = _moe_call(alpha, xe, plan, moe_w)
    return h.reshape(n_seq, seq, D_MODEL)
```

```python
import functools

import jax
import jax.numpy as jnp
from jax import lax
from jax.experimental import pallas as pl
from jax.experimental.pallas import tpu as pltpu

D_MODEL = 1024
D_A = 1024
D_R = 1280
N_RG_BLOCKS = 10
RG_BLOCK = 128
RG_C = 8.0
D_IN = 3 * D_A + 2 * D_R + 2 * D_MODEL
N_EXPERTS = 16
N_GROUPS = 4
EXPERTS_PER_GROUP = 4
N_PAIRS = 6
N_CLASSES = N_GROUPS * N_PAIRS
D_FF = 512
LN_EPS = 1e-5

OFF_HA, OFF_BA, OFF_CA = 0, D_A, 2 * D_A
OFF_XR = 3 * D_A
OFF_YR = OFF_XR + D_R
OFF_GA = OFF_YR + D_R
OFF_GR = OFF_GA + D_MODEL

SUBLANES = 8
LANES = 128
EXT = LANES
X_PITCH = D_MODEL // LANES
XE_PITCH = (D_MODEL + EXT) // LANES
TT = 256
TM = 256
DMA_UNROLL = 8
VMEM_LIMIT = 56 * 1024 * 1024

_BF = jnp.bfloat16
_F32 = jnp.float32


def _sigmoid(v):
    return 0.5 * jnp.tanh(0.5 * v) + 0.5


def _gelu_tanh(v):
    c = 0.7978845608028654
    return 0.5 * v * (1.0 + jnp.tanh(c * (v + 0.044715 * (v * v * v))))


def _layer_norm(y, g, b):
    mu = jnp.mean(y, axis=-1, keepdims=True)
    yc = y - mu
    var = jnp.mean(yc * yc, axis=-1, keepdims=True)
    return yc * lax.rsqrt(var + LN_EPS) * g + b


def _first_index_of(vals, target):
    idx = jnp.full(target.shape, float(len(vals) - 1), _F32)
    for j in range(len(vals) - 2, -1, -1):
        idx = jnp.where(vals[j] == target, float(j), idx)
    return idx


def _route(logits_t):
    m = jnp.max(logits_t, axis=0, keepdims=True)
    e = jnp.exp(logits_t - m)
    p = e / jnp.sum(e, axis=0, keepdims=True)
    neg = jnp.full((1, logits_t.shape[1]), -jnp.inf, _F32)
    scores, m1s, i1s, m2s, i2s = [], [], [], [], []
    for g in range(N_GROUPS):
        v = [p[g * EXPERTS_PER_GROUP + j:g * EXPERTS_PER_GROUP + j + 1, :] for j in range(EXPERTS_PER_GROUP)]
        m1 = jnp.maximum(jnp.maximum(v[0], v[1]), jnp.maximum(v[2], v[3]))
        i1 = _first_index_of(v, m1)
        w = [jnp.where(i1 == float(j), neg, v[j]) for j in range(EXPERTS_PER_GROUP)]
        m2 = jnp.maximum(jnp.maximum(w[0], w[1]), jnp.maximum(w[2], w[3]))
        i2 = _first_index_of(w, m2)
        scores.append(m1 + m2)
        m1s.append(m1); i1s.append(i1); m2s.append(m2); i2s.append(i2)
    best = jnp.maximum(jnp.maximum(scores[0], scores[1]), jnp.maximum(scores[2], scores[3]))
    gsel = _first_index_of(scores, best)

    def pick(xs):
        out = xs[N_GROUPS - 1]
        for g in range(N_GROUPS - 2, -1, -1):
            out = jnp.where(gsel == float(g), xs[g], out)
        return out

    m1, i1, m2, i2 = pick(m1s), pick(i1s), pick(m2s), pick(i2s)
    den = m1 + m2
    gate1, gate2 = m1 / den, m2 / den
    first_is_lo = i1 < i2
    lo = jnp.where(first_is_lo, i1, i2)
    hi = jnp.where(first_is_lo, i2, i1)
    g_lo = jnp.where(first_is_lo, gate1, gate2)
    g_hi = jnp.where(first_is_lo, gate2, gate1)
    pair = lo * (7.0 - lo) * 0.5 + (hi - lo - 1.0)
    cls = gsel * float(N_PAIRS) + pair
    return cls, g_lo, g_hi


def _load_token_major(ref, n_rows, pitch, lead=()):
    cols = [ref[lead + (pl.ds(c, n_rows, stride=pitch), slice(None))] for c in range(D_MODEL // LANES)]
    return jnp.concatenate(cols, axis=1)


def _store_token_major(ref, val, pitch, lead=()):
    n_rows = val.shape[0]
    for c in range(val.shape[1] // LANES):
        ref[lead + (pl.ds(c, n_rows, stride=pitch), slice(None))] = val[:, c * LANES:(c + 1) * LANES]


def _mix_kernel(alpha, x_token_major,
                x_ref, win_ref, caw_ref, wa_ref, crw_ref, crb_ref, wrg_ref, brga_ref, brgx_ref,
                lam_ref, wr_ref, wo_ref, g1_ref, b1_ref, wrt_hi_ref, wrt_lo_ref, brt_ref,
                xe_ref, route_ref,
                ubuf, xrbuf, abuf, bbuf, hbuf, hcarry):
    t = pl.program_id(1)
    tt = route_ref.shape[1]

    @pl.when(t == 0)
    def _():
        ubuf[pl.ds(0, SUBLANES), :] = jnp.zeros((SUBLANES, D_A), _F32)
        xrbuf[pl.ds(0, SUBLANES), :] = jnp.zeros((SUBLANES, D_R), _F32)
        hcarry[...] = jnp.zeros_like(hcarry)

    @pl.when(t > 0)
    def _():
        ubuf[pl.ds(0, SUBLANES), :] = ubuf[pl.ds(tt, SUBLANES), :]
        xrbuf[pl.ds(0, SUBLANES), :] = xrbuf[pl.ds(tt, SUBLANES), :]

    x = _load_token_major(x_ref, tt, X_PITCH) if x_token_major else x_ref[...]
    xb = x.astype(_BF)

    def proj(off, width):
        return jnp.dot(xb, win_ref[:, off:off + width], preferred_element_type=_F32)

    ubuf[pl.ds(SUBLANES, tt), :] = proj(OFF_CA, D_A) * proj(OFF_HA, D_A)
    conv_a = (ubuf[pl.ds(SUBLANES - 2, tt), :] * caw_ref[0:1, :]
              + ubuf[pl.ds(SUBLANES - 1, tt), :] * caw_ref[1:2, :]
              + ubuf[pl.ds(SUBLANES, tt), :] * caw_ref[2:3, :])
    va = (proj(OFF_BA, D_A) * conv_a).astype(_BF)
    ya = jnp.dot(va, wa_ref[...], preferred_element_type=_F32)

    xrbuf[pl.ds(SUBLANES, tt), :] = proj(OFF_XR, D_R)
    xc = (xrbuf[pl.ds(SUBLANES - 3, tt), :] * crw_ref[0:1, :]
          + xrbuf[pl.ds(SUBLANES - 2, tt), :] * crw_ref[1:2, :]
          + xrbuf[pl.ds(SUBLANES - 1, tt), :] * crw_ref[2:3, :]
          + xrbuf[pl.ds(SUBLANES, tt), :] * crw_ref[3:4, :]
          + crb_ref[...])
    xcb = xc.astype(_BF)
    z = -lam_ref[...]
    softplus_neg_lam = jnp.maximum(z, 0.0) + jnp.log1p(jnp.exp(-jnp.abs(z)))
    row = lax.broadcasted_iota(jnp.int32, (tt // SUBLANES, SUBLANES, RG_BLOCK), 1)
    for h in range(N_RG_BLOCKS):
        sl = slice(h * RG_BLOCK, (h + 1) * RG_BLOCK)
        gh = jnp.dot(xcb[:, sl], wrg_ref[h], preferred_element_type=_F32)
        r_gate = _sigmoid(gh[:, :RG_BLOCK] + brga_ref[:, sl])
        i_gate = _sigmoid(gh[:, RG_BLOCK:] + brgx_ref[:, sl])
        log_a = (-RG_C) * r_gate * softplus_neg_lam[:, sl]
        a_t = jnp.exp(log_a)
        b_t = jnp.sqrt(1.0 - a_t * a_t) * (i_gate * xc[:, sl])
        a3 = a_t.reshape(tt // SUBLANES, SUBLANES, RG_BLOCK)
        b3 = b_t.reshape(tt // SUBLANES, SUBLANES, RG_BLOCK)
        for k in (1, 2, 4):
            keep = row >= k
            b3 = jnp.where(keep, a3 * pltpu.roll(b3, k, 1) + b3, b3)
            a3 = jnp.where(keep, a3 * pltpu.roll(a3, k, 1), a3)
        abuf[:, sl] = a3.reshape(tt, RG_BLOCK)
        bbuf[:, sl] = b3.reshape(tt, RG_BLOCK)

    def carry_groups(g, h_prev):
        o = pl.multiple_of(g * SUBLANES, SUBLANES)
        hg = abuf[pl.ds(o, SUBLANES), :] * h_prev + bbuf[pl.ds(o, SUBLANES), :]
        hbuf[pl.ds(o, SUBLANES), :] = hg
        return hg[SUBLANES - 1:SUBLANES, :]

    hcarry[...] = lax.fori_loop(0, tt // SUBLANES, carry_groups, hcarry[...])
    vr = (hbuf[...] * _gelu_tanh(proj(OFF_YR, D_R))).astype(_BF)
    yr = jnp.dot(vr, wr_ref[...], preferred_element_type=_F32)

    merged = _sigmoid(proj(OFF_GA, D_MODEL)) * ya + _sigmoid(proj(OFF_GR, D_MODEL)) * yr
    o = jnp.dot(merged.astype(_BF), wo_ref[...], preferred_element_type=_F32)
    x1 = _layer_norm(alpha * x + o, g1_ref[...], b1_ref[...])

    x1_hi = x1.astype(_BF)
    x1_lo = (x1 - x1_hi.astype(_F32)).astype(_BF)
    nt_dims = (((1,), (1,)), ((), ()))
    logits_t = (lax.dot_general(wrt_hi_ref[...], x1_hi, nt_dims, preferred_element_type=_F32)
                + lax.dot_general(wrt_hi_ref[...], x1_lo, nt_dims, preferred_element_type=_F32)
                + lax.dot_general(wrt_lo_ref[...], x1_hi, nt_dims, preferred_element_type=_F32)
                + brt_ref[...])
    cls, g_lo, g_hi = _route(logits_t)
    rsel = lax.broadcasted_iota(jnp.int32, (SUBLANES, tt), 0)
    route_ref[...] = jnp.where(rsel == 0, cls, jnp.where(rsel == 1, g_lo, jnp.where(rsel == 2, g_hi, 0.0)))
    esel = lax.broadcasted_iota(jnp.int32, (EXT, tt), 0)
    ext_t = jnp.where(esel == 0, cls, jnp.where(esel == 1, g_lo, jnp.where(esel == 2, g_hi, 0.0)))
    _store_token_major(xe_ref, jnp.concatenate([x1, ext_t.T], axis=1), XE_PITCH)


def _mix_call(alpha, x2d, x_token_major, n_seq, n_tok, w):
    seq = n_tok // n_seq
    nt = seq // TT

    def const(shape):
        nd = len(shape)
        return pl.BlockSpec(shape, lambda b, t, _nd=nd: (0,) * _nd, pipeline_mode=pl.Buffered(1))

    in_specs = [
        (pl.BlockSpec((TT * X_PITCH, LANES), lambda b, t: (b * nt + t, 0)) if x_token_major
         else pl.BlockSpec((TT, D_MODEL), lambda b, t: (b * nt + t, 0))),
        const((D_MODEL, D_IN)),
        const((3, D_A)),
        const((D_A, D_MODEL)),
        const((4, D_R)),
        const((1, D_R)),
        const((N_RG_BLOCKS, RG_BLOCK, 2 * RG_BLOCK)),
        const((1, D_R)),
        const((1, D_R)),
        const((1, D_R)),
        const((D_R, D_MODEL)),
        const((D_MODEL, D_MODEL)),
        const((1, D_MODEL)),
        const((1, D_MODEL)),
        const((N_EXPERTS, D_MODEL)),
        const((N_EXPERTS, D_MODEL)),
        const((N_EXPERTS, 1)),
    ]
    out_specs = [
        pl.BlockSpec((TT * XE_PITCH, LANES), lambda b, t: (b * nt + t, 0)),
        pl.BlockSpec((SUBLANES, TT), lambda b, t: (0, b * nt + t)),
    ]
    out_shape = [
        jax.ShapeDtypeStruct((n_tok * XE_PITCH, LANES), _F32),
        jax.ShapeDtypeStruct((SUBLANES, n_tok), _F32),
    ]
    scratch = [
        pltpu.VMEM((TT + SUBLANES, D_A), _F32),
        pltpu.VMEM((TT + SUBLANES, D_R), _F32),
        pltpu.VMEM((TT, D_R), _F32),
        pltpu.VMEM((TT, D_R), _F32),
        pltpu.VMEM((TT, D_R), _F32),
        pltpu.VMEM((1, D_R), _F32),
    ]
    return pl.pallas_call(
        functools.partial(_mix_kernel, alpha, x_token_major),
        grid=(n_seq, nt),
        in_specs=in_specs,
        out_specs=out_specs,
        out_shape=out_shape,
        scratch_shapes=scratch,
        compiler_params=pltpu.CompilerParams(
            dimension_semantics=("arbitrary", "arbitrary"), vmem_limit_bytes=VMEM_LIMIT),
        name="mix",
    )(x2d, *w)


def _moe_kernel(alpha, n_tiles, out_token_major,
                elo_ref, ehi_ref, nv_ref,
                inv_cur, inv_nxt, xe_hbm,
                wg_lo, wu_lo, wd_lo, wg_hi, wu_hi, wd_hi, g2_ref, b2_ref,
                out_hbm,
                gbuf, obuf, gsem, ssem, pend):
    i = pl.program_id(0)
    nv = nv_ref[i]
    o_pitch = X_PITCH if out_token_major else 1

    def gather_copy(tok, r, s):
        return pltpu.make_async_copy(xe_hbm.at[pl.ds(tok * XE_PITCH, XE_PITCH), :],
                                     gbuf.at[s, pl.ds(r * XE_PITCH, XE_PITCH), :], gsem.at[s])

    def scatter_copy(tok, r, s, n_tok=1):
        src = r * o_pitch if isinstance(r, int) else pl.multiple_of(r * o_pitch, o_pitch)
        return pltpu.make_async_copy(obuf.at[s, pl.ds(src, n_tok * o_pitch), :],
                                     out_hbm.at[pl.ds(tok * o_pitch, n_tok * o_pitch), :], ssem.at[s])

    def start_gather(idx_ref, s):
        for r in range(TM):
            gather_copy(idx_ref[0, 0, r], r, s).start()

    def start_scatter(idx_ref, s, n_rows):
        n_chunks = lax.shift_right_logical(n_rows, DMA_UNROLL.bit_length() - 1)

        def chunk(c, carry):
            base = c * DMA_UNROLL
            for j in range(DMA_UNROLL):
                scatter_copy(idx_ref[0, 0, base + j], base + j, s).start()
            return carry
        lax.fori_loop(0, n_chunks, chunk, 0)

        def tail(r, carry):
            scatter_copy(idx_ref[0, 0, r], r, s).start()
            return carry
        lax.fori_loop(n_chunks * DMA_UNROLL, n_rows, tail, 0)

    def wait_scatter(s):
        n_rows = pend[s]
        width = TM
        while width >= 1:
            @pl.when(jnp.bitwise_and(n_rows, width) != 0)
            def _(width=width):
                scatter_copy(0, 0, s, n_tok=width).wait()
            width //= 2
        pend[s] = 0

    @pl.when(i == 0)
    def _():
        pend[0] = 0
        pend[1] = 0
        start_gather(inv_cur, 0)

    nxt = jnp.minimum(i + 1, n_tiles - 1)
    next_valid = jnp.logical_and(i + 1 < n_tiles, nv_ref[nxt] > 0)

    def tile_body(s):
        pltpu.make_async_copy(xe_hbm.at[pl.ds(0, TM * XE_PITCH), :], gbuf.at[s], gsem.at[s]).wait()
        wait_scatter(s)

        @pl.when(next_valid)
        def _():
            start_gather(inv_nxt, 1 - s)

        xf = _load_token_major(gbuf, TM, XE_PITCH, lead=(s,))
        ext = gbuf[s, pl.ds(D_MODEL // LANES, TM, stride=XE_PITCH), :]
        gate_lo = ext[:, 1:2]
        gate_hi = ext[:, 2:3]
        xb = xf.astype(_BF)

        def expert(wg, wu, wd, gate):
            a = jnp.dot(xb, wg[...], preferred_element_type=_F32)
            u = jnp.dot(xb, wu[...], preferred_element_type=_F32)
            hgt = ((a * _sigmoid(a)) * u * gate).astype(_BF)
            return jnp.dot(hgt, wd[...], preferred_element_type=_F32)

        moe = expert(wg_lo, wu_lo, wd_lo, gate_lo) + expert(wg_hi, wu_hi, wd_hi, gate_hi)
        y = _layer_norm(alpha * xf + moe, g2_ref[...], b2_ref[...])
        if out_token_major:
            _store_token_major(obuf, y, X_PITCH, lead=(s,))
        else:
            obuf[s] = y
        start_scatter(inv_cur, s, nv)
        pend[s] = nv

    for s in range(2):
        @pl.when(jnp.logical_and(nv > 0, i % 2 == s))
        def _(s=s):
            tile_body(s)

    @pl.when(i == n_tiles - 1)
    def _():
        wait_scatter(0)
        wait_scatter(1)


def _moe_call(alpha, xe, plan, w, n_tok, out_token_major):
    elo, ehi, nvalid, inv = plan
    n_tiles = elo.shape[0]
    wg, wu, wd, g2, b2 = w
    if out_token_major:
        obuf_shape, out_rows, out_cols = (2, TM * X_PITCH, LANES), n_tok * X_PITCH, LANES
    else:
        obuf_shape, out_rows, out_cols = (2, TM, D_MODEL), n_tok, D_MODEL

    def wspec(shape, which):
        def imap(i, elo_r, ehi_r, nv_r):
            return ((elo_r, ehi_r)[which][i], 0, 0)
        return pl.BlockSpec((None,) + shape, imap)

    grid_spec = pltpu.PrefetchScalarGridSpec(
        num_scalar_prefetch=3,
        grid=(n_tiles,),
        in_specs=[
            pl.BlockSpec((1, 1, TM), lambda i, *_: (i, 0, 0), memory_space=pltpu.SMEM),
            pl.BlockSpec((1, 1, TM), lambda i, *_: (jnp.minimum(i + 1, n_tiles - 1), 0, 0),
                         memory_space=pltpu.SMEM),
            pl.BlockSpec(memory_space=pl.ANY),
            wspec((D_MODEL, D_FF), 0), wspec((D_MODEL, D_FF), 0), wspec((D_FF, D_MODEL), 0),
            wspec((D_MODEL, D_FF), 1), wspec((D_MODEL, D_FF), 1), wspec((D_FF, D_MODEL), 1),
            pl.BlockSpec((1, D_MODEL), lambda i, *_: (0, 0)),
            pl.BlockSpec((1, D_MODEL), lambda i, *_: (0, 0)),
        ],
        out_specs=pl.BlockSpec(memory_space=pl.ANY),
        scratch_shapes=[
            pltpu.VMEM((2, TM * XE_PITCH, LANES), _F32),
            pltpu.VMEM(obuf_shape, _F32),
            pltpu.SemaphoreType.DMA((2,)),
            pltpu.SemaphoreType.DMA((2,)),
            pltpu.SMEM((2,), jnp.int32),
        ],
    )
    inv3 = inv.reshape(n_tiles, 1, TM)
    return pl.pallas_call(
        functools.partial(_moe_kernel, alpha, n_tiles, out_token_major),
        grid_spec=grid_spec,
        out_shape=jax.ShapeDtypeStruct((out_rows, out_cols), _F32),
        compiler_params=pltpu.CompilerParams(
            dimension_semantics=("arbitrary",), vmem_limit_bytes=VMEM_LIMIT),
        name="moe",
    )(elo, ehi, nvalid, inv3, inv3, xe, wg, wu, wd, wg, wu, wd, g2, b2)


_PAIR_LO = (0, 0, 0, 1, 1, 2)
_PAIR_HI = (1, 2, 3, 2, 3, 3)


def _plan(cls, n_tiles):
    n_tok = cls.shape[0]
    onehot = (cls[:, None] == jnp.arange(N_CLASSES, dtype=jnp.int32)[None, :]).astype(jnp.int32)
    csum = jnp.cumsum(onehot, axis=0)
    counts = csum[-1]
    rank = jnp.sum(csum * onehot, axis=1) - 1
    tiles = (counts + TM - 1) // TM
    tile_end = jnp.cumsum(tiles)
    tile_start = tile_end - tiles
    pos = jnp.sum(onehot * tile_start[None, :], axis=1) * TM + rank
    inv = jnp.zeros((n_tiles * TM,), jnp.int32).at[pos].set(jnp.arange(n_tok, dtype=jnp.int32))
    ti = jnp.arange(n_tiles, dtype=jnp.int32)
    n_used = tile_end[-1]
    tsel = jnp.minimum(ti, n_used - 1)
    tcls = jnp.sum((tile_end[None, :] <= tsel[:, None]).astype(jnp.int32), axis=1)
    nvalid = jnp.clip(counts[tcls] - (ti - tile_start[tcls]) * TM, 0, TM)
    nvalid = jnp.where(ti < n_used, nvalid, 0).astype(jnp.int32)
    grp = tcls // N_PAIRS
    pair = tcls % N_PAIRS
    elo = grp * EXPERTS_PER_GROUP + jnp.asarray(_PAIR_LO, jnp.int32)[pair]
    ehi = grp * EXPERTS_PER_GROUP + jnp.asarray(_PAIR_HI, jnp.int32)[pair]
    return elo.astype(jnp.int32), ehi.astype(jnp.int32), nvalid, inv


def kernel(x, w_in, conv_a_w, w_branch_a, conv_r_w, conv_r_b, w_rg_a, b_rg_a, w_rg_x, b_rg_x, rg_lambda, w_branch_r, w_out, ln1_g, ln1_b, w_router, b_router, w_exp_gate, w_exp_up, w_exp_down, ln2_g, ln2_b):
    n_seq, seq, _ = x.shape
    depth = w_in.shape[0]
    n_tok = n_seq * seq
    assert seq % TT == 0 and n_tok % TM == 0
    n_tiles = n_tok // TM + N_CLASSES
    alpha = (2.0 * depth) ** 0.25

    wrt = w_router.T.astype(_F32)
    wrt_hi = wrt.astype(_BF)
    wrt_lo = (wrt - wrt_hi.astype(_F32)).astype(_BF)
    brt = b_router.astype(_F32).reshape(N_EXPERTS, 1)

    h = x.reshape(n_tok, D_MODEL)
    for l in range(depth):
        mix_w = (
            w_in[l].astype(_BF),
            conv_a_w[l],
            w_branch_a[l].astype(_BF),
            conv_r_w[l],
            conv_r_b[l].reshape(1, D_R),
            jnp.concatenate([w_rg_a[l], w_rg_x[l]], axis=-1).astype(_BF),
            b_rg_a[l].reshape(1, D_R),
            b_rg_x[l].reshape(1, D_R),
            rg_lambda[l].reshape(1, D_R),
            w_branch_r[l].astype(_BF),
            w_out[l].astype(_BF),
            ln1_g[l].reshape(1, D_MODEL),
            ln1_b[l].reshape(1, D_MODEL),
            wrt_hi, wrt_lo, brt,
        )
        xe, route = _mix_call(alpha, h, l > 0, n_seq, n_tok, mix_w)
        plan = _plan(route[0].astype(jnp.int32), n_tiles)
        moe_w = (
            w_exp_gate[l].astype(_BF), w_exp_up[l].astype(_BF), w_exp_down[l].astype(_BF),
            ln2_g[l].reshape(1, D_MODEL), ln2_b[l].reshape(1, D_MODEL),
        )
        h = _moe_call(alpha, xe, plan, moe_w, n_tok, l < depth - 1)
    return h.reshape(n_seq, seq, D_MODEL)
```

```python
import functools

import jax
import jax.numpy as jnp
from jax import lax
from jax.experimental import pallas as pl
from jax.experimental.pallas import tpu as pltpu

D_MODEL = 1024
D_A = 1024
D_R = 1280
N_RG_BLOCKS = 10
RG_BLOCK = 128
RG_C = 8.0
CONV_A_WIDTH = 3
CONV_R_WIDTH = 4
D_IN = 3 * D_A + 2 * D_R + 2 * D_MODEL
N_EXPERTS = 16
N_GROUPS = 4
EXPERTS_PER_GROUP = 4
N_PAIRS = 6
N_CLASSES = N_GROUPS * N_PAIRS
D_FF = 512
LN_EPS = 1e-5

OFF_HA, OFF_BA, OFF_CA = 0, D_A, 2 * D_A
OFF_XR = 3 * D_A
OFF_YR = OFF_XR + D_R
OFF_GA = OFF_YR + D_R
OFF_GR = OFF_GA + D_MODEL

SUBLANES = 8
LANES = 128
EXT = LANES
X_PITCH = D_MODEL // LANES
XE_PITCH = (D_MODEL + EXT) // LANES
TT = 512
TM = 256
DMA_UNROLL = 8
VMEM_LIMIT = 56 * 1024 * 1024

_BF = jnp.bfloat16
_F32 = jnp.float32

GELU_K1 = 0.7978845608028654
GELU_K2 = GELU_K1 * 0.044715


def _layer_norm(y, g, b):
    mu = jnp.mean(y, axis=-1, keepdims=True)
    yc = y - mu
    var = jnp.mean(yc * yc, axis=-1, keepdims=True)
    return yc * lax.rsqrt(var + LN_EPS) * g + b


def _first_index_of(vals, target):
    idx = jnp.full(target.shape, float(len(vals) - 1), _F32)
    for j in range(len(vals) - 2, -1, -1):
        idx = jnp.where(vals[j] == target, float(j), idx)
    return idx


def _route(logits_t):
    m = jnp.max(logits_t, axis=0, keepdims=True)
    e = jnp.exp(logits_t - m)
    p = e / jnp.sum(e, axis=0, keepdims=True)
    neg = jnp.full((1, logits_t.shape[1]), -jnp.inf, _F32)
    scores, m1s, i1s, m2s, i2s = [], [], [], [], []
    for g in range(N_GROUPS):
        v = [p[g * EXPERTS_PER_GROUP + j:g * EXPERTS_PER_GROUP + j + 1, :] for j in range(EXPERTS_PER_GROUP)]
        m1 = jnp.maximum(jnp.maximum(v[0], v[1]), jnp.maximum(v[2], v[3]))
        i1 = _first_index_of(v, m1)
        w = [jnp.where(i1 == float(j), neg, v[j]) for j in range(EXPERTS_PER_GROUP)]
        m2 = jnp.maximum(jnp.maximum(w[0], w[1]), jnp.maximum(w[2], w[3]))
        i2 = _first_index_of(w, m2)
        scores.append(m1 + m2)
        m1s.append(m1); i1s.append(i1); m2s.append(m2); i2s.append(i2)
    best = jnp.maximum(jnp.maximum(scores[0], scores[1]), jnp.maximum(scores[2], scores[3]))
    gsel = _first_index_of(scores, best)

    def pick(xs):
        out = xs[N_GROUPS - 1]
        for g in range(N_GROUPS - 2, -1, -1):
            out = jnp.where(gsel == float(g), xs[g], out)
        return out

    m1, i1, m2, i2 = pick(m1s), pick(i1s), pick(m2s), pick(i2s)
    den = m1 + m2
    gate1, gate2 = m1 / den, m2 / den
    first_is_lo = i1 < i2
    lo = jnp.where(first_is_lo, i1, i2)
    hi = jnp.where(first_is_lo, i2, i1)
    g_lo = jnp.where(first_is_lo, gate1, gate2)
    g_hi = jnp.where(first_is_lo, gate2, gate1)
    pair = lo * (7.0 - lo) * 0.5 + (hi - lo - 1.0)
    cls = gsel * float(N_PAIRS) + pair
    return cls, g_lo, g_hi


def _load_token_major(ref, n_rows, pitch, lead=()):
    cols = [ref[lead + (pl.ds(c, n_rows, stride=pitch), slice(None))] for c in range(D_MODEL // LANES)]
    return jnp.concatenate(cols, axis=1)


def _store_token_major(ref, val, pitch, lead=()):
    n_rows = val.shape[0]
    for c in range(val.shape[1] // LANES):
        ref[lead + (pl.ds(c, n_rows, stride=pitch), slice(None))] = val[:, c * LANES:(c + 1) * LANES]


def _causal_conv(u, hist_ref, w_ref, n_taps, out_ref, bias=None):
    tt = u.shape[0]

    def tap0(val):
        out = val * w_ref[n_taps - 1:n_taps, :]
        return out if bias is None else out + bias

    acc = tap0(u)
    for d in range(1, n_taps):
        acc = acc + pltpu.roll(u, d, 0) * w_ref[n_taps - 1 - d:n_taps - d, :]
    out_ref[...] = acc
    head = u[0:SUBLANES, :]
    hist = hist_ref[...]
    row = lax.broadcasted_iota(jnp.int32, head.shape, 0)
    acc_head = tap0(head)
    for d in range(1, n_taps):
        shifted = jnp.where(row < d, pltpu.roll(hist, d, 0), pltpu.roll(head, d, 0))
        acc_head = acc_head + shifted * w_ref[n_taps - 1 - d:n_taps - d, :]
    out_ref[pl.ds(0, SUBLANES), :] = acc_head
    hist_ref[...] = u[tt - SUBLANES:tt, :]


def _mix_kernel(alpha, x_token_major,
                x_ref, win_ref, caw_ref, wa_ref, crw_ref, crb_ref, wrg_ref, brga_ref, brgx_ref,
                lam_ref, wr_ref, wo_ref, g1_ref, b1_ref, wrt_hi_ref, wrt_lo_ref, brt_ref,
                xe_ref, route_ref,
                uhist, xrhist, cabuf, xcbuf, abuf, bbuf, hbuf, hcarry):
    t = pl.program_id(1)
    tt = route_ref.shape[1]

    @pl.when(t == 0)
    def _():
        uhist[...] = jnp.zeros_like(uhist)
        xrhist[...] = jnp.zeros_like(xrhist)
        hcarry[...] = jnp.zeros_like(hcarry)

    x = _load_token_major(x_ref, tt, X_PITCH) if x_token_major else x_ref[...]
    xb = x.astype(_BF)

    def proj(off, width):
        return jnp.dot(xb, win_ref[:, off:off + width], preferred_element_type=_F32)

    _causal_conv(proj(OFF_CA, D_A) * proj(OFF_HA, D_A), uhist, caw_ref, CONV_A_WIDTH, cabuf)
    va = (proj(OFF_BA, D_A) * cabuf[...]).astype(_BF)
    ya = jnp.dot(va, wa_ref[...], preferred_element_type=_F32)

    _causal_conv(proj(OFF_XR, D_R), xrhist, crw_ref, CONV_R_WIDTH, xcbuf, bias=crb_ref[...])
    z = -lam_ref[...]
    softplus_neg_lam = jnp.maximum(z, 0.0) + jnp.log1p(jnp.exp(-jnp.abs(z)))
    c_half = (-0.5 * RG_C) * softplus_neg_lam
    row = lax.broadcasted_iota(jnp.int32, (tt // SUBLANES, SUBLANES, RG_BLOCK), 1)
    for h in range(N_RG_BLOCKS):
        sl = slice(h * RG_BLOCK, (h + 1) * RG_BLOCK)
        xc = xcbuf[:, sl]
        gh = jnp.dot(xc.astype(_BF), wrg_ref[h], preferred_element_type=_F32)
        t_r = jnp.tanh(gh[:, :RG_BLOCK] + brga_ref[:, sl])
        t_i = jnp.tanh(gh[:, RG_BLOCK:] + brgx_ref[:, sl])
        log_a = c_half[:, sl] * t_r + c_half[:, sl]
        a_t = jnp.exp(log_a)
        b_t = jnp.sqrt(1.0 - a_t * a_t) * (t_i * xc + xc)
        a3 = a_t.reshape(tt // SUBLANES, SUBLANES, RG_BLOCK)
        b3 = b_t.reshape(tt // SUBLANES, SUBLANES, RG_BLOCK)
        for k in (1, 2, 4):
            keep = row >= k
            b3 = jnp.where(keep, a3 * pltpu.roll(b3, k, 1) + b3, b3)
            a3 = jnp.where(keep, a3 * pltpu.roll(a3, k, 1), a3)
        abuf[:, sl] = a3.reshape(tt, RG_BLOCK)
        bbuf[:, sl] = b3.reshape(tt, RG_BLOCK)

    def carry_groups(g, h_prev):
        o = pl.multiple_of(g * SUBLANES, SUBLANES)
        hg = abuf[pl.ds(o, SUBLANES), :] * h_prev + bbuf[pl.ds(o, SUBLANES), :]
        hbuf[pl.ds(o, SUBLANES), :] = hg
        return hg[SUBLANES - 1:SUBLANES, :]

    hcarry[...] = lax.fori_loop(0, tt // SUBLANES, carry_groups, hcarry[...])
    v = proj(OFF_YR, D_R)
    th = jnp.tanh(v * (GELU_K2 * (v * v) + GELU_K1))
    vr = (hbuf[...] * (v * th + v)).astype(_BF)
    yr = jnp.dot(vr, wr_ref[...], preferred_element_type=_F32)

    t_a = jnp.tanh(proj(OFF_GA, D_MODEL))
    t_g = jnp.tanh(proj(OFF_GR, D_MODEL))
    merged2 = (t_a * ya + ya) + (t_g * yr + yr)
    o = jnp.dot(merged2.astype(_BF), wo_ref[...], preferred_element_type=_F32)
    x1 = _layer_norm(alpha * x + o, g1_ref[...], b1_ref[...])

    x1_hi = x1.astype(_BF)
    x1_lo = (x1 - x1_hi.astype(_F32)).astype(_BF)
    nt_dims = (((1,), (1,)), ((), ()))
    logits_t = (lax.dot_general(wrt_hi_ref[...], x1_hi, nt_dims, preferred_element_type=_F32)
                + lax.dot_general(wrt_hi_ref[...], x1_lo, nt_dims, preferred_element_type=_F32)
                + lax.dot_general(wrt_lo_ref[...], x1_hi, nt_dims, preferred_element_type=_F32)
                + brt_ref[...])
    cls, g_lo, g_hi = _route(logits_t)
    rsel = lax.broadcasted_iota(jnp.int32, (SUBLANES, tt), 0)
    route_ref[...] = jnp.where(rsel == 0, cls, jnp.where(rsel == 1, g_lo, jnp.where(rsel == 2, g_hi, 0.0)))
    esel = lax.broadcasted_iota(jnp.int32, (EXT, tt), 0)
    ext_t = jnp.where(esel == 0, cls, jnp.where(esel == 1, g_lo, jnp.where(esel == 2, g_hi, 0.0)))
    _store_token_major(xe_ref, jnp.concatenate([x1, ext_t.T], axis=1), XE_PITCH)


def _mix_call(alpha, layer, x2d, x_token_major, n_seq, n_tok, w):
    seq = n_tok // n_seq
    nt = seq // TT

    def layer_const(shape):
        zeros = (0,) * len(shape)
        return pl.BlockSpec((None,) + shape, lambda b, t: (layer,) + zeros, pipeline_mode=pl.Buffered(1))

    def const(shape):
        zeros = (0,) * len(shape)
        return pl.BlockSpec(shape, lambda b, t: zeros, pipeline_mode=pl.Buffered(1))

    in_specs = [
        (pl.BlockSpec((TT * X_PITCH, LANES), lambda b, t: (b * nt + t, 0)) if x_token_major
         else pl.BlockSpec((TT, D_MODEL), lambda b, t: (b * nt + t, 0))),
        layer_const((D_MODEL, D_IN)),
        layer_const((CONV_A_WIDTH, D_A)),
        layer_const((D_A, D_MODEL)),
        layer_const((CONV_R_WIDTH, D_R)),
        layer_const((1, D_R)),
        layer_const((N_RG_BLOCKS, RG_BLOCK, 2 * RG_BLOCK)),
        layer_const((1, D_R)),
        layer_const((1, D_R)),
        layer_const((1, D_R)),
        layer_const((D_R, D_MODEL)),
        layer_const((D_MODEL, D_MODEL)),
        layer_const((1, D_MODEL)),
        layer_const((1, D_MODEL)),
        const((N_EXPERTS, D_MODEL)),
        const((N_EXPERTS, D_MODEL)),
        const((N_EXPERTS, 1)),
    ]
    out_specs = [
        pl.BlockSpec((TT * XE_PITCH, LANES), lambda b, t: (b * nt + t, 0)),
        pl.BlockSpec((SUBLANES, TT), lambda b, t: (0, b * nt + t)),
    ]
    out_shape = [
        jax.ShapeDtypeStruct((n_tok * XE_PITCH, LANES), _F32),
        jax.ShapeDtypeStruct((SUBLANES, n_tok), _F32),
    ]
    scratch = [
        pltpu.VMEM((SUBLANES, D_A), _F32),
        pltpu.VMEM((SUBLANES, D_R), _F32),
        pltpu.VMEM((TT, D_A), _F32),
        pltpu.VMEM((TT, D_R), _F32),
        pltpu.VMEM((TT, D_R), _F32),
        pltpu.VMEM((TT, D_R), _F32),
        pltpu.VMEM((TT, D_R), _F32),
        pltpu.VMEM((1, D_R), _F32),
    ]
    return pl.pallas_call(
        functools.partial(_mix_kernel, alpha, x_token_major),
        grid=(n_seq, nt),
        in_specs=in_specs,
        out_specs=out_specs,
        out_shape=out_shape,
        scratch_shapes=scratch,
        compiler_params=pltpu.CompilerParams(
            dimension_semantics=("arbitrary", "arbitrary"), vmem_limit_bytes=VMEM_LIMIT),
        name="mix",
    )(x2d, *w)


def _moe_kernel(alpha, n_tiles, out_token_major,
                elo_ref, ehi_ref, nv_ref, off_ref, order_ref,
                xe_hbm,
                wg_lo, wu_lo, wd_lo, wg_hi, wu_hi, wd_hi, g2_ref, b2_ref,
                out_hbm,
                gbuf, obuf, gsem, ssem, pend):
    i = pl.program_id(0)
    nv = nv_ref[i]
    o_pitch = X_PITCH if out_token_major else 1

    def gather_copy(tok, r, s):
        return pltpu.make_async_copy(xe_hbm.at[pl.ds(tok * XE_PITCH, XE_PITCH), :],
                                     gbuf.at[s, pl.ds(r * XE_PITCH, XE_PITCH), :], gsem.at[s])

    def scatter_copy(tok, r, s, n_tok=1):
        src = r * o_pitch if isinstance(r, int) else pl.multiple_of(r * o_pitch, o_pitch)
        return pltpu.make_async_copy(obuf.at[s, pl.ds(src, n_tok * o_pitch), :],
                                     out_hbm.at[pl.ds(tok * o_pitch, n_tok * o_pitch), :], ssem.at[s])

    def start_gather(tile, s):
        base = off_ref[tile]
        for r in range(TM):
            gather_copy(order_ref[base + r], r, s).start()

    def start_scatter(tile, s, n_rows):
        base = off_ref[tile]
        n_chunks = lax.shift_right_logical(n_rows, DMA_UNROLL.bit_length() - 1)

        def chunk(c, carry):
            r0 = c * DMA_UNROLL
            for j in range(DMA_UNROLL):
                scatter_copy(order_ref[base + r0 + j], r0 + j, s).start()
            return carry
        lax.fori_loop(0, n_chunks, chunk, 0)

        def tail(r, carry):
            scatter_copy(order_ref[base + r], r, s).start()
            return carry
        lax.fori_loop(n_chunks * DMA_UNROLL, n_rows, tail, 0)

    def wait_scatter(s):
        n_rows = pend[s]
        width = TM
        while width >= 1:
            @pl.when(jnp.bitwise_and(n_rows, width) != 0)
            def _(width=width):
                scatter_copy(0, 0, s, n_tok=width).wait()
            width //= 2
        pend[s] = 0

    @pl.when(i == 0)
    def _():
        pend[0] = 0
        pend[1] = 0
        start_gather(0, 0)

    nxt = jnp.minimum(i + 1, n_tiles - 1)
    next_valid = jnp.logical_and(i + 1 < n_tiles, nv_ref[nxt] > 0)

    def tile_body(s):
        pltpu.make_async_copy(xe_hbm.at[pl.ds(0, TM * XE_PITCH), :], gbuf.at[s], gsem.at[s]).wait()
        wait_scatter(s)

        @pl.when(next_valid)
        def _():
            start_gather(nxt, 1 - s)

        xf = _load_token_major(gbuf, TM, XE_PITCH, lead=(s,))
        ext = gbuf[s, pl.ds(D_MODEL // LANES, TM, stride=XE_PITCH), :]
        gate_lo = ext[:, 1:2]
        gate_hi = ext[:, 2:3]
        xb = xf.astype(_BF)

        def expert(wg, wu, wd, gate):
            a2 = jnp.dot(xb, wg[...], preferred_element_type=_F32)
            u = jnp.dot(xb, wu[...], preferred_element_type=_F32)
            hgt = ((a2 * jnp.tanh(a2) + a2) * (u * gate)).astype(_BF)
            return jnp.dot(hgt, wd[...], preferred_element_type=_F32)

        moe = expert(wg_lo, wu_lo, wd_lo, gate_lo) + expert(wg_hi, wu_hi, wd_hi, gate_hi)
        y = _layer_norm(alpha * xf + moe, g2_ref[...], b2_ref[...])
        if out_token_major:
            _store_token_major(obuf, y, X_PITCH, lead=(s,))
        else:
            obuf[s] = y
        start_scatter(i, s, nv)
        pend[s] = nv

    for s in range(2):
        @pl.when(jnp.logical_and(nv > 0, i % 2 == s))
        def _(s=s):
            tile_body(s)

    @pl.when(i == n_tiles - 1)
    def _():
        wait_scatter(0)
        wait_scatter(1)


def _moe_call(alpha, layer, xe, plan, w, n_tok, out_token_major):
    elo, ehi, nvalid, off, order = plan
    n_tiles = elo.shape[0]
    wg, wu, wd, g2, b2 = w
    if out_token_major:
        obuf_shape, out_rows, out_cols = (2, TM * X_PITCH, LANES), n_tok * X_PITCH, LANES
    else:
        obuf_shape, out_rows, out_cols = (2, TM, D_MODEL), n_tok, D_MODEL

    def wspec(shape, which):
        def imap(i, elo_r, ehi_r, *_):
            return (layer, (elo_r, ehi_r)[which][i], 0, 0)
        return pl.BlockSpec((None, None) + shape, imap)

    grid_spec = pltpu.PrefetchScalarGridSpec(
        num_scalar_prefetch=5,
        grid=(n_tiles,),
        in_specs=[
            pl.BlockSpec(memory_space=pl.ANY),
            wspec((D_MODEL, D_FF), 0), wspec((D_MODEL, D_FF), 0), wspec((D_FF, D_MODEL), 0),
            wspec((D_MODEL, D_FF), 1), wspec((D_MODEL, D_FF), 1), wspec((D_FF, D_MODEL), 1),
            pl.BlockSpec((None, 1, D_MODEL), lambda i, *_: (layer, 0, 0)),
            pl.BlockSpec((None, 1, D_MODEL), lambda i, *_: (layer, 0, 0)),
        ],
        out_specs=pl.BlockSpec(memory_space=pl.ANY),
        scratch_shapes=[
            pltpu.VMEM((2, TM * XE_PITCH, LANES), _F32),
            pltpu.VMEM(obuf_shape, _F32),
            pltpu.SemaphoreType.DMA((2,)),
            pltpu.SemaphoreType.DMA((2,)),
            pltpu.SMEM((2,), jnp.int32),
        ],
    )
    return pl.pallas_call(
        functools.partial(_moe_kernel, alpha, n_tiles, out_token_major),
        grid_spec=grid_spec,
        out_shape=jax.ShapeDtypeStruct((out_rows, out_cols), _F32),
        compiler_params=pltpu.CompilerParams(
            dimension_semantics=("arbitrary",), vmem_limit_bytes=VMEM_LIMIT),
        name="moe",
    )(elo, ehi, nvalid, off, order, xe, wg, wu, wd, wg, wu, wd, g2, b2)


_PAIR_LO = (0, 0, 0, 1, 1, 2)
_PAIR_HI = (1, 2, 3, 2, 3, 3)


def _plan(cls, n_tiles):
    n_tok = cls.shape[0]
    tok = jnp.arange(n_tok, dtype=jnp.int32)
    order = jnp.sort(cls * n_tok + tok) % n_tok
    order = jnp.concatenate([order, jnp.zeros((TM,), jnp.int32)])
    counts = jnp.sum((cls[:, None] == jnp.arange(N_CLASSES, dtype=jnp.int32)[None, :]).astype(jnp.int32), axis=0)
    cstart = jnp.cumsum(counts) - counts
    tiles = (counts + TM - 1) // TM
    tile_end = jnp.cumsum(tiles)
    tile_start = tile_end - tiles
    ti = jnp.arange(n_tiles, dtype=jnp.int32)
    n_used = tile_end[-1]
    tsel = jnp.minimum(ti, n_used - 1)
    tcls = jnp.sum((tile_end[None, :] <= tsel[:, None]).astype(jnp.int32), axis=1)
    within = (tsel - tile_start[tcls]) * TM
    nvalid = jnp.where(ti < n_used, jnp.clip(counts[tcls] - within, 0, TM), 0).astype(jnp.int32)
    off = (cstart[tcls] + within).astype(jnp.int32)
    grp = tcls // N_PAIRS
    pair = tcls % N_PAIRS
    elo = grp * EXPERTS_PER_GROUP + jnp.asarray(_PAIR_LO, jnp.int32)[pair]
    ehi = grp * EXPERTS_PER_GROUP + jnp.asarray(_PAIR_HI, jnp.int32)[pair]
    return elo.astype(jnp.int32), ehi.astype(jnp.int32), nvalid, off, order


def kernel(x, w_in, conv_a_w, w_branch_a, conv_r_w, conv_r_b, w_rg_a, b_rg_a, w_rg_x, b_rg_x, rg_lambda, w_branch_r, w_out, ln1_g, ln1_b, w_router, b_router, w_exp_gate, w_exp_up, w_exp_down, ln2_g, ln2_b):
    n_seq, seq, _ = x.shape
    depth = w_in.shape[0]
    n_tok = n_seq * seq
    assert seq % TT == 0 and n_tok % TM == 0
    n_tiles = n_tok // TM + N_CLASSES
    alpha = (2.0 * depth) ** 0.25

    wrt = w_router.T.astype(_F32)
    wrt_hi = wrt.astype(_BF)
    wrt_lo = (wrt - wrt_hi.astype(_F32)).astype(_BF)
    brt = b_router.astype(_F32).reshape(N_EXPERTS, 1)

    in_scale = jnp.concatenate([jnp.ones((OFF_GA,), _F32), jnp.full((D_IN - OFF_GA,), 0.5, _F32)])
    mix_w = (
        (w_in * in_scale).astype(_BF),
        conv_a_w,
        w_branch_a.astype(_BF),
        conv_r_w,
        conv_r_b.reshape(depth, 1, D_R),
        (0.5 * jnp.concatenate([w_rg_a, w_rg_x], axis=-1)).astype(_BF),
        0.5 * b_rg_a.reshape(depth, 1, D_R),
        0.5 * b_rg_x.reshape(depth, 1, D_R),
        rg_lambda.reshape(depth, 1, D_R),
        (0.25 * w_branch_r).astype(_BF),
        (0.5 * w_out).astype(_BF),
        ln1_g.reshape(depth, 1, D_MODEL),
        ln1_b.reshape(depth, 1, D_MODEL),
        wrt_hi, wrt_lo, brt,
    )
    moe_w = (
        (0.5 * w_exp_gate).astype(_BF), w_exp_up.astype(_BF), w_exp_down.astype(_BF),
        ln2_g.reshape(depth, 1, D_MODEL), ln2_b.reshape(depth, 1, D_MODEL),
    )

    h = x.reshape(n_tok, D_MODEL)
    for l in range(depth):
        xe, route = _mix_call(alpha, l, h, l > 0, n_seq, n_tok, mix_w)
        plan = _plan(route[0].astype(jnp.int32), n_tiles)
        h = _moe_call(alpha, l, xe, plan, moe_w, n_tok, l < depth - 1)
    return h.reshape(n_seq, seq, D_MODEL)
```

```python
import functools

import jax
import jax.numpy as jnp
from jax import lax
from jax.experimental import pallas as pl
from jax.experimental.pallas import tpu as pltpu

D_MODEL = 1024
D_A = 1024
D_R = 1280
N_RG_BLOCKS = 10
RG_BLOCK = 128
RG_C = 8.0
CONV_A_WIDTH = 3
CONV_R_WIDTH = 4
D_IN = 3 * D_A + 2 * D_R + 2 * D_MODEL
N_EXPERTS = 16
N_GROUPS = 4
EXPERTS_PER_GROUP = 4
N_PAIRS = 6
N_CLASSES = N_GROUPS * N_PAIRS
D_FF = 512
LN_EPS = 1e-5

OFF_HA, OFF_BA, OFF_CA = 0, D_A, 2 * D_A
OFF_XR = 3 * D_A
OFF_YR = OFF_XR + D_R
OFF_GA = OFF_YR + D_R
OFF_GR = OFF_GA + D_MODEL
LATE_COLS = D_IN - OFF_YR
LATE_CHUNK = 256

SUBLANES = 8
LANES = 128
EXT = LANES
X_PITCH = D_MODEL // LANES
XE_PITCH = (D_MODEL + EXT) // LANES
TT = 512
TM = 256
DMA_UNROLL = 8
VMEM_LIMIT = 56 * 1024 * 1024

_BF = jnp.bfloat16
_F32 = jnp.float32

GELU_K1 = 0.7978845608028654
GELU_K2 = GELU_K1 * 0.044715


def _layer_norm(y, g, b):
    mu = jnp.mean(y, axis=-1, keepdims=True)
    yc = y - mu
    var = jnp.mean(yc * yc, axis=-1, keepdims=True)
    return yc * lax.rsqrt(var + LN_EPS) * g + b


def _first_index_of(vals, target):
    idx = jnp.full(target.shape, float(len(vals) - 1), _F32)
    for j in range(len(vals) - 2, -1, -1):
        idx = jnp.where(vals[j] == target, float(j), idx)
    return idx


def _route(logits_t):
    m = jnp.max(logits_t, axis=0, keepdims=True)
    e = jnp.exp(logits_t - m)
    p = e / jnp.sum(e, axis=0, keepdims=True)
    neg = jnp.full((1, logits_t.shape[1]), -jnp.inf, _F32)
    scores, m1s, i1s, m2s, i2s = [], [], [], [], []
    for g in range(N_GROUPS):
        v = [p[g * EXPERTS_PER_GROUP + j:g * EXPERTS_PER_GROUP + j + 1, :] for j in range(EXPERTS_PER_GROUP)]
        m1 = jnp.maximum(jnp.maximum(v[0], v[1]), jnp.maximum(v[2], v[3]))
        i1 = _first_index_of(v, m1)
        w = [jnp.where(i1 == float(j), neg, v[j]) for j in range(EXPERTS_PER_GROUP)]
        m2 = jnp.maximum(jnp.maximum(w[0], w[1]), jnp.maximum(w[2], w[3]))
        i2 = _first_index_of(w, m2)
        scores.append(m1 + m2)
        m1s.append(m1); i1s.append(i1); m2s.append(m2); i2s.append(i2)
    best = jnp.maximum(jnp.maximum(scores[0], scores[1]), jnp.maximum(scores[2], scores[3]))
    gsel = _first_index_of(scores, best)

    def pick(xs):
        out = xs[N_GROUPS - 1]
        for g in range(N_GROUPS - 2, -1, -1):
            out = jnp.where(gsel == float(g), xs[g], out)
        return out

    m1, i1, m2, i2 = pick(m1s), pick(i1s), pick(m2s), pick(i2s)
    den = m1 + m2
    gate1, gate2 = m1 / den, m2 / den
    first_is_lo = i1 < i2
    lo = jnp.where(first_is_lo, i1, i2)
    hi = jnp.where(first_is_lo, i2, i1)
    g_lo = jnp.where(first_is_lo, gate1, gate2)
    g_hi = jnp.where(first_is_lo, gate2, gate1)
    pair = lo * (7.0 - lo) * 0.5 + (hi - lo - 1.0)
    cls = gsel * float(N_PAIRS) + pair
    return cls, g_lo, g_hi


def _load_token_major(ref, n_rows, pitch, lead=()):
    cols = [ref[lead + (pl.ds(c, n_rows, stride=pitch), slice(None))] for c in range(D_MODEL // LANES)]
    return jnp.concatenate(cols, axis=1)


def _store_token_major(ref, val, pitch, lead=()):
    n_rows = val.shape[0]
    for c in range(val.shape[1] // LANES):
        ref[lead + (pl.ds(c, n_rows, stride=pitch), slice(None))] = val[:, c * LANES:(c + 1) * LANES]


def _causal_conv(u, hist_ref, w_ref, n_taps, out_ref, bias=None):
    tt = u.shape[0]

    def tap0(val):
        out = val * w_ref[n_taps - 1:n_taps, :]
        return out if bias is None else out + bias

    acc = tap0(u)
    for d in range(1, n_taps):
        acc = acc + pltpu.roll(u, d, 0) * w_ref[n_taps - 1 - d:n_taps - d, :]
    out_ref[...] = acc
    head = u[0:SUBLANES, :]
    hist = hist_ref[...]
    row = lax.broadcasted_iota(jnp.int32, head.shape, 0)
    acc_head = tap0(head)
    for d in range(1, n_taps):
        shifted = jnp.where(row < d, pltpu.roll(hist, d, 0), pltpu.roll(head, d, 0))
        acc_head = acc_head + shifted * w_ref[n_taps - 1 - d:n_taps - d, :]
    out_ref[pl.ds(0, SUBLANES), :] = acc_head
    hist_ref[...] = u[tt - SUBLANES:tt, :]


def _mix_kernel(alpha, x_token_major,
                x_ref, win_ref, caw_ref, wa_ref, crw_ref, crb_ref, wrg_ref, brga_ref, brgx_ref,
                lam_ref, wr_ref, wo_ref, g1_ref, b1_ref, wrt_hi_ref, wrt_lo_ref, brt_ref,
                xe_ref, route_ref,
                uhist, xrhist, cabuf, xcbuf, abuf, bbuf, hbuf, hcarry, latebuf):
    t = pl.program_id(1)
    tt = route_ref.shape[1]

    @pl.when(t == 0)
    def _():
        uhist[...] = jnp.zeros_like(uhist)
        xrhist[...] = jnp.zeros_like(xrhist)
        hcarry[...] = jnp.zeros_like(hcarry)

    x = _load_token_major(x_ref, tt, X_PITCH) if x_token_major else x_ref[...]
    xb = x.astype(_BF)

    def proj(off, width):
        return jnp.dot(xb, win_ref[:, off:off + width], preferred_element_type=_F32)

    _causal_conv(proj(OFF_CA, D_A) * proj(OFF_HA, D_A), uhist, caw_ref, CONV_A_WIDTH, cabuf)
    va = (proj(OFF_BA, D_A) * cabuf[...]).astype(_BF)
    ya = jnp.dot(va, wa_ref[...], preferred_element_type=_F32)

    _causal_conv(proj(OFF_XR, D_R), xrhist, crw_ref, CONV_R_WIDTH, xcbuf, bias=crb_ref[...])
    z = -lam_ref[...]
    softplus_neg_lam = jnp.maximum(z, 0.0) + jnp.log1p(jnp.exp(-jnp.abs(z)))
    c_half = (-0.5 * RG_C) * softplus_neg_lam
    row = lax.broadcasted_iota(jnp.int32, (tt // SUBLANES, SUBLANES, RG_BLOCK), 1)
    late_chunks = [[] for _ in range(N_RG_BLOCKS)]
    for j in range(LATE_COLS // LATE_CHUNK):
        late_chunks[(j * N_RG_BLOCKS) // (LATE_COLS // LATE_CHUNK)].append(j)
    for h in range(N_RG_BLOCKS):
        for j in late_chunks[h]:
            latebuf[:, j * LATE_CHUNK:(j + 1) * LATE_CHUNK] = proj(OFF_YR + j * LATE_CHUNK, LATE_CHUNK)
        sl = slice(h * RG_BLOCK, (h + 1) * RG_BLOCK)
        xc = xcbuf[:, sl]
        gh = jnp.dot(xc.astype(_BF), wrg_ref[h], preferred_element_type=_F32)
        t_r = jnp.tanh(gh[:, :RG_BLOCK] + brga_ref[:, sl])
        t_i = jnp.tanh(gh[:, RG_BLOCK:] + brgx_ref[:, sl])
        log_a = c_half[:, sl] * t_r + c_half[:, sl]
        a_t = jnp.exp(log_a)
        b_t = jnp.sqrt(1.0 - a_t * a_t) * (t_i * xc + xc)
        a3 = a_t.reshape(tt // SUBLANES, SUBLANES, RG_BLOCK)
        b3 = b_t.reshape(tt // SUBLANES, SUBLANES, RG_BLOCK)
        for k in (1, 2, 4):
            keep = row >= k
            b3 = jnp.where(keep, a3 * pltpu.roll(b3, k, 1) + b3, b3)
            a3 = jnp.where(keep, a3 * pltpu.roll(a3, k, 1), a3)
        abuf[:, sl] = a3.reshape(tt, RG_BLOCK)
        bbuf[:, sl] = b3.reshape(tt, RG_BLOCK)

    def carry_groups(g, h_prev):
        o = pl.multiple_of(g * SUBLANES, SUBLANES)
        hg = abuf[pl.ds(o, SUBLANES), :] * h_prev + bbuf[pl.ds(o, SUBLANES), :]
        hbuf[pl.ds(o, SUBLANES), :] = hg
        return hg[SUBLANES - 1:SUBLANES, :]

    hcarry[...] = lax.fori_loop(0, tt // SUBLANES, carry_groups, hcarry[...])
    v = latebuf[:, :D_R]
    th = jnp.tanh(v * (GELU_K2 * (v * v) + GELU_K1))
    vr = (hbuf[...] * (v * th + v)).astype(_BF)
    yr = jnp.dot(vr, wr_ref[...], preferred_element_type=_F32)

    t_a = jnp.tanh(latebuf[:, OFF_GA - OFF_YR:OFF_GR - OFF_YR])
    t_g = jnp.tanh(latebuf[:, OFF_GR - OFF_YR:])
    merged2 = (t_a * ya + ya) + (t_g * yr + yr)
    o = jnp.dot(merged2.astype(_BF), wo_ref[...], preferred_element_type=_F32)
    x1 = _layer_norm(alpha * x + o, g1_ref[...], b1_ref[...])

    x1_hi = x1.astype(_BF)
    x1_lo = (x1 - x1_hi.astype(_F32)).astype(_BF)
    nt_dims = (((1,), (1,)), ((), ()))
    logits_t = (lax.dot_general(wrt_hi_ref[...], x1_hi, nt_dims, preferred_element_type=_F32)
                + lax.dot_general(wrt_hi_ref[...], x1_lo, nt_dims, preferred_element_type=_F32)
                + lax.dot_general(wrt_lo_ref[...], x1_hi, nt_dims, preferred_element_type=_F32)
                + brt_ref[...])
    cls, g_lo, g_hi = _route(logits_t)
    rsel = lax.broadcasted_iota(jnp.int32, (SUBLANES, tt), 0)
    route_ref[...] = jnp.where(rsel == 0, cls, jnp.where(rsel == 1, g_lo, jnp.where(rsel == 2, g_hi, 0.0)))
    esel = lax.broadcasted_iota(jnp.int32, (EXT, tt), 0)
    ext_t = jnp.where(esel == 0, cls, jnp.where(esel == 1, g_lo, jnp.where(esel == 2, g_hi, 0.0)))
    _store_token_major(xe_ref, jnp.concatenate([x1, ext_t.T], axis=1), XE_PITCH)


def _mix_call(alpha, layer, x2d, x_token_major, n_seq, n_tok, w):
    seq = n_tok // n_seq
    nt = seq // TT

    def layer_const(shape):
        zeros = (0,) * len(shape)
        return pl.BlockSpec((None,) + shape, lambda b, t: (layer,) + zeros, pipeline_mode=pl.Buffered(1))

    def const(shape):
        zeros = (0,) * len(shape)
        return pl.BlockSpec(shape, lambda b, t: zeros, pipeline_mode=pl.Buffered(1))

    in_specs = [
        (pl.BlockSpec((TT * X_PITCH, LANES), lambda b, t: (b * nt + t, 0)) if x_token_major
         else pl.BlockSpec((TT, D_MODEL), lambda b, t: (b * nt + t, 0))),
        layer_const((D_MODEL, D_IN)),
        layer_const((CONV_A_WIDTH, D_A)),
        layer_const((D_A, D_MODEL)),
        layer_const((CONV_R_WIDTH, D_R)),
        layer_const((1, D_R)),
        layer_const((N_RG_BLOCKS, RG_BLOCK, 2 * RG_BLOCK)),
        layer_const((1, D_R)),
        layer_const((1, D_R)),
        layer_const((1, D_R)),
        layer_const((D_R, D_MODEL)),
        layer_const((D_MODEL, D_MODEL)),
        layer_const((1, D_MODEL)),
        layer_const((1, D_MODEL)),
        const((N_EXPERTS, D_MODEL)),
        const((N_EXPERTS, D_MODEL)),
        const((N_EXPERTS, 1)),
    ]
    out_specs = [
        pl.BlockSpec((TT * XE_PITCH, LANES), lambda b, t: (b * nt + t, 0)),
        pl.BlockSpec((SUBLANES, TT), lambda b, t: (0, b * nt + t)),
    ]
    out_shape = [
        jax.ShapeDtypeStruct((n_tok * XE_PITCH, LANES), _F32),
        jax.ShapeDtypeStruct((SUBLANES, n_tok), _F32),
    ]
    scratch = [
        pltpu.VMEM((SUBLANES, D_A), _F32),
        pltpu.VMEM((SUBLANES, D_R), _F32),
        pltpu.VMEM((TT, D_A), _F32),
        pltpu.VMEM((TT, D_R), _F32),
        pltpu.VMEM((TT, D_R), _F32),
        pltpu.VMEM((TT, D_R), _F32),
        pltpu.VMEM((TT, D_R), _F32),
        pltpu.VMEM((1, D_R), _F32),
        pltpu.VMEM((TT, LATE_COLS), _F32),
    ]
    return pl.pallas_call(
        functools.partial(_mix_kernel, alpha, x_token_major),
        grid=(n_seq, nt),
        in_specs=in_specs,
        out_specs=out_specs,
        out_shape=out_shape,
        scratch_shapes=scratch,
        compiler_params=pltpu.CompilerParams(
            dimension_semantics=("arbitrary", "arbitrary"), vmem_limit_bytes=VMEM_LIMIT),
        name="mix",
    )(x2d, *w)


def _moe_kernel(alpha, n_tiles, n_tok, out_token_major,
                elo_ref, ehi_ref, nv_ref, off_ref, order_ref,
                xe_hbm,
                wg_lo, wu_lo, wd_lo, wg_hi, wu_hi, wd_hi, g2_ref, b2_ref,
                out_hbm,
                gbuf0, gbuf1, obuf0, obuf1, gsem, ssem):
    gbuf = (gbuf0, gbuf1)
    obuf = (obuf0, obuf1)
    i = pl.program_id(0)
    nv = nv_ref[i]
    prev = jnp.maximum(i - 1, 0)
    nv_prev = jnp.where(i > 0, nv_ref[prev], 0)
    o_pitch = X_PITCH if out_token_major else 1

    def gather_copy(tok, r, s):
        return pltpu.make_async_copy(xe_hbm.at[pl.ds(tok * XE_PITCH, XE_PITCH), :],
                                     gbuf[s].at[pl.ds(r * XE_PITCH, XE_PITCH), :], gsem.at[s])

    def scatter_copy(tok, r, s, n_rows=1):
        return pltpu.make_async_copy(obuf[s].at[pl.ds(r * o_pitch, n_rows * o_pitch), :],
                                     out_hbm.at[pl.ds(tok * o_pitch, n_rows * o_pitch), :], ssem.at[s])

    def start_gather(tile, s):
        base = off_ref[tile]
        for r in range(TM):
            gather_copy(order_ref[base + r], r, s).start()

    def wait_gather(s):
        pltpu.make_async_copy(xe_hbm.at[pl.ds(0, TM * XE_PITCH), :], gbuf[s], gsem.at[s]).wait()

    def start_scatter(tile, s):
        base = off_ref[tile]
        n_valid = nv_ref[tile]
        for r in range(TM):
            tok = jnp.where(r < n_valid, order_ref[base + r], n_tok + r)
            scatter_copy(tok, r, s).start()

    def wait_scatter(s):
        scatter_copy(0, 0, s, n_rows=TM).wait()

    def tile_body(s, first):
        wait_gather(s)
        xf = _load_token_major(gbuf[s], TM, XE_PITCH)
        ext = gbuf[s][pl.ds(D_MODEL // LANES, TM, stride=XE_PITCH), :]
        gate_lo = ext[:, 1:2]
        gate_hi = ext[:, 2:3]
        xb = xf.astype(_BF)
        start_gather(i + 1, 1 - s)
        if not first:
            start_scatter(i - 1, 1 - s)

        def expert(wg, wu, wd, gate):
            a2 = jnp.dot(xb, wg[...], preferred_element_type=_F32)
            u = jnp.dot(xb, wu[...], preferred_element_type=_F32)
            hgt = ((a2 * jnp.tanh(a2) + a2) * (u * gate)).astype(_BF)
            return jnp.dot(hgt, wd[...], preferred_element_type=_F32)

        moe = expert(wg_lo, wu_lo, wd_lo, gate_lo) + expert(wg_hi, wu_hi, wd_hi, gate_hi)
        y = _layer_norm(alpha * xf + moe, g2_ref[...], b2_ref[...])

        @pl.when(i >= 2)
        def _():
            wait_scatter(s)

        if out_token_major:
            _store_token_major(obuf[s], y, X_PITCH)
        else:
            obuf[s][...] = y

    @pl.when(i == 0)
    def _():
        start_gather(0, 0)
        obuf[1][...] = jnp.zeros_like(obuf[1])
        spare = scatter_copy(n_tok, 0, 1, n_rows=TM)
        spare.start()
        spare.wait()
        tile_body(0, True)

    for s in range(2):
        @pl.when(jnp.logical_and(jnp.logical_and(i > 0, nv > 0), i % 2 == s))
        def _(s=s):
            tile_body(s, False)

        @pl.when(jnp.logical_and(jnp.logical_and(nv == 0, nv_prev > 0), i % 2 == s))
        def _(s=s):
            wait_gather(s)
            start_scatter(i - 1, 1 - s)

            @pl.when(i >= 2)
            def _():
                wait_scatter(s)
            wait_scatter(1 - s)


def _moe_call(alpha, layer, xe, plan, w, n_tok, out_token_major):
    elo, ehi, nvalid, off, order = plan
    n_tiles = elo.shape[0]
    wg, wu, wd, g2, b2 = w
    n_slots = n_tok + TM
    if out_token_major:
        obuf_shape, out_rows, out_cols = (TM * X_PITCH, LANES), n_slots * X_PITCH, LANES
    else:
        obuf_shape, out_rows, out_cols = (TM, D_MODEL), n_slots, D_MODEL

    def wspec(shape, which):
        def imap(i, elo_r, ehi_r, *_):
            return (layer, (elo_r, ehi_r)[which][i], 0, 0)
        return pl.BlockSpec((None, None) + shape, imap)

    grid_spec = pltpu.PrefetchScalarGridSpec(
        num_scalar_prefetch=5,
        grid=(n_tiles,),
        in_specs=[
            pl.BlockSpec(memory_space=pl.ANY),
            wspec((D_MODEL, D_FF), 0), wspec((D_MODEL, D_FF), 0), wspec((D_FF, D_MODEL), 0),
            wspec((D_MODEL, D_FF), 1), wspec((D_MODEL, D_FF), 1), wspec((D_FF, D_MODEL), 1),
            pl.BlockSpec((None, 1, D_MODEL), lambda i, *_: (layer, 0, 0)),
            pl.BlockSpec((None, 1, D_MODEL), lambda i, *_: (layer, 0, 0)),
        ],
        out_specs=pl.BlockSpec(memory_space=pl.ANY),
        scratch_shapes=[
            pltpu.VMEM((TM * XE_PITCH, LANES), _F32),
            pltpu.VMEM((TM * XE_PITCH, LANES), _F32),
            pltpu.VMEM(obuf_shape, _F32),
            pltpu.VMEM(obuf_shape, _F32),
            pltpu.SemaphoreType.DMA((2,)),
            pltpu.SemaphoreType.DMA((2,)),
        ],
    )
    return pl.pallas_call(
        functools.partial(_moe_kernel, alpha, n_tiles, n_tok, out_token_major),
        grid_spec=grid_spec,
        out_shape=jax.ShapeDtypeStruct((out_rows, out_cols), _F32),
        compiler_params=pltpu.CompilerParams(
            dimension_semantics=("arbitrary",), vmem_limit_bytes=VMEM_LIMIT),
        name="moe",
    )(elo, ehi, nvalid, off, order, xe, wg, wu, wd, wg, wu, wd, g2, b2)


_PAIR_LO = (0, 0, 0, 1, 1, 2)
_PAIR_HI = (1, 2, 3, 2, 3, 3)


def _plan(cls, n_tiles):
    n_tok = cls.shape[0]
    tok = jnp.arange(n_tok, dtype=jnp.int32)
    order = jnp.sort(cls * n_tok + tok) % n_tok
    order = jnp.concatenate([order, jnp.zeros((TM,), jnp.int32)])
    counts = jnp.sum((cls[:, None] == jnp.arange(N_CLASSES, dtype=jnp.int32)[None, :]).astype(jnp.int32), axis=0)
    cstart = jnp.cumsum(counts) - counts
    tiles = (counts + TM - 1) // TM
    tile_end = jnp.cumsum(tiles)
    tile_start = tile_end - tiles
    ti = jnp.arange(n_tiles, dtype=jnp.int32)
    n_used = tile_end[-1]
    tsel = jnp.minimum(ti, n_used - 1)
    tcls = jnp.sum((tile_end[None, :] <= tsel[:, None]).astype(jnp.int32), axis=1)
    within = (tsel - tile_start[tcls]) * TM
    nvalid = jnp.where(ti < n_used, jnp.clip(counts[tcls] - within, 0, TM), 0).astype(jnp.int32)
    off = (cstart[tcls] + within).astype(jnp.int32)
    grp = tcls // N_PAIRS
    pair = tcls % N_PAIRS
    elo = grp * EXPERTS_PER_GROUP + jnp.asarray(_PAIR_LO, jnp.int32)[pair]
    ehi = grp * EXPERTS_PER_GROUP + jnp.asarray(_PAIR_HI, jnp.int32)[pair]
    return elo.astype(jnp.int32), ehi.astype(jnp.int32), nvalid, off, order


def kernel(x, w_in, conv_a_w, w_branch_a, conv_r_w, conv_r_b, w_rg_a, b_rg_a, w_rg_x, b_rg_x, rg_lambda, w_branch_r, w_out, ln1_g, ln1_b, w_router, b_router, w_exp_gate, w_exp_up, w_exp_down, ln2_g, ln2_b):
    n_seq, seq, _ = x.shape
    depth = w_in.shape[0]
    n_tok = n_seq * seq
    assert seq % TT == 0 and n_tok % TM == 0
    n_tiles = n_tok // TM + N_CLASSES
    alpha = (2.0 * depth) ** 0.25

    wrt = w_router.T.astype(_F32)
    wrt_hi = wrt.astype(_BF)
    wrt_lo = (wrt - wrt_hi.astype(_F32)).astype(_BF)
    brt = b_router.astype(_F32).reshape(N_EXPERTS, 1)

    in_scale = jnp.concatenate([jnp.ones((OFF_GA,), _F32), jnp.full((D_IN - OFF_GA,), 0.5, _F32)])
    mix_w = (
        (w_in * in_scale).astype(_BF),
        conv_a_w,
        w_branch_a.astype(_BF),
        conv_r_w,
        conv_r_b.reshape(depth, 1, D_R),
        (0.5 * jnp.concatenate([w_rg_a, w_rg_x], axis=-1)).astype(_BF),
        0.5 * b_rg_a.reshape(depth, 1, D_R),
        0.5 * b_rg_x.reshape(depth, 1, D_R),
        rg_lambda.reshape(depth, 1, D_R),
        (0.25 * w_branch_r).astype(_BF),
        (0.5 * w_out).astype(_BF),
        ln1_g.reshape(depth, 1, D_MODEL),
        ln1_b.reshape(depth, 1, D_MODEL),
        wrt_hi, wrt_lo, brt,
    )
    moe_w = (
        (0.5 * w_exp_gate).astype(_BF), w_exp_up.astype(_BF), w_exp_down.astype(_BF),
        ln2_g.reshape(depth, 1, D_MODEL), ln2_b.reshape(depth, 1, D_MODEL),
    )

    h = x.reshape(n_tok, D_MODEL)
    for l in range(depth):
        xe, route = _mix_call(alpha, l, h, l > 0, n_seq, n_tok, mix_w)
        plan = _plan(route[0].astype(jnp.int32), n_tiles)
        h = _moe_call(alpha, l, xe, plan, moe_w, n_tok, l < depth - 1)
    return h[:n_tok].reshape(n_seq, seq, D_MODEL)
```

```python
import functools

import jax
import jax.numpy as jnp
from jax import lax
from jax.experimental import pallas as pl
from jax.experimental.pallas import tpu as pltpu

D_MODEL = 1024
D_A = 1024
D_R = 1280
N_RG_BLOCKS = 10
RG_BLOCK = 128
RG_C = 8.0
CONV_A_WIDTH = 3
CONV_R_WIDTH = 4
D_IN = 3 * D_A + 2 * D_R + 2 * D_MODEL
N_EXPERTS = 16
N_GROUPS = 4
EXPERTS_PER_GROUP = 4
N_PAIRS = 6
N_CLASSES = N_GROUPS * N_PAIRS
D_FF = 512
LN_EPS = 1e-5

OFF_HA, OFF_BA, OFF_CA = 0, D_A, 2 * D_A
OFF_XR = 3 * D_A
OFF_YR = OFF_XR + D_R
OFF_GA = OFF_YR + D_R
OFF_GR = OFF_GA + D_MODEL
LATE_COLS = D_IN - OFF_YR
LATE_CHUNK = 256

SUBLANES = 8
LANES = 128
EXT = LANES
X_PITCH = D_MODEL // LANES
XE_PITCH = (D_MODEL + EXT) // LANES
TT = 512
TM = 256
DMA_UNROLL = 8
VMEM_LIMIT = 56 * 1024 * 1024

_BF = jnp.bfloat16
_F32 = jnp.float32

GELU_K1 = 0.7978845608028654
GELU_K2 = GELU_K1 * 0.044715


def _layer_norm(y, g, b):
    mu = jnp.mean(y, axis=-1, keepdims=True)
    yc = y - mu
    var = jnp.mean(yc * yc, axis=-1, keepdims=True)
    return yc * lax.rsqrt(var + LN_EPS) * g + b


def _first_index_of(vals, target):
    idx = jnp.full(target.shape, float(len(vals) - 1), _F32)
    for j in range(len(vals) - 2, -1, -1):
        idx = jnp.where(vals[j] == target, float(j), idx)
    return idx


def _route(logits_t):
    m = jnp.max(logits_t, axis=0, keepdims=True)
    e = jnp.exp(logits_t - m)
    p = e / jnp.sum(e, axis=0, keepdims=True)
    neg = jnp.full((1, logits_t.shape[1]), -jnp.inf, _F32)
    scores, m1s, i1s, m2s, i2s = [], [], [], [], []
    for g in range(N_GROUPS):
        v = [p[g * EXPERTS_PER_GROUP + j:g * EXPERTS_PER_GROUP + j + 1, :] for j in range(EXPERTS_PER_GROUP)]
        m1 = jnp.maximum(jnp.maximum(v[0], v[1]), jnp.maximum(v[2], v[3]))
        i1 = _first_index_of(v, m1)
        w = [jnp.where(i1 == float(j), neg, v[j]) for j in range(EXPERTS_PER_GROUP)]
        m2 = jnp.maximum(jnp.maximum(w[0], w[1]), jnp.maximum(w[2], w[3]))
        i2 = _first_index_of(w, m2)
        scores.append(m1 + m2)
        m1s.append(m1); i1s.append(i1); m2s.append(m2); i2s.append(i2)
    best = jnp.maximum(jnp.maximum(scores[0], scores[1]), jnp.maximum(scores[2], scores[3]))
    gsel = _first_index_of(scores, best)

    def pick(xs):
        out = xs[N_GROUPS - 1]
        for g in range(N_GROUPS - 2, -1, -1):
            out = jnp.where(gsel == float(g), xs[g], out)
        return out

    m1, i1, m2, i2 = pick(m1s), pick(i1s), pick(m2s), pick(i2s)
    den = m1 + m2
    gate1, gate2 = m1 / den, m2 / den
    first_is_lo = i1 < i2
    lo = jnp.where(first_is_lo, i1, i2)
    hi = jnp.where(first_is_lo, i2, i1)
    g_lo = jnp.where(first_is_lo, gate1, gate2)
    g_hi = jnp.where(first_is_lo, gate2, gate1)
    pair = lo * (7.0 - lo) * 0.5 + (hi - lo - 1.0)
    cls = gsel * float(N_PAIRS) + pair
    return cls, g_lo, g_hi


def _load_token_major(ref, n_rows, pitch, lead=()):
    cols = [ref[lead + (pl.ds(c, n_rows, stride=pitch), slice(None))] for c in range(D_MODEL // LANES)]
    return jnp.concatenate(cols, axis=1)


def _store_token_major(ref, val, pitch, lead=()):
    n_rows = val.shape[0]
    for c in range(val.shape[1] // LANES):
        ref[lead + (pl.ds(c, n_rows, stride=pitch), slice(None))] = val[:, c * LANES:(c + 1) * LANES]


def _causal_conv(u, hist_ref, w_ref, n_taps, out_ref, bias=None):
    tt = u.shape[0]

    def tap0(val):
        out = val * w_ref[n_taps - 1:n_taps, :]
        return out if bias is None else out + bias

    acc = tap0(u)
    for d in range(1, n_taps):
        acc = acc + pltpu.roll(u, d, 0) * w_ref[n_taps - 1 - d:n_taps - d, :]
    out_ref[...] = acc
    head = u[0:SUBLANES, :]
    hist = hist_ref[...]
    row = lax.broadcasted_iota(jnp.int32, head.shape, 0)
    acc_head = tap0(head)
    for d in range(1, n_taps):
        shifted = jnp.where(row < d, pltpu.roll(hist, d, 0), pltpu.roll(head, d, 0))
        acc_head = acc_head + shifted * w_ref[n_taps - 1 - d:n_taps - d, :]
    out_ref[pl.ds(0, SUBLANES), :] = acc_head
    hist_ref[...] = u[tt - SUBLANES:tt, :]


def _mix_kernel(alpha, x_token_major,
                x_ref, win_ref, caw_ref, wa_ref, crw_ref, crb_ref, wrg_ref, brga_ref, brgx_ref,
                lam_ref, wr_ref, wo_ref, g1_ref, b1_ref, wrt_hi_ref, wrt_lo_ref, brt_ref,
                xe_ref, route_ref,
                uhist, xrhist, cabuf, xcbuf, abuf, bbuf, hbuf, hcarry, latebuf):
    t = pl.program_id(1)
    tt = route_ref.shape[1]

    @pl.when(t == 0)
    def _():
        uhist[...] = jnp.zeros_like(uhist)
        xrhist[...] = jnp.zeros_like(xrhist)
        hcarry[...] = jnp.zeros_like(hcarry)

    x = _load_token_major(x_ref, tt, X_PITCH) if x_token_major else x_ref[...]
    xb = x.astype(_BF)

    def proj(off, width):
        return jnp.dot(xb, win_ref[:, off:off + width], preferred_element_type=_F32)

    _causal_conv(proj(OFF_CA, D_A) * proj(OFF_HA, D_A), uhist, caw_ref, CONV_A_WIDTH, cabuf)
    va = (proj(OFF_BA, D_A) * cabuf[...]).astype(_BF)
    ya = jnp.dot(va, wa_ref[...], preferred_element_type=_F32)

    _causal_conv(proj(OFF_XR, D_R), xrhist, crw_ref, CONV_R_WIDTH, xcbuf, bias=crb_ref[...])
    z = -lam_ref[...]
    softplus_neg_lam = jnp.maximum(z, 0.0) + jnp.log1p(jnp.exp(-jnp.abs(z)))
    c_half = (-0.5 * RG_C) * softplus_neg_lam
    row = lax.broadcasted_iota(jnp.int32, (tt // SUBLANES, SUBLANES, RG_BLOCK), 1)
    late_chunks = [[] for _ in range(N_RG_BLOCKS)]
    for j in range(LATE_COLS // LATE_CHUNK):
        late_chunks[(j * N_RG_BLOCKS) // (LATE_COLS // LATE_CHUNK)].append(j)
    for h in range(N_RG_BLOCKS):
        for j in late_chunks[h]:
            latebuf[:, j * LATE_CHUNK:(j + 1) * LATE_CHUNK] = proj(OFF_YR + j * LATE_CHUNK, LATE_CHUNK)
        sl = slice(h * RG_BLOCK, (h + 1) * RG_BLOCK)
        xc = xcbuf[:, sl]
        gh = jnp.dot(xc.astype(_BF), wrg_ref[h], preferred_element_type=_F32)
        t_r = jnp.tanh(gh[:, :RG_BLOCK] + brga_ref[:, sl])
        t_i = jnp.tanh(gh[:, RG_BLOCK:] + brgx_ref[:, sl])
        log_a = c_half[:, sl] * t_r + c_half[:, sl]
        a_t = jnp.exp(log_a)
        b_t = jnp.sqrt(1.0 - a_t * a_t) * (t_i * xc + xc)
        a3 = a_t.reshape(tt // SUBLANES, SUBLANES, RG_BLOCK)
        b3 = b_t.reshape(tt // SUBLANES, SUBLANES, RG_BLOCK)
        for k in (1, 2, 4):
            keep = row >= k
            b3 = jnp.where(keep, a3 * pltpu.roll(b3, k, 1) + b3, b3)
            a3 = jnp.where(keep, a3 * pltpu.roll(a3, k, 1), a3)
        abuf[:, sl] = a3.reshape(tt, RG_BLOCK)
        bbuf[:, sl] = b3.reshape(tt, RG_BLOCK)

    def carry_groups(g, h_prev):
        o = pl.multiple_of(g * SUBLANES, SUBLANES)
        hg = abuf[pl.ds(o, SUBLANES), :] * h_prev + bbuf[pl.ds(o, SUBLANES), :]
        hbuf[pl.ds(o, SUBLANES), :] = hg
        return hg[SUBLANES - 1:SUBLANES, :]

    hcarry[...] = lax.fori_loop(0, tt // SUBLANES, carry_groups, hcarry[...])
    v = latebuf[:, :D_R]
    th = jnp.tanh(v * (GELU_K2 * (v * v) + GELU_K1))
    vr = (hbuf[...] * (v * th + v)).astype(_BF)
    yr = jnp.dot(vr, wr_ref[...], preferred_element_type=_F32)

    t_a = jnp.tanh(latebuf[:, OFF_GA - OFF_YR:OFF_GR - OFF_YR])
    t_g = jnp.tanh(latebuf[:, OFF_GR - OFF_YR:])
    merged2 = (t_a * ya + ya) + (t_g * yr + yr)
    o = jnp.dot(merged2.astype(_BF), wo_ref[...], preferred_element_type=_F32)
    x1 = _layer_norm(alpha * x + o, g1_ref[...], b1_ref[...])

    x1_hi = x1.astype(_BF)
    x1_lo = (x1 - x1_hi.astype(_F32)).astype(_BF)
    nt_dims = (((1,), (1,)), ((), ()))
    logits_t = (lax.dot_general(wrt_hi_ref[...], x1_hi, nt_dims, preferred_element_type=_F32)
                + lax.dot_general(wrt_hi_ref[...], x1_lo, nt_dims, preferred_element_type=_F32)
                + lax.dot_general(wrt_lo_ref[...], x1_hi, nt_dims, preferred_element_type=_F32)
                + brt_ref[...])
    cls, g_lo, g_hi = _route(logits_t)
    rsel = lax.broadcasted_iota(jnp.int32, (SUBLANES, tt), 0)
    route_ref[...] = jnp.where(rsel == 0, cls, jnp.where(rsel == 1, g_lo, jnp.where(rsel == 2, g_hi, 0.0)))
    esel = lax.broadcasted_iota(jnp.int32, (EXT, tt), 0)
    ext_t = jnp.where(esel == 0, cls, jnp.where(esel == 1, g_lo, jnp.where(esel == 2, g_hi, 0.0)))
    _store_token_major(xe_ref, jnp.concatenate([x1, ext_t.T], axis=1), XE_PITCH)


def _mix_call(alpha, layer, x2d, x_token_major, n_seq, n_tok, w):
    seq = n_tok // n_seq
    nt = seq // TT

    def layer_const(shape):
        zeros = (0,) * len(shape)
        return pl.BlockSpec((None,) + shape, lambda b, t: (layer,) + zeros, pipeline_mode=pl.Buffered(1))

    def const(shape):
        zeros = (0,) * len(shape)
        return pl.BlockSpec(shape, lambda b, t: zeros, pipeline_mode=pl.Buffered(1))

    in_specs = [
        (pl.BlockSpec((TT * X_PITCH, LANES), lambda b, t: (b * nt + t, 0)) if x_token_major
         else pl.BlockSpec((TT, D_MODEL), lambda b, t: (b * nt + t, 0))),
        layer_const((D_MODEL, D_IN)),
        layer_const((CONV_A_WIDTH, D_A)),
        layer_const((D_A, D_MODEL)),
        layer_const((CONV_R_WIDTH, D_R)),
        layer_const((1, D_R)),
        layer_const((N_RG_BLOCKS, RG_BLOCK, 2 * RG_BLOCK)),
        layer_const((1, D_R)),
        layer_const((1, D_R)),
        layer_const((1, D_R)),
        layer_const((D_R, D_MODEL)),
        layer_const((D_MODEL, D_MODEL)),
        layer_const((1, D_MODEL)),
        layer_const((1, D_MODEL)),
        const((N_EXPERTS, D_MODEL)),
        const((N_EXPERTS, D_MODEL)),
        const((N_EXPERTS, 1)),
    ]
    out_specs = [
        pl.BlockSpec((TT * XE_PITCH, LANES), lambda b, t: (b * nt + t, 0)),
        pl.BlockSpec((SUBLANES, TT), lambda b, t: (0, b * nt + t)),
    ]
    out_shape = [
        jax.ShapeDtypeStruct((n_tok * XE_PITCH, LANES), _F32),
        jax.ShapeDtypeStruct((SUBLANES, n_tok), _F32),
    ]
    scratch = [
        pltpu.VMEM((SUBLANES, D_A), _F32),
        pltpu.VMEM((SUBLANES, D_R), _F32),
        pltpu.VMEM((TT, D_A), _F32),
        pltpu.VMEM((TT, D_R), _F32),
        pltpu.VMEM((TT, D_R), _F32),
        pltpu.VMEM((TT, D_R), _F32),
        pltpu.VMEM((TT, D_R), _F32),
        pltpu.VMEM((1, D_R), _F32),
        pltpu.VMEM((TT, LATE_COLS), _F32),
    ]
    return pl.pallas_call(
        functools.partial(_mix_kernel, alpha, x_token_major),
        grid=(n_seq, nt),
        in_specs=in_specs,
        out_specs=out_specs,
        out_shape=out_shape,
        scratch_shapes=scratch,
        compiler_params=pltpu.CompilerParams(
            dimension_semantics=("arbitrary", "arbitrary"), vmem_limit_bytes=VMEM_LIMIT),
        name="mix",
    )(x2d, *w)


def _moe_kernel(alpha, n_tiles, out_token_major,
                elo_ref, ehi_ref, nv_ref, off_ref, order_ref,
                xe_hbm,
                wg_lo, wu_lo, wd_lo, wg_hi, wu_hi, wd_hi, g2_ref, b2_ref,
                out_hbm,
                gbuf0, gbuf1, obuf0, obuf1, gsem, ssem, pend):
    gbuf = (gbuf0, gbuf1)
    obuf = (obuf0, obuf1)
    i = pl.program_id(0)
    nv = nv_ref[i]
    o_pitch = X_PITCH if out_token_major else 1

    def gather_copy(tok, r, s):
        return pltpu.make_async_copy(xe_hbm.at[pl.ds(tok * XE_PITCH, XE_PITCH), :],
                                     gbuf[s].at[pl.ds(r * XE_PITCH, XE_PITCH), :], gsem.at[s])

    def scatter_copy(tok, r, s, n_rows=1):
        src = r * o_pitch if isinstance(r, int) else pl.multiple_of(r * o_pitch, o_pitch)
        return pltpu.make_async_copy(obuf[s].at[pl.ds(src, n_rows * o_pitch), :],
                                     out_hbm.at[pl.ds(tok * o_pitch, n_rows * o_pitch), :], ssem.at[s])

    def start_gather(tile, s):
        base = off_ref[tile]
        for r in range(TM):
            gather_copy(order_ref[base + r], r, s).start(priority=r % 2)

    def wait_gather(s):
        pltpu.make_async_copy(xe_hbm.at[pl.ds(0, TM * XE_PITCH), :], gbuf[s], gsem.at[s]).wait()

    def start_scatter(tile, s, n_rows):
        base = off_ref[tile]
        n_chunks = lax.shift_right_logical(n_rows, DMA_UNROLL.bit_length() - 1)

        def chunk(c, carry):
            r0 = c * DMA_UNROLL
            for j in range(DMA_UNROLL):
                scatter_copy(order_ref[base + r0 + j], r0 + j, s).start(priority=j % 2)
            return carry
        lax.fori_loop(0, n_chunks, chunk, 0)

        def tail(r, carry):
            scatter_copy(order_ref[base + r], r, s).start()
            return carry
        lax.fori_loop(n_chunks * DMA_UNROLL, n_rows, tail, 0)

    def wait_scatter(s):
        n_rows = pend[s]
        width = TM
        while width >= 1:
            @pl.when(jnp.bitwise_and(n_rows, width) != 0)
            def _(width=width):
                scatter_copy(0, 0, s, n_rows=width).wait()
            width //= 2
        pend[s] = 0

    @pl.when(i == 0)
    def _():
        pend[0] = 0
        pend[1] = 0
        start_gather(0, 0)

    nxt = jnp.minimum(i + 1, n_tiles - 1)
    next_valid = jnp.logical_and(i + 1 < n_tiles, nv_ref[nxt] > 0)

    def tile_body(s):
        wait_gather(s)
        wait_scatter(s)

        @pl.when(next_valid)
        def _():
            start_gather(nxt, 1 - s)

        xf = _load_token_major(gbuf[s], TM, XE_PITCH)
        ext = gbuf[s][pl.ds(D_MODEL // LANES, TM, stride=XE_PITCH), :]
        gate_lo = ext[:, 1:2]
        gate_hi = ext[:, 2:3]
        xb = xf.astype(_BF)

        def expert(wg, wu, wd, gate):
            a2 = jnp.dot(xb, wg[...], preferred_element_type=_F32)
            u = jnp.dot(xb, wu[...], preferred_element_type=_F32)
            hgt = ((a2 * jnp.tanh(a2) + a2) * (u * gate)).astype(_BF)
            return jnp.dot(hgt, wd[...], preferred_element_type=_F32)

        moe = expert(wg_lo, wu_lo, wd_lo, gate_lo) + expert(wg_hi, wu_hi, wd_hi, gate_hi)
        y = _layer_norm(alpha * xf + moe, g2_ref[...], b2_ref[...])
        if out_token_major:
            _store_token_major(obuf[s], y, X_PITCH)
        else:
            obuf[s][...] = y
        start_scatter(i, s, nv)
        pend[s] = nv

    for s in range(2):
        @pl.when(jnp.logical_and(nv > 0, i % 2 == s))
        def _(s=s):
            tile_body(s)

    @pl.when(i == n_tiles - 1)
    def _():
        wait_scatter(0)
        wait_scatter(1)


def _moe_call(alpha, layer, xe, plan, w, n_tok, out_token_major):
    elo, ehi, nvalid, off, order = plan
    n_tiles = elo.shape[0]
    wg, wu, wd, g2, b2 = w
    if out_token_major:
        obuf_shape, out_rows, out_cols = (TM * X_PITCH, LANES), n_tok * X_PITCH, LANES
    else:
        obuf_shape, out_rows, out_cols = (TM, D_MODEL), n_tok, D_MODEL

    def wspec(shape, which):
        def imap(i, elo_r, ehi_r, *_):
            return (layer, (elo_r, ehi_r)[which][i], 0, 0)
        return pl.BlockSpec((None, None) + shape, imap)

    grid_spec = pltpu.PrefetchScalarGridSpec(
        num_scalar_prefetch=5,
        grid=(n_tiles,),
        in_specs=[
            pl.BlockSpec(memory_space=pl.ANY),
            wspec((D_MODEL, D_FF), 0), wspec((D_MODEL, D_FF), 0), wspec((D_FF, D_MODEL), 0),
            wspec((D_MODEL, D_FF), 1), wspec((D_MODEL, D_FF), 1), wspec((D_FF, D_MODEL), 1),
            pl.BlockSpec((None, 1, D_MODEL), lambda i, *_: (layer, 0, 0)),
            pl.BlockSpec((None, 1, D_MODEL), lambda i, *_: (layer, 0, 0)),
        ],
        out_specs=pl.BlockSpec(memory_space=pl.ANY),
        scratch_shapes=[
            pltpu.VMEM((TM * XE_PITCH, LANES), _F32),
            pltpu.VMEM((TM * XE_PITCH, LANES), _F32),
            pltpu.VMEM(obuf_shape, _F32),
            pltpu.VMEM(obuf_shape, _F32),
            pltpu.SemaphoreType.DMA((2,)),
            pltpu.SemaphoreType.DMA((2,)),
            pltpu.SMEM((2,), jnp.int32),
        ],
    )
    return pl.pallas_call(
        functools.partial(_moe_kernel, alpha, n_tiles, out_token_major),
        grid_spec=grid_spec,
        out_shape=jax.ShapeDtypeStruct((out_rows, out_cols), _F32),
        compiler_params=pltpu.CompilerParams(
            dimension_semantics=("arbitrary",), vmem_limit_bytes=VMEM_LIMIT),
        name="moe",
    )(elo, ehi, nvalid, off, order, xe, wg, wu, wd, wg, wu, wd, g2, b2)


_PAIR_LO = (0, 0, 0, 1, 1, 2)
_PAIR_HI = (1, 2, 3, 2, 3, 3)


def _plan(cls, n_tiles):
    n_tok = cls.shape[0]
    tok = jnp.arange(n_tok, dtype=jnp.int32)
    order = jnp.sort(cls * n_tok + tok) % n_tok
    order = jnp.concatenate([order, jnp.zeros((TM,), jnp.int32)])
    counts = jnp.sum((cls[:, None] == jnp.arange(N_CLASSES, dtype=jnp.int32)[None, :]).astype(jnp.int32), axis=0)
    cstart = jnp.cumsum(counts) - counts
    tiles = (counts + TM - 1) // TM
    tile_end = jnp.cumsum(tiles)
    tile_start = tile_end - tiles
    ti = jnp.arange(n_tiles, dtype=jnp.int32)
    n_used = tile_end[-1]
    tsel = jnp.minimum(ti, n_used - 1)
    tcls = jnp.sum((tile_end[None, :] <= tsel[:, None]).astype(jnp.int32), axis=1)
    within = (tsel - tile_start[tcls]) * TM
    nvalid = jnp.where(ti < n_used, jnp.clip(counts[tcls] - within, 0, TM), 0).astype(jnp.int32)
    off = (cstart[tcls] + within).astype(jnp.int32)
    grp = tcls // N_PAIRS
    pair = tcls % N_PAIRS
    elo = grp * EXPERTS_PER_GROUP + jnp.asarray(_PAIR_LO, jnp.int32)[pair]
    ehi = grp * EXPERTS_PER_GROUP + jnp.asarray(_PAIR_HI, jnp.int32)[pair]
    return elo.astype(jnp.int32), ehi.astype(jnp.int32), nvalid, off, order


def kernel(x, w_in, conv_a_w, w_branch_a, conv_r_w, conv_r_b, w_rg_a, b_rg_a, w_rg_x, b_rg_x, rg_lambda, w_branch_r, w_out, ln1_g, ln1_b, w_router, b_router, w_exp_gate, w_exp_up, w_exp_down, ln2_g, ln2_b):
    n_seq, seq, _ = x.shape
    depth = w_in.shape[0]
    n_tok = n_seq * seq
    assert seq % TT == 0 and n_tok % TM == 0
    n_tiles = n_tok // TM + N_CLASSES
    alpha = (2.0 * depth) ** 0.25

    wrt = w_router.T.astype(_F32)
    wrt_hi = wrt.astype(_BF)
    wrt_lo = (wrt - wrt_hi.astype(_F32)).astype(_BF)
    brt = b_router.astype(_F32).reshape(N_EXPERTS, 1)

    in_scale = jnp.concatenate([jnp.ones((OFF_GA,), _F32), jnp.full((D_IN - OFF_GA,), 0.5, _F32)])
    mix_w = (
        (w_in * in_scale).astype(_BF),
        conv_a_w,
        w_branch_a.astype(_BF),
        conv_r_w,
        conv_r_b.reshape(depth, 1, D_R),
        (0.5 * jnp.concatenate([w_rg_a, w_rg_x], axis=-1)).astype(_BF),
        0.5 * b_rg_a.reshape(depth, 1, D_R),
        0.5 * b_rg_x.reshape(depth, 1, D_R),
        rg_lambda.reshape(depth, 1, D_R),
        (0.25 * w_branch_r).astype(_BF),
        (0.5 * w_out).astype(_BF),
        ln1_g.reshape(depth, 1, D_MODEL),
        ln1_b.reshape(depth, 1, D_MODEL),
        wrt_hi, wrt_lo, brt,
    )
    moe_w = (
        (0.5 * w_exp_gate).astype(_BF), w_exp_up.astype(_BF), w_exp_down.astype(_BF),
        ln2_g.reshape(depth, 1, D_MODEL), ln2_b.reshape(depth, 1, D_MODEL),
    )

    h = x.reshape(n_tok, D_MODEL)
    for l in range(depth):
        xe, route = _mix_call(alpha, l, h, l > 0, n_seq, n_tok, mix_w)
        plan = _plan(route[0].astype(jnp.int32), n_tiles)
        h = _moe_call(alpha, l, xe, plan, moe_w, n_tok, l < depth - 1)
    return h.reshape(n_seq, seq, D_MODEL)
```

```python
import functools

import jax
import jax.numpy as jnp
from jax import lax
from jax.experimental import pallas as pl
from jax.experimental.pallas import tpu as pltpu

D_MODEL = 1024
D_A = 1024
D_R = 1280
N_RG_BLOCKS = 10
RG_BLOCK = 128
RG_C = 8.0
CONV_A_WIDTH = 3
CONV_R_WIDTH = 4
D_IN = 3 * D_A + 2 * D_R + 2 * D_MODEL
N_EXPERTS = 16
N_GROUPS = 4
EXPERTS_PER_GROUP = 4
N_PAIRS = 6
N_CLASSES = N_GROUPS * N_PAIRS
D_FF = 512
LN_EPS = 1e-5

OFF_HA, OFF_BA, OFF_CA = 0, D_A, 2 * D_A
OFF_XR = 3 * D_A
OFF_YR = OFF_XR + D_R
OFF_GA = OFF_YR + D_R
OFF_GR = OFF_GA + D_MODEL
LATE_COLS = D_IN - OFF_YR
LATE_CHUNK = 256

SUBLANES = 8
LANES = 128
EXT = LANES
X_PITCH = D_MODEL // LANES
XE_PITCH = (D_MODEL + EXT) // LANES
TT = 512
TM = 256
DMA_UNROLL = 8
VMEM_LIMIT = 56 * 1024 * 1024

_BF = jnp.bfloat16
_F32 = jnp.float32

GELU_K1 = 0.7978845608028654
GELU_K2 = GELU_K1 * 0.044715


def _layer_norm(y, g, b):
    mu = jnp.mean(y, axis=-1, keepdims=True)
    yc = y - mu
    var = jnp.mean(yc * yc, axis=-1, keepdims=True)
    return yc * lax.rsqrt(var + LN_EPS) * g + b


def _first_index_of(vals, target):
    idx = jnp.full(target.shape, float(len(vals) - 1), _F32)
    for j in range(len(vals) - 2, -1, -1):
        idx = jnp.where(vals[j] == target, float(j), idx)
    return idx


def _route(logits_t):
    m = jnp.max(logits_t, axis=0, keepdims=True)
    e = jnp.exp(logits_t - m)
    p = e / jnp.sum(e, axis=0, keepdims=True)
    neg = jnp.full((1, logits_t.shape[1]), -jnp.inf, _F32)
    scores, m1s, i1s, m2s, i2s = [], [], [], [], []
    for g in range(N_GROUPS):
        v = [p[g * EXPERTS_PER_GROUP + j:g * EXPERTS_PER_GROUP + j + 1, :] for j in range(EXPERTS_PER_GROUP)]
        m1 = jnp.maximum(jnp.maximum(v[0], v[1]), jnp.maximum(v[2], v[3]))
        i1 = _first_index_of(v, m1)
        w = [jnp.where(i1 == float(j), neg, v[j]) for j in range(EXPERTS_PER_GROUP)]
        m2 = jnp.maximum(jnp.maximum(w[0], w[1]), jnp.maximum(w[2], w[3]))
        i2 = _first_index_of(w, m2)
        scores.append(m1 + m2)
        m1s.append(m1); i1s.append(i1); m2s.append(m2); i2s.append(i2)
    best = jnp.maximum(jnp.maximum(scores[0], scores[1]), jnp.maximum(scores[2], scores[3]))
    gsel = _first_index_of(scores, best)

    def pick(xs):
        out = xs[N_GROUPS - 1]
        for g in range(N_GROUPS - 2, -1, -1):
            out = jnp.where(gsel == float(g), xs[g], out)
        return out

    m1, i1, m2, i2 = pick(m1s), pick(i1s), pick(m2s), pick(i2s)
    den = m1 + m2
    gate1, gate2 = m1 / den, m2 / den
    first_is_lo = i1 < i2
    lo = jnp.where(first_is_lo, i1, i2)
    hi = jnp.where(first_is_lo, i2, i1)
    g_lo = jnp.where(first_is_lo, gate1, gate2)
    g_hi = jnp.where(first_is_lo, gate2, gate1)
    pair = lo * (7.0 - lo) * 0.5 + (hi - lo - 1.0)
    cls = gsel * float(N_PAIRS) + pair
    return cls, g_lo, g_hi


def _load_token_major(ref, n_rows, pitch, lead=()):
    cols = [ref[lead + (pl.ds(c, n_rows, stride=pitch), slice(None))] for c in range(D_MODEL // LANES)]
    return jnp.concatenate(cols, axis=1)


def _store_token_major(ref, val, pitch, lead=()):
    n_rows = val.shape[0]
    for c in range(val.shape[1] // LANES):
        ref[lead + (pl.ds(c, n_rows, stride=pitch), slice(None))] = val[:, c * LANES:(c + 1) * LANES]


def _causal_conv(u, hist_ref, w_ref, n_taps, out_ref, cols=slice(None), bias=None):
    tt = u.shape[0]

    def tap(d):
        return w_ref[n_taps - 1 - d:n_taps - d, cols]

    def tap0(val):
        out = val * tap(0)
        return out if bias is None else out + bias

    acc = tap0(u)
    for d in range(1, n_taps):
        acc = acc + pltpu.roll(u, d, 0) * tap(d)
    out_ref[:, cols] = acc
    head = u[0:SUBLANES, :]
    hist = hist_ref[:, cols]
    row = lax.broadcasted_iota(jnp.int32, head.shape, 0)
    acc_head = tap0(head)
    for d in range(1, n_taps):
        shifted = jnp.where(row < d, pltpu.roll(hist, d, 0), pltpu.roll(head, d, 0))
        acc_head = acc_head + shifted * tap(d)
    out_ref[pl.ds(0, SUBLANES), cols] = acc_head
    hist_ref[:, cols] = u[tt - SUBLANES:tt, :]


def _mix_kernel(alpha, x_token_major,
                x_ref, win_ref, caw_ref, wa_ref, crw_ref, crb_ref, wrg_ref, brga_ref, brgx_ref,
                lam_ref, wr_ref, wo_ref, g1_ref, b1_ref, wrt_hi_ref, wrt_lo_ref, brt_ref,
                xe_ref, route_ref,
                uhist, xrhist, cabuf, xcbuf, abuf, bbuf, hbuf, hcarry, latebuf, vabuf):
    t = pl.program_id(1)
    tt = route_ref.shape[1]

    @pl.when(t == 0)
    def _():
        uhist[...] = jnp.zeros_like(uhist)
        xrhist[...] = jnp.zeros_like(xrhist)
        hcarry[...] = jnp.zeros_like(hcarry)

    x = _load_token_major(x_ref, tt, X_PITCH) if x_token_major else x_ref[...]
    xb = x.astype(_BF)

    def proj(off, width):
        return jnp.dot(xb, win_ref[:, off:off + width], preferred_element_type=_F32)

    def mixer_a_chunk(c):
        cols = slice(c * LATE_CHUNK, (c + 1) * LATE_CHUNK)
        u = proj(OFF_CA + c * LATE_CHUNK, LATE_CHUNK) * proj(OFF_HA + c * LATE_CHUNK, LATE_CHUNK)
        _causal_conv(u, uhist, caw_ref, CONV_A_WIDTH, cabuf, cols=cols)
        vabuf[:, cols] = (proj(OFF_BA + c * LATE_CHUNK, LATE_CHUNK) * cabuf[:, cols]).astype(_BF)

    def late_chunk(j):
        latebuf[:, j * LATE_CHUNK:(j + 1) * LATE_CHUNK] = proj(OFF_YR + j * LATE_CHUNK, LATE_CHUNK)

    _causal_conv(proj(OFF_XR, D_R), xrhist, crw_ref, CONV_R_WIDTH, xcbuf, bias=crb_ref[...])
    z = -lam_ref[...]
    softplus_neg_lam = jnp.maximum(z, 0.0) + jnp.log1p(jnp.exp(-jnp.abs(z)))
    c_half = (-0.5 * RG_C) * softplus_neg_lam
    row = lax.broadcasted_iota(jnp.int32, (tt // SUBLANES, SUBLANES, RG_BLOCK), 1)
    side_work = [functools.partial(mixer_a_chunk, c) for c in range(D_A // LATE_CHUNK)]
    side_work += [functools.partial(late_chunk, j) for j in range(LATE_COLS // LATE_CHUNK)]
    side_cost = [3] * (D_A // LATE_CHUNK) + [1] * (LATE_COLS // LATE_CHUNK)
    per_head = sum(side_cost) / N_RG_BLOCKS
    done, spent = 0, 0.0
    for h in range(N_RG_BLOCKS):
        while done < len(side_work) and (spent < (h + 1) * per_head or h == N_RG_BLOCKS - 1):
            side_work[done]()
            spent += side_cost[done]
            done += 1
        sl = slice(h * RG_BLOCK, (h + 1) * RG_BLOCK)
        xc = xcbuf[:, sl]
        gh = jnp.dot(xc.astype(_BF), wrg_ref[h], preferred_element_type=_F32)
        t_r = jnp.tanh(gh[:, :RG_BLOCK] + brga_ref[:, sl])
        t_i = jnp.tanh(gh[:, RG_BLOCK:] + brgx_ref[:, sl])
        log_a = c_half[:, sl] * t_r + c_half[:, sl]
        a_t = jnp.exp(log_a)
        b_t = jnp.sqrt(1.0 - a_t * a_t) * (t_i * xc + xc)
        a3 = a_t.reshape(tt // SUBLANES, SUBLANES, RG_BLOCK)
        b3 = b_t.reshape(tt // SUBLANES, SUBLANES, RG_BLOCK)
        for k in (1, 2, 4):
            keep = row >= k
            b3 = jnp.where(keep, a3 * pltpu.roll(b3, k, 1) + b3, b3)
            a3 = jnp.where(keep, a3 * pltpu.roll(a3, k, 1), a3)
        abuf[:, sl] = a3.reshape(tt, RG_BLOCK)
        bbuf[:, sl] = b3.reshape(tt, RG_BLOCK)

    def carry_groups(g, h_prev):
        o = pl.multiple_of(g * SUBLANES, SUBLANES)
        hg = abuf[pl.ds(o, SUBLANES), :] * h_prev + bbuf[pl.ds(o, SUBLANES), :]
        hbuf[pl.ds(o, SUBLANES), :] = hg
        return hg[SUBLANES - 1:SUBLANES, :]

    hcarry[...] = lax.fori_loop(0, tt // SUBLANES, carry_groups, hcarry[...], unroll=4)
    ya = jnp.dot(vabuf[...], wa_ref[...], preferred_element_type=_F32)
    v = latebuf[:, :D_R]
    th = jnp.tanh(v * (GELU_K2 * (v * v) + GELU_K1))
    vr = (hbuf[...] * (v * th + v)).astype(_BF)
    yr = jnp.dot(vr, wr_ref[...], preferred_element_type=_F32)

    t_a = jnp.tanh(latebuf[:, OFF_GA - OFF_YR:OFF_GR - OFF_YR])
    t_g = jnp.tanh(latebuf[:, OFF_GR - OFF_YR:])
    merged2 = (t_a * ya + ya) + (t_g * yr + yr)
    o = jnp.dot(merged2.astype(_BF), wo_ref[...], preferred_element_type=_F32)
    x1 = _layer_norm(alpha * x + o, g1_ref[...], b1_ref[...])

    x1_hi = x1.astype(_BF)
    x1_lo = (x1 - x1_hi.astype(_F32)).astype(_BF)
    nt_dims = (((1,), (1,)), ((), ()))
    logits_t = (lax.dot_general(wrt_hi_ref[...], x1_hi, nt_dims, preferred_element_type=_F32)
                + lax.dot_general(wrt_hi_ref[...], x1_lo, nt_dims, preferred_element_type=_F32)
                + lax.dot_general(wrt_lo_ref[...], x1_hi, nt_dims, preferred_element_type=_F32)
                + brt_ref[...])
    cls, g_lo, g_hi = _route(logits_t)
    rsel = lax.broadcasted_iota(jnp.int32, (SUBLANES, tt), 0)
    route_ref[...] = jnp.where(rsel == 0, cls, jnp.where(rsel == 1, g_lo, jnp.where(rsel == 2, g_hi, 0.0)))
    esel = lax.broadcasted_iota(jnp.int32, (EXT, tt), 0)
    ext_t = jnp.where(esel == 0, cls, jnp.where(esel == 1, g_lo, jnp.where(esel == 2, g_hi, 0.0)))
    _store_token_major(xe_ref, jnp.concatenate([x1, ext_t.T], axis=1), XE_PITCH)


def _mix_call(alpha, layer, x2d, x_token_major, n_seq, n_tok, w):
    seq = n_tok // n_seq
    nt = seq // TT

    def layer_const(shape):
        zeros = (0,) * len(shape)
        return pl.BlockSpec((None,) + shape, lambda b, t: (layer,) + zeros, pipeline_mode=pl.Buffered(1))

    def const(shape):
        zeros = (0,) * len(shape)
        return pl.BlockSpec(shape, lambda b, t: zeros, pipeline_mode=pl.Buffered(1))

    in_specs = [
        (pl.BlockSpec((TT * X_PITCH, LANES), lambda b, t: (b * nt + t, 0)) if x_token_major
         else pl.BlockSpec((TT, D_MODEL), lambda b, t: (b * nt + t, 0))),
        layer_const((D_MODEL, D_IN)),
        layer_const((CONV_A_WIDTH, D_A)),
        layer_const((D_A, D_MODEL)),
        layer_const((CONV_R_WIDTH, D_R)),
        layer_const((1, D_R)),
        layer_const((N_RG_BLOCKS, RG_BLOCK, 2 * RG_BLOCK)),
        layer_const((1, D_R)),
        layer_const((1, D_R)),
        layer_const((1, D_R)),
        layer_const((D_R, D_MODEL)),
        layer_const((D_MODEL, D_MODEL)),
        layer_const((1, D_MODEL)),
        layer_const((1, D_MODEL)),
        const((N_EXPERTS, D_MODEL)),
        const((N_EXPERTS, D_MODEL)),
        const((N_EXPERTS, 1)),
    ]
    out_specs = [
        pl.BlockSpec((TT * XE_PITCH, LANES), lambda b, t: (b * nt + t, 0)),
        pl.BlockSpec((SUBLANES, TT), lambda b, t: (0, b * nt + t)),
    ]
    out_shape = [
        jax.ShapeDtypeStruct((n_tok * XE_PITCH, LANES), _F32),
        jax.ShapeDtypeStruct((SUBLANES, n_tok), _F32),
    ]
    scratch = [
        pltpu.VMEM((SUBLANES, D_A), _F32),
        pltpu.VMEM((SUBLANES, D_R), _F32),
        pltpu.VMEM((TT, D_A), _F32),
        pltpu.VMEM((TT, D_R), _F32),
        pltpu.VMEM((TT, D_R), _F32),
        pltpu.VMEM((TT, D_R), _F32),
        pltpu.VMEM((TT, D_R), _F32),
        pltpu.VMEM((1, D_R), _F32),
        pltpu.VMEM((TT, LATE_COLS), _F32),
        pltpu.VMEM((TT, D_A), _BF),
    ]
    return pl.pallas_call(
        functools.partial(_mix_kernel, alpha, x_token_major),
        grid=(n_seq, nt),
        in_specs=in_specs,
        out_specs=out_specs,
        out_shape=out_shape,
        scratch_shapes=scratch,
        compiler_params=pltpu.CompilerParams(
            dimension_semantics=("arbitrary", "arbitrary"), vmem_limit_bytes=VMEM_LIMIT),
        name="mix",
    )(x2d, *w)


def _moe_kernel(alpha, n_tiles, out_token_major,
                elo_ref, ehi_ref, nv_ref, off_ref, order_ref,
                xe_hbm,
                wg_lo, wu_lo, wd_lo, wg_hi, wu_hi, wd_hi, g2_ref, b2_ref,
                out_hbm,
                gbuf0, gbuf1, obuf0, obuf1, gsem, ssem, pend):
    gbuf = (gbuf0, gbuf1)
    obuf = (obuf0, obuf1)
    i = pl.program_id(0)
    nv = nv_ref[i]
    o_pitch = X_PITCH if out_token_major else 1

    def gather_copy(tok, r, s):
        return pltpu.make_async_copy(xe_hbm.at[pl.ds(tok * XE_PITCH, XE_PITCH), :],
                                     gbuf[s].at[pl.ds(r * XE_PITCH, XE_PITCH), :], gsem.at[s])

    def scatter_copy(tok, r, s, n_rows=1):
        src = r * o_pitch if isinstance(r, int) else pl.multiple_of(r * o_pitch, o_pitch)
        return pltpu.make_async_copy(obuf[s].at[pl.ds(src, n_rows * o_pitch), :],
                                     out_hbm.at[pl.ds(tok * o_pitch, n_rows * o_pitch), :], ssem.at[s])

    def start_gather(tile, s):
        base = off_ref[tile]
        for r in range(TM):
            gather_copy(order_ref[base + r], r, s).start(priority=r % 2)

    def wait_gather(s):
        pltpu.make_async_copy(xe_hbm.at[pl.ds(0, TM * XE_PITCH), :], gbuf[s], gsem.at[s]).wait()

    def start_scatter(tile, s, n_rows):
        base = off_ref[tile]
        n_chunks = lax.shift_right_logical(n_rows, DMA_UNROLL.bit_length() - 1)

        def chunk(c, carry):
            r0 = c * DMA_UNROLL
            for j in range(DMA_UNROLL):
                scatter_copy(order_ref[base + r0 + j], r0 + j, s).start(priority=j % 2)
            return carry
        lax.fori_loop(0, n_chunks, chunk, 0)

        def tail(r, carry):
            scatter_copy(order_ref[base + r], r, s).start()
            return carry
        lax.fori_loop(n_chunks * DMA_UNROLL, n_rows, tail, 0)

    def wait_scatter(s):
        n_rows = pend[s]
        width = TM
        while width >= 1:
            @pl.when(jnp.bitwise_and(n_rows, width) != 0)
            def _(width=width):
                scatter_copy(0, 0, s, n_rows=width).wait()
            width //= 2
        pend[s] = 0

    @pl.when(i == 0)
    def _():
        pend[0] = 0
        pend[1] = 0
        start_gather(0, 0)

    nxt = jnp.minimum(i + 1, n_tiles - 1)
    next_valid = jnp.logical_and(i + 1 < n_tiles, nv_ref[nxt] > 0)

    def tile_body(s):
        wait_gather(s)
        wait_scatter(s)

        @pl.when(next_valid)
        def _():
            start_gather(nxt, 1 - s)

        xf = _load_token_major(gbuf[s], TM, XE_PITCH)
        ext = gbuf[s][pl.ds(D_MODEL // LANES, TM, stride=XE_PITCH), :]
        gate_lo = ext[:, 1:2]
        gate_hi = ext[:, 2:3]
        xb = xf.astype(_BF)

        def expert(wg, wu, wd, gate):
            a2 = jnp.dot(xb, wg[...], preferred_element_type=_F32)
            u = jnp.dot(xb, wu[...], preferred_element_type=_F32)
            hgt = ((a2 * jnp.tanh(a2) + a2) * (u * gate)).astype(_BF)
            return jnp.dot(hgt, wd[...], preferred_element_type=_F32)

        moe = expert(wg_lo, wu_lo, wd_lo, gate_lo) + expert(wg_hi, wu_hi, wd_hi, gate_hi)
        y = _layer_norm(alpha * xf + moe, g2_ref[...], b2_ref[...])
        if out_token_major:
            _store_token_major(obuf[s], y, X_PITCH)
        else:
            obuf[s][...] = y
        start_scatter(i, s, nv)
        pend[s] = nv

    for s in range(2):
        @pl.when(jnp.logical_and(nv > 0, i % 2 == s))
        def _(s=s):
            tile_body(s)

    @pl.when(i == n_tiles - 1)
    def _():
        wait_scatter(0)
        wait_scatter(1)


def _moe_call(alpha, layer, xe, plan, w, n_tok, out_token_major):
    elo, ehi, nvalid, off, order = plan
    n_tiles = elo.shape[0]
    wg, wu, wd, g2, b2 = w
    if out_token_major:
        obuf_shape, out_rows, out_cols = (TM * X_PITCH, LANES), n_tok * X_PITCH, LANES
    else:
        obuf_shape, out_rows, out_cols = (TM, D_MODEL), n_tok, D_MODEL

    def wspec(shape, which):
        def imap(i, elo_r, ehi_r, *_):
            return (layer, (elo_r, ehi_r)[which][i], 0, 0)
        return pl.BlockSpec((None, None) + shape, imap)

    grid_spec = pltpu.PrefetchScalarGridSpec(
        num_scalar_prefetch=5,
        grid=(n_tiles,),
        in_specs=[
            pl.BlockSpec(memory_space=pl.ANY),
            wspec((D_MODEL, D_FF), 0), wspec((D_MODEL, D_FF), 0), wspec((D_FF, D_MODEL), 0),
            wspec((D_MODEL, D_FF), 1), wspec((D_MODEL, D_FF), 1), wspec((D_FF, D_MODEL), 1),
            pl.BlockSpec((None, 1, D_MODEL), lambda i, *_: (layer, 0, 0)),
            pl.BlockSpec((None, 1, D_MODEL), lambda i, *_: (layer, 0, 0)),
        ],
        out_specs=pl.BlockSpec(memory_space=pl.ANY),
        scratch_shapes=[
            pltpu.VMEM((TM * XE_PITCH, LANES), _F32),
            pltpu.VMEM((TM * XE_PITCH, LANES), _F32),
            pltpu.VMEM(obuf_shape, _F32),
            pltpu.VMEM(obuf_shape, _F32),
            pltpu.SemaphoreType.DMA((2,)),
            pltpu.SemaphoreType.DMA((2,)),
            pltpu.SMEM((2,), jnp.int32),
        ],
    )
    return pl.pallas_call(
        functools.partial(_moe_kernel, alpha, n_tiles, out_token_major),
        grid_spec=grid_spec,
        out_shape=jax.ShapeDtypeStruct((out_rows, out_cols), _F32),
        compiler_params=pltpu.CompilerParams(
            dimension_semantics=("arbitrary",), vmem_limit_bytes=VMEM_LIMIT),
        name="moe",
    )(elo, ehi, nvalid, off, order, xe, wg, wu, wd, wg, wu, wd, g2, b2)


_PAIR_LO = (0, 0, 0, 1, 1, 2)
_PAIR_HI = (1, 2, 3, 2, 3, 3)


def _plan(cls, n_tiles):
    n_tok = cls.shape[0]
    tok = jnp.arange(n_tok, dtype=jnp.int32)
    order = jnp.sort(cls * n_tok + tok) % n_tok
    order = jnp.concatenate([order, jnp.zeros((TM,), jnp.int32)])
    counts = jnp.sum((cls[:, None] == jnp.arange(N_CLASSES, dtype=jnp.int32)[None, :]).astype(jnp.int32), axis=0)
    cstart = jnp.cumsum(counts) - counts
    tiles = (counts + TM - 1) // TM
    tile_end = jnp.cumsum(tiles)
    tile_start = tile_end - tiles
    ti = jnp.arange(n_tiles, dtype=jnp.int32)
    n_used = tile_end[-1]
    tsel = jnp.minimum(ti, n_used - 1)
    tcls = jnp.sum((tile_end[None, :] <= tsel[:, None]).astype(jnp.int32), axis=1)
    within = (tsel - tile_start[tcls]) * TM
    nvalid = jnp.where(ti < n_used, jnp.clip(counts[tcls] - within, 0, TM), 0).astype(jnp.int32)
    off = (cstart[tcls] + within).astype(jnp.int32)
    grp = tcls // N_PAIRS
    pair = tcls % N_PAIRS
    elo = grp * EXPERTS_PER_GROUP + jnp.asarray(_PAIR_LO, jnp.int32)[pair]
    ehi = grp * EXPERTS_PER_GROUP + jnp.asarray(_PAIR_HI, jnp.int32)[pair]
    return elo.astype(jnp.int32), ehi.astype(jnp.int32), nvalid, off, order


def kernel(x, w_in, conv_a_w, w_branch_a, conv_r_w, conv_r_b, w_rg_a, b_rg_a, w_rg_x, b_rg_x, rg_lambda, w_branch_r, w_out, ln1_g, ln1_b, w_router, b_router, w_exp_gate, w_exp_up, w_exp_down, ln2_g, ln2_b):
    n_seq, seq, _ = x.shape
    depth = w_in.shape[0]
    n_tok = n_seq * seq
    assert seq % TT == 0 and n_tok % TM == 0
    n_tiles = n_tok // TM + N_CLASSES
    alpha = (2.0 * depth) ** 0.25

    wrt = w_router.T.astype(_F32)
    wrt_hi = wrt.astype(_BF)
    wrt_lo = (wrt - wrt_hi.astype(_F32)).astype(_BF)
    brt = b_router.astype(_F32).reshape(N_EXPERTS, 1)

    in_scale = jnp.concatenate([jnp.ones((OFF_GA,), _F32), jnp.full((D_IN - OFF_GA,), 0.5, _F32)])
    mix_w = (
        (w_in * in_scale).astype(_BF),
        conv_a_w,
        w_branch_a.astype(_BF),
        conv_r_w,
        conv_r_b.reshape(depth, 1, D_R),
        (0.5 * jnp.concatenate([w_rg_a, w_rg_x], axis=-1)).astype(_BF),
        0.5 * b_rg_a.reshape(depth, 1, D_R),
        0.5 * b_rg_x.reshape(depth, 1, D_R),
        rg_lambda.reshape(depth, 1, D_R),
        (0.25 * w_branch_r).astype(_BF),
        (0.5 * w_out).astype(_BF),
        ln1_g.reshape(depth, 1, D_MODEL),
        ln1_b.reshape(depth, 1, D_MODEL),
        wrt_hi, wrt_lo, brt,
    )
    moe_w = (
        (0.5 * w_exp_gate).astype(_BF), w_exp_up.astype(_BF), w_exp_down.astype(_BF),
        ln2_g.reshape(depth, 1, D_MODEL), ln2_b.reshape(depth, 1, D_MODEL),
    )

    h = x.reshape(n_tok, D_MODEL)
    for l in range(depth):
        xe, route = _mix_call(alpha, l, h, l > 0, n_seq, n_tok, mix_w)
        plan = _plan(route[0].astype(jnp.int32), n_tiles)
        h = _moe_call(alpha, l, xe, plan, moe_w, n_tok, l < depth - 1)
    return h.reshape(n_seq, seq, D_MODEL)
```

```python
import functools

import jax
import jax.numpy as jnp
from jax import lax
from jax.experimental import pallas as pl
from jax.experimental.pallas import tpu as pltpu

D_MODEL = 1024
D_A = 1024
D_R = 1280
N_RG_BLOCKS = 10
RG_BLOCK = 128
RG_C = 8.0
CONV_A_WIDTH = 3
CONV_R_WIDTH = 4
D_IN = 3 * D_A + 2 * D_R + 2 * D_MODEL
N_EXPERTS = 16
N_GROUPS = 4
EXPERTS_PER_GROUP = 4
N_PAIRS = 6
N_CLASSES = N_GROUPS * N_PAIRS
D_FF = 512
LN_EPS = 1e-5

OFF_HA, OFF_BA, OFF_CA = 0, D_A, 2 * D_A
OFF_XR = 3 * D_A
OFF_YR = OFF_XR + D_R
OFF_GA = OFF_YR + D_R
OFF_GR = OFF_GA + D_MODEL
LATE_COLS = D_IN - OFF_YR
LATE_CHUNK = 256

SUBLANES = 8
LANES = 128
EXT = LANES
X_PITCH = D_MODEL // LANES
XE_PITCH = (D_MODEL + EXT) // LANES
TT = 512
TM = 256
DMA_UNROLL = 8
VMEM_LIMIT = 56 * 1024 * 1024

_BF = jnp.bfloat16
_F32 = jnp.float32

GELU_K1 = 0.7978845608028654
GELU_K2 = GELU_K1 * 0.044715


def _layer_norm(y, g, b):
    mu = jnp.mean(y, axis=-1, keepdims=True)
    yc = y - mu
    var = jnp.mean(yc * yc, axis=-1, keepdims=True)
    return yc * lax.rsqrt(var + LN_EPS) * g + b


def _first_index_of(vals, target):
    idx = jnp.full(target.shape, float(len(vals) - 1), _F32)
    for j in range(len(vals) - 2, -1, -1):
        idx = jnp.where(vals[j] == target, float(j), idx)
    return idx


def _route(logits_t):
    m = jnp.max(logits_t, axis=0, keepdims=True)
    e = jnp.exp(logits_t - m)
    p = e / jnp.sum(e, axis=0, keepdims=True)
    neg = jnp.full((1, logits_t.shape[1]), -jnp.inf, _F32)
    scores, m1s, i1s, m2s, i2s = [], [], [], [], []
    for g in range(N_GROUPS):
        v = [p[g * EXPERTS_PER_GROUP + j:g * EXPERTS_PER_GROUP + j + 1, :] for j in range(EXPERTS_PER_GROUP)]
        m1 = jnp.maximum(jnp.maximum(v[0], v[1]), jnp.maximum(v[2], v[3]))
        i1 = _first_index_of(v, m1)
        w = [jnp.where(i1 == float(j), neg, v[j]) for j in range(EXPERTS_PER_GROUP)]
        m2 = jnp.maximum(jnp.maximum(w[0], w[1]), jnp.maximum(w[2], w[3]))
        i2 = _first_index_of(w, m2)
        scores.append(m1 + m2)
        m1s.append(m1); i1s.append(i1); m2s.append(m2); i2s.append(i2)
    best = jnp.maximum(jnp.maximum(scores[0], scores[1]), jnp.maximum(scores[2], scores[3]))
    gsel = _first_index_of(scores, best)

    def pick(xs):
        out = xs[N_GROUPS - 1]
        for g in range(N_GROUPS - 2, -1, -1):
            out = jnp.where(gsel == float(g), xs[g], out)
        return out

    m1, i1, m2, i2 = pick(m1s), pick(i1s), pick(m2s), pick(i2s)
    den = m1 + m2
    gate1, gate2 = m1 / den, m2 / den
    first_is_lo = i1 < i2
    lo = jnp.where(first_is_lo, i1, i2)
    hi = jnp.where(first_is_lo, i2, i1)
    g_lo = jnp.where(first_is_lo, gate1, gate2)
    g_hi = jnp.where(first_is_lo, gate2, gate1)
    pair = lo * (7.0 - lo) * 0.5 + (hi - lo - 1.0)
    cls = gsel * float(N_PAIRS) + pair
    return cls, g_lo, g_hi


def _load_token_major(ref, n_rows, pitch, lead=()):
    cols = [ref[lead + (pl.ds(c, n_rows, stride=pitch), slice(None))] for c in range(D_MODEL // LANES)]
    return jnp.concatenate(cols, axis=1)


def _store_token_major(ref, val, pitch, lead=()):
    n_rows = val.shape[0]
    for c in range(val.shape[1] // LANES):
        ref[lead + (pl.ds(c, n_rows, stride=pitch), slice(None))] = val[:, c * LANES:(c + 1) * LANES]


def _causal_conv(u, hist_ref, w_ref, n_taps, out_ref, cols=slice(None), bias=None):
    tt = u.shape[0]

    def tap(d):
        return w_ref[n_taps - 1 - d:n_taps - d, cols]

    def tap0(val):
        out = val * tap(0)
        return out if bias is None else out + bias

    acc = tap0(u)
    for d in range(1, n_taps):
        acc = acc + pltpu.roll(u, d, 0) * tap(d)
    out_ref[:, cols] = acc
    head = u[0:SUBLANES, :]
    hist = hist_ref[:, cols]
    row = lax.broadcasted_iota(jnp.int32, head.shape, 0)
    acc_head = tap0(head)
    for d in range(1, n_taps):
        shifted = jnp.where(row < d, pltpu.roll(hist, d, 0), pltpu.roll(head, d, 0))
        acc_head = acc_head + shifted * tap(d)
    out_ref[pl.ds(0, SUBLANES), cols] = acc_head
    hist_ref[:, cols] = u[tt - SUBLANES:tt, :]


def _mix_kernel(alpha, x_token_major,
                x_ref, win_ref, caw_ref, wa_ref, crw_ref, crb_ref, wrg_ref, brga_ref, brgx_ref,
                lam_ref, wr_ref, wo_ref, g1_ref, b1_ref, wrt_hi_ref, wrt_lo_ref, brt_ref,
                xe_ref, route_ref,
                uhist, xrhist, cabuf, xcbuf, abuf, bbuf, hbuf, hcarry, latebuf, vabuf):
    t = pl.program_id(1)
    tt = route_ref.shape[1]

    @pl.when(t == 0)
    def _():
        uhist[...] = jnp.zeros_like(uhist)
        xrhist[...] = jnp.zeros_like(xrhist)
        hcarry[...] = jnp.zeros_like(hcarry)

    x = _load_token_major(x_ref, tt, X_PITCH) if x_token_major else x_ref[...]
    xb = x.astype(_BF)

    def proj(off, width):
        return jnp.dot(xb, win_ref[:, off:off + width], preferred_element_type=_F32)

    def mixer_a_chunk(c):
        cols = slice(c * LATE_CHUNK, (c + 1) * LATE_CHUNK)
        u = proj(OFF_CA + c * LATE_CHUNK, LATE_CHUNK) * proj(OFF_HA + c * LATE_CHUNK, LATE_CHUNK)
        _causal_conv(u, uhist, caw_ref, CONV_A_WIDTH, cabuf, cols=cols)
        vabuf[:, cols] = (proj(OFF_BA + c * LATE_CHUNK, LATE_CHUNK) * cabuf[:, cols]).astype(_BF)

    def late_chunk(j):
        latebuf[:, j * LATE_CHUNK:(j + 1) * LATE_CHUNK] = proj(OFF_YR + j * LATE_CHUNK, LATE_CHUNK)

    def xr_chunk(c):
        cols = slice(c * LATE_CHUNK, (c + 1) * LATE_CHUNK)
        _causal_conv(proj(OFF_XR + c * LATE_CHUNK, LATE_CHUNK), xrhist, crw_ref, CONV_R_WIDTH, xcbuf,
                     cols=cols, bias=crb_ref[:, cols])

    heads_per_chunk = LATE_CHUNK // RG_BLOCK
    z = -lam_ref[...]
    softplus_neg_lam = jnp.maximum(z, 0.0) + jnp.log1p(jnp.exp(-jnp.abs(z)))
    c_half = (-0.5 * RG_C) * softplus_neg_lam
    row = lax.broadcasted_iota(jnp.int32, (tt // SUBLANES, SUBLANES, RG_BLOCK), 1)
    side_work = [functools.partial(mixer_a_chunk, c) for c in range(D_A // LATE_CHUNK)]
    side_work += [functools.partial(late_chunk, j) for j in range(LATE_COLS // LATE_CHUNK)]
    side_cost = [3] * (D_A // LATE_CHUNK) + [1] * (LATE_COLS // LATE_CHUNK)
    per_head = sum(side_cost) / N_RG_BLOCKS
    done, spent = 0, 0.0
    for h in range(N_RG_BLOCKS):
        if h % heads_per_chunk == 0:
            xr_chunk(h // heads_per_chunk)
        while done < len(side_work) and (spent < (h + 1) * per_head or h == N_RG_BLOCKS - 1):
            side_work[done]()
            spent += side_cost[done]
            done += 1
        sl = slice(h * RG_BLOCK, (h + 1) * RG_BLOCK)
        xc = xcbuf[:, sl]
        gh = jnp.dot(xc.astype(_BF), wrg_ref[h], preferred_element_type=_F32)
        t_r = jnp.tanh(gh[:, :RG_BLOCK] + brga_ref[:, sl])
        t_i = jnp.tanh(gh[:, RG_BLOCK:] + brgx_ref[:, sl])
        log_a = c_half[:, sl] * t_r + c_half[:, sl]
        a_t = jnp.exp(log_a)
        b_t = jnp.exp(0.5 * jnp.log(1.0 - a_t * a_t)) * (t_i * xc + xc)
        a3 = a_t.reshape(tt // SUBLANES, SUBLANES, RG_BLOCK)
        b3 = b_t.reshape(tt // SUBLANES, SUBLANES, RG_BLOCK)
        for k in (1, 2, 4):
            keep = row >= k
            b3 = jnp.where(keep, a3 * pltpu.roll(b3, k, 1) + b3, b3)
            a3 = jnp.where(keep, a3 * pltpu.roll(a3, k, 1), a3)
        abuf[:, sl] = a3.reshape(tt, RG_BLOCK)
        bbuf[:, sl] = b3.reshape(tt, RG_BLOCK)

    def carry_groups(g, h_prev):
        o = pl.multiple_of(g * SUBLANES, SUBLANES)
        hg = abuf[pl.ds(o, SUBLANES), :] * h_prev + bbuf[pl.ds(o, SUBLANES), :]
        hbuf[pl.ds(o, SUBLANES), :] = hg
        return hg[SUBLANES - 1:SUBLANES, :]

    hcarry[...] = lax.fori_loop(0, tt // SUBLANES, carry_groups, hcarry[...], unroll=4)
    ya = jnp.dot(vabuf[...], wa_ref[...], preferred_element_type=_F32)
    v = latebuf[:, :D_R]
    th = jnp.tanh(v * (GELU_K2 * (v * v) + GELU_K1))
    vr = (hbuf[...] * (v * th + v)).astype(_BF)
    yr = jnp.dot(vr, wr_ref[...], preferred_element_type=_F32)

    t_a = jnp.tanh(latebuf[:, OFF_GA - OFF_YR:OFF_GR - OFF_YR])
    t_g = jnp.tanh(latebuf[:, OFF_GR - OFF_YR:])
    merged2 = (t_a * ya + ya) + (t_g * yr + yr)
    o = jnp.dot(merged2.astype(_BF), wo_ref[...], preferred_element_type=_F32)
    x1 = _layer_norm(alpha * x + o, g1_ref[...], b1_ref[...])

    x1_hi = x1.astype(_BF)
    x1_lo = (x1 - x1_hi.astype(_F32)).astype(_BF)
    nt_dims = (((1,), (1,)), ((), ()))
    logits_t = (lax.dot_general(wrt_hi_ref[...], x1_hi, nt_dims, preferred_element_type=_F32)
                + lax.dot_general(wrt_hi_ref[...], x1_lo, nt_dims, preferred_element_type=_F32)
                + lax.dot_general(wrt_lo_ref[...], x1_hi, nt_dims, preferred_element_type=_F32)
                + brt_ref[...])
    cls, g_lo, g_hi = _route(logits_t)
    rsel = lax.broadcasted_iota(jnp.int32, (SUBLANES, tt), 0)
    route_ref[...] = jnp.where(rsel == 0, cls, jnp.where(rsel == 1, g_lo, jnp.where(rsel == 2, g_hi, 0.0)))
    esel = lax.broadcasted_iota(jnp.int32, (EXT, tt), 0)
    ext_t = jnp.where(esel == 0, cls, jnp.where(esel == 1, g_lo, jnp.where(esel == 2, g_hi, 0.0)))
    _store_token_major(xe_ref, jnp.concatenate([x1, ext_t.T], axis=1), XE_PITCH)


def _mix_call(alpha, layer, x2d, x_token_major, n_seq, n_tok, w):
    seq = n_tok // n_seq
    nt = seq // TT

    def layer_const(shape):
        zeros = (0,) * len(shape)
        return pl.BlockSpec((None,) + shape, lambda b, t: (layer,) + zeros, pipeline_mode=pl.Buffered(1))

    def const(shape):
        zeros = (0,) * len(shape)
        return pl.BlockSpec(shape, lambda b, t: zeros, pipeline_mode=pl.Buffered(1))

    in_specs = [
        (pl.BlockSpec((TT * X_PITCH, LANES), lambda b, t: (b * nt + t, 0)) if x_token_major
         else pl.BlockSpec((TT, D_MODEL), lambda b, t: (b * nt + t, 0))),
        layer_const((D_MODEL, D_IN)),
        layer_const((CONV_A_WIDTH, D_A)),
        layer_const((D_A, D_MODEL)),
        layer_const((CONV_R_WIDTH, D_R)),
        layer_const((1, D_R)),
        layer_const((N_RG_BLOCKS, RG_BLOCK, 2 * RG_BLOCK)),
        layer_const((1, D_R)),
        layer_const((1, D_R)),
        layer_const((1, D_R)),
        layer_const((D_R, D_MODEL)),
        layer_const((D_MODEL, D_MODEL)),
        layer_const((1, D_MODEL)),
        layer_const((1, D_MODEL)),
        const((N_EXPERTS, D_MODEL)),
        const((N_EXPERTS, D_MODEL)),
        const((N_EXPERTS, 1)),
    ]
    out_specs = [
        pl.BlockSpec((TT * XE_PITCH, LANES), lambda b, t: (b * nt + t, 0)),
        pl.BlockSpec((SUBLANES, TT), lambda b, t: (0, b * nt + t)),
    ]
    out_shape = [
        jax.ShapeDtypeStruct((n_tok * XE_PITCH, LANES), _F32),
        jax.ShapeDtypeStruct((SUBLANES, n_tok), _F32),
    ]
    scratch = [
        pltpu.VMEM((SUBLANES, D_A), _F32),
        pltpu.VMEM((SUBLANES, D_R), _F32),
        pltpu.VMEM((TT, D_A), _F32),
        pltpu.VMEM((TT, D_R), _F32),
        pltpu.VMEM((TT, D_R), _F32),
        pltpu.VMEM((TT, D_R), _F32),
        pltpu.VMEM((TT, D_R), _F32),
        pltpu.VMEM((1, D_R), _F32),
        pltpu.VMEM((TT, LATE_COLS), _F32),
        pltpu.VMEM((TT, D_A), _BF),
    ]
    return pl.pallas_call(
        functools.partial(_mix_kernel, alpha, x_token_major),
        grid=(n_seq, nt),
        in_specs=in_specs,
        out_specs=out_specs,
        out_shape=out_shape,
        scratch_shapes=scratch,
        compiler_params=pltpu.CompilerParams(
            dimension_semantics=("arbitrary", "arbitrary"), vmem_limit_bytes=VMEM_LIMIT),
        name="mix",
    )(x2d, *w)


def _moe_kernel(alpha, n_tiles, out_token_major,
                elo_ref, ehi_ref, nv_ref, off_ref, order_ref,
                xe_hbm,
                wg_lo, wu_lo, wd_lo, wg_hi, wu_hi, wd_hi, g2_ref, b2_ref,
                out_hbm,
                gbuf0, gbuf1, obuf0, obuf1, gsem, ssem, pend):
    gbuf = (gbuf0, gbuf1)
    obuf = (obuf0, obuf1)
    i = pl.program_id(0)
    nv = nv_ref[i]
    o_pitch = X_PITCH if out_token_major else 1

    def gather_copy(tok, r, s):
        return pltpu.make_async_copy(xe_hbm.at[pl.ds(tok * XE_PITCH, XE_PITCH), :],
                                     gbuf[s].at[pl.ds(r * XE_PITCH, XE_PITCH), :], gsem.at[s])

    def scatter_copy(tok, r, s, n_rows=1):
        src = r * o_pitch if isinstance(r, int) else pl.multiple_of(r * o_pitch, o_pitch)
        return pltpu.make_async_copy(obuf[s].at[pl.ds(src, n_rows * o_pitch), :],
                                     out_hbm.at[pl.ds(tok * o_pitch, n_rows * o_pitch), :], ssem.at[s])

    def start_gather(tile, s):
        base = off_ref[tile]
        for r in range(TM):
            gather_copy(order_ref[base + r], r, s).start(priority=r % 2)

    def wait_gather(s):
        pltpu.make_async_copy(xe_hbm.at[pl.ds(0, TM * XE_PITCH), :], gbuf[s], gsem.at[s]).wait()

    def start_scatter(tile, s, n_rows):
        base = off_ref[tile]
        n_chunks = lax.shift_right_logical(n_rows, DMA_UNROLL.bit_length() - 1)

        def chunk(c, carry):
            r0 = c * DMA_UNROLL
            for j in range(DMA_UNROLL):
                scatter_copy(order_ref[base + r0 + j], r0 + j, s).start(priority=j % 2)
            return carry
        lax.fori_loop(0, n_chunks, chunk, 0)

        def tail(r, carry):
            scatter_copy(order_ref[base + r], r, s).start()
            return carry
        lax.fori_loop(n_chunks * DMA_UNROLL, n_rows, tail, 0)

    def wait_scatter(s):
        n_rows = pend[s]
        width = TM
        while width >= 1:
            @pl.when(jnp.bitwise_and(n_rows, width) != 0)
            def _(width=width):
                scatter_copy(0, 0, s, n_rows=width).wait()
            width //= 2
        pend[s] = 0

    @pl.when(i == 0)
    def _():
        pend[0] = 0
        pend[1] = 0
        start_gather(0, 0)

    nxt = jnp.minimum(i + 1, n_tiles - 1)
    next_valid = jnp.logical_and(i + 1 < n_tiles, nv_ref[nxt] > 0)

    def tile_body(s):
        wait_gather(s)
        wait_scatter(s)

        @pl.when(next_valid)
        def _():
            start_gather(nxt, 1 - s)

        xf = _load_token_major(gbuf[s], TM, XE_PITCH)
        ext = gbuf[s][pl.ds(D_MODEL // LANES, TM, stride=XE_PITCH), :]
        gate_lo = ext[:, 1:2]
        gate_hi = ext[:, 2:3]
        xb = xf.astype(_BF)

        def expert(wg, wu, wd, gate):
            a2 = jnp.dot(xb, wg[...], preferred_element_type=_F32)
            u = jnp.dot(xb, wu[...], preferred_element_type=_F32)
            hgt = ((a2 * jnp.tanh(a2) + a2) * (u * gate)).astype(_BF)
            return jnp.dot(hgt, wd[...], preferred_element_type=_F32)

        moe = expert(wg_lo, wu_lo, wd_lo, gate_lo) + expert(wg_hi, wu_hi, wd_hi, gate_hi)
        y = _layer_norm(alpha * xf + moe, g2_ref[...], b2_ref[...])
        if out_token_major:
            _store_token_major(obuf[s], y, X_PITCH)
        else:
            obuf[s][...] = y
        start_scatter(i, s, nv)
        pend[s] = nv

    for s in range(2):
        @pl.when(jnp.logical_and(nv > 0, i % 2 == s))
        def _(s=s):
            tile_body(s)

    @pl.when(i == n_tiles - 1)
    def _():
        wait_scatter(0)
        wait_scatter(1)


def _moe_call(alpha, layer, xe, plan, w, n_tok, out_token_major):
    elo, ehi, nvalid, off, order = plan
    n_tiles = elo.shape[0]
    wg, wu, wd, g2, b2 = w
    if out_token_major:
        obuf_shape, out_rows, out_cols = (TM * X_PITCH, LANES), n_tok * X_PITCH, LANES
    else:
        obuf_shape, out_rows, out_cols = (TM, D_MODEL), n_tok, D_MODEL

    def wspec(shape, which):
        def imap(i, elo_r, ehi_r, *_):
            return (layer, (elo_r, ehi_r)[which][i], 0, 0)
        return pl.BlockSpec((None, None) + shape, imap)

    grid_spec = pltpu.PrefetchScalarGridSpec(
        num_scalar_prefetch=5,
        grid=(n_tiles,),
        in_specs=[
            pl.BlockSpec(memory_space=pl.ANY),
            wspec((D_MODEL, D_FF), 0), wspec((D_MODEL, D_FF), 0), wspec((D_FF, D_MODEL), 0),
            wspec((D_MODEL, D_FF), 1), wspec((D_MODEL, D_FF), 1), wspec((D_FF, D_MODEL), 1),
            pl.BlockSpec((None, 1, D_MODEL), lambda i, *_: (layer, 0, 0)),
            pl.BlockSpec((None, 1, D_MODEL), lambda i, *_: (layer, 0, 0)),
        ],
        out_specs=pl.BlockSpec(memory_space=pl.ANY),
        scratch_shapes=[
            pltpu.VMEM((TM * XE_PITCH, LANES), _F32),
            pltpu.VMEM((TM * XE_PITCH, LANES), _F32),
            pltpu.VMEM(obuf_shape, _F32),
            pltpu.VMEM(obuf_shape, _F32),
            pltpu.SemaphoreType.DMA((2,)),
            pltpu.SemaphoreType.DMA((2,)),
            pltpu.SMEM((2,), jnp.int32),
        ],
    )
    return pl.pallas_call(
        functools.partial(_moe_kernel, alpha, n_tiles, out_token_major),
        grid_spec=grid_spec,
        out_shape=jax.ShapeDtypeStruct((out_rows, out_cols), _F32),
        compiler_params=pltpu.CompilerParams(
            dimension_semantics=("arbitrary",), vmem_limit_bytes=VMEM_LIMIT),
        name="moe",
    )(elo, ehi, nvalid, off, order, xe, wg, wu, wd, wg, wu, wd, g2, b2)


_PAIR_LO = (0, 0, 0, 1, 1, 2)
_PAIR_HI = (1, 2, 3, 2, 3, 3)


def _plan(cls, n_tiles):
    n_tok = cls.shape[0]
    tok = jnp.arange(n_tok, dtype=jnp.int32)
    order = jnp.sort(cls * n_tok + tok) % n_tok
    order = jnp.concatenate([order, jnp.zeros((TM,), jnp.int32)])
    counts = jnp.sum((cls[:, None] == jnp.arange(N_CLASSES, dtype=jnp.int32)[None, :]).astype(jnp.int32), axis=0)
    cstart = jnp.cumsum(counts) - counts
    tiles = (counts + TM - 1) // TM
    tile_end = jnp.cumsum(tiles)
    tile_start = tile_end - tiles
    ti = jnp.arange(n_tiles, dtype=jnp.int32)
    n_used = tile_end[-1]
    tsel = jnp.minimum(ti, n_used - 1)
    tcls = jnp.sum((tile_end[None, :] <= tsel[:, None]).astype(jnp.int32), axis=1)
    within = (tsel - tile_start[tcls]) * TM
    nvalid = jnp.where(ti < n_used, jnp.clip(counts[tcls] - within, 0, TM), 0).astype(jnp.int32)
    off = (cstart[tcls] + within).astype(jnp.int32)
    grp = tcls // N_PAIRS
    pair = tcls % N_PAIRS
    elo = grp * EXPERTS_PER_GROUP + jnp.asarray(_PAIR_LO, jnp.int32)[pair]
    ehi = grp * EXPERTS_PER_GROUP + jnp.asarray(_PAIR_HI, jnp.int32)[pair]
    return elo.astype(jnp.int32), ehi.astype(jnp.int32), nvalid, off, order


def kernel(x, w_in, conv_a_w, w_branch_a, conv_r_w, conv_r_b, w_rg_a, b_rg_a, w_rg_x, b_rg_x, rg_lambda, w_branch_r, w_out, ln1_g, ln1_b, w_router, b_router, w_exp_gate, w_exp_up, w_exp_down, ln2_g, ln2_b):
    n_seq, seq, _ = x.shape
    depth = w_in.shape[0]
    n_tok = n_seq * seq
    assert seq % TT == 0 and n_tok % TM == 0
    n_tiles = n_tok // TM + N_CLASSES
    alpha = (2.0 * depth) ** 0.25

    wrt = w_router.T.astype(_F32)
    wrt_hi = wrt.astype(_BF)
    wrt_lo = (wrt - wrt_hi.astype(_F32)).astype(_BF)
    brt = b_router.astype(_F32).reshape(N_EXPERTS, 1)

    in_scale = jnp.concatenate([jnp.ones((OFF_GA,), _F32), jnp.full((D_IN - OFF_GA,), 0.5, _F32)])
    mix_w = (
        (w_in * in_scale).astype(_BF),
        conv_a_w,
        w_branch_a.astype(_BF),
        conv_r_w,
        conv_r_b.reshape(depth, 1, D_R),
        (0.5 * jnp.concatenate([w_rg_a, w_rg_x], axis=-1)).astype(_BF),
        0.5 * b_rg_a.reshape(depth, 1, D_R),
        0.5 * b_rg_x.reshape(depth, 1, D_R),
        rg_lambda.reshape(depth, 1, D_R),
        (0.25 * w_branch_r).astype(_BF),
        (0.5 * w_out).astype(_BF),
        ln1_g.reshape(depth, 1, D_MODEL),
        ln1_b.reshape(depth, 1, D_MODEL),
        wrt_hi, wrt_lo, brt,
    )
    moe_w = (
        (0.5 * w_exp_gate).astype(_BF), w_exp_up.astype(_BF), w_exp_down.astype(_BF),
        ln2_g.reshape(depth, 1, D_MODEL), ln2_b.reshape(depth, 1, D_MODEL),
    )

    h = x.reshape(n_tok, D_MODEL)
    for l in range(depth):
        xe, route = _mix_call(alpha, l, h, l > 0, n_seq, n_tok, mix_w)
        plan = _plan(route[0].astype(jnp.int32), n_tiles)
        h = _moe_call(alpha, l, xe, plan, moe_w, n_tok, l < depth - 1)
    return h.reshape(n_seq, seq, D_MODEL)
```

```python
import functools

import jax
import jax.numpy as jnp
from jax import lax
from jax.experimental import pallas as pl
from jax.experimental.pallas import tpu as pltpu

D_MODEL = 1024
D_A = 1024
D_R = 1280
N_RG_BLOCKS = 10
RG_BLOCK = 128
RG_C = 8.0
CONV_A_WIDTH = 3
CONV_R_WIDTH = 4
D_IN = 3 * D_A + 2 * D_R + 2 * D_MODEL
N_EXPERTS = 16
N_GROUPS = 4
EXPERTS_PER_GROUP = 4
N_PAIRS = 6
N_CLASSES = N_GROUPS * N_PAIRS
D_FF = 512
LN_EPS = 1e-5

OFF_HA, OFF_BA, OFF_CA = 0, D_A, 2 * D_A
OFF_XR = 3 * D_A
OFF_YR = OFF_XR + D_R
OFF_GA = OFF_YR + D_R
OFF_GR = OFF_GA + D_MODEL
LATE_COLS = D_IN - OFF_YR
LATE_CHUNK = 256

SUBLANES = 8
LANES = 128
EXT = LANES
X_PITCH = D_MODEL // LANES
XE_PITCH = (D_MODEL + EXT) // LANES
TT = 512
TM = 256
DMA_UNROLL = 8
VMEM_LIMIT = 56 * 1024 * 1024

_BF = jnp.bfloat16
_F32 = jnp.float32

GELU_K1 = 0.7978845608028654
GELU_K2 = GELU_K1 * 0.044715


def _layer_norm(y, g, b):
    mu = jnp.mean(y, axis=-1, keepdims=True)
    yc = y - mu
    var = jnp.mean(yc * yc, axis=-1, keepdims=True)
    return yc * lax.rsqrt(var + LN_EPS) * g + b


def _first_index_of(vals, target):
    idx = jnp.full(target.shape, float(len(vals) - 1), _F32)
    for j in range(len(vals) - 2, -1, -1):
        idx = jnp.where(vals[j] == target, float(j), idx)
    return idx


def _route(logits_t):
    m = jnp.max(logits_t, axis=0, keepdims=True)
    e = jnp.exp(logits_t - m)
    p = e / jnp.sum(e, axis=0, keepdims=True)
    neg = jnp.full((1, logits_t.shape[1]), -jnp.inf, _F32)
    scores, m1s, i1s, m2s, i2s = [], [], [], [], []
    for g in range(N_GROUPS):
        v = [p[g * EXPERTS_PER_GROUP + j:g * EXPERTS_PER_GROUP + j + 1, :] for j in range(EXPERTS_PER_GROUP)]
        m1 = jnp.maximum(jnp.maximum(v[0], v[1]), jnp.maximum(v[2], v[3]))
        i1 = _first_index_of(v, m1)
        w = [jnp.where(i1 == float(j), neg, v[j]) for j in range(EXPERTS_PER_GROUP)]
        m2 = jnp.maximum(jnp.maximum(w[0], w[1]), jnp.maximum(w[2], w[3]))
        i2 = _first_index_of(w, m2)
        scores.append(m1 + m2)
        m1s.append(m1); i1s.append(i1); m2s.append(m2); i2s.append(i2)
    best = jnp.maximum(jnp.maximum(scores[0], scores[1]), jnp.maximum(scores[2], scores[3]))
    gsel = _first_index_of(scores, best)

    def pick(xs):
        out = xs[N_GROUPS - 1]
        for g in range(N_GROUPS - 2, -1, -1):
            out = jnp.where(gsel == float(g), xs[g], out)
        return out

    m1, i1, m2, i2 = pick(m1s), pick(i1s), pick(m2s), pick(i2s)
    den = m1 + m2
    gate1, gate2 = m1 / den, m2 / den
    first_is_lo = i1 < i2
    lo = jnp.where(first_is_lo, i1, i2)
    hi = jnp.where(first_is_lo, i2, i1)
    g_lo = jnp.where(first_is_lo, gate1, gate2)
    g_hi = jnp.where(first_is_lo, gate2, gate1)
    pair = lo * (7.0 - lo) * 0.5 + (hi - lo - 1.0)
    cls = gsel * float(N_PAIRS) + pair
    return cls, g_lo, g_hi


def _load_token_major(ref, n_rows, pitch, lead=()):
    cols = [ref[lead + (pl.ds(c, n_rows, stride=pitch), slice(None))] for c in range(D_MODEL // LANES)]
    return jnp.concatenate(cols, axis=1)


def _store_token_major(ref, val, pitch, lead=()):
    n_rows = val.shape[0]
    for c in range(val.shape[1] // LANES):
        ref[lead + (pl.ds(c, n_rows, stride=pitch), slice(None))] = val[:, c * LANES:(c + 1) * LANES]


def _causal_conv(u, hist_ref, w_ref, n_taps, out_ref, cols=slice(None), bias=None):
    tt = u.shape[0]

    def tap(d):
        return w_ref[n_taps - 1 - d:n_taps - d, cols]

    def tap0(val):
        out = val * tap(0)
        return out if bias is None else out + bias

    acc = tap0(u)
    for d in range(1, n_taps):
        acc = acc + pltpu.roll(u, d, 0) * tap(d)
    out_ref[:, cols] = acc
    head = u[0:SUBLANES, :]
    hist = hist_ref[:, cols]
    row = lax.broadcasted_iota(jnp.int32, head.shape, 0)
    acc_head = tap0(head)
    for d in range(1, n_taps):
        shifted = jnp.where(row < d, pltpu.roll(hist, d, 0), pltpu.roll(head, d, 0))
        acc_head = acc_head + shifted * tap(d)
    out_ref[pl.ds(0, SUBLANES), cols] = acc_head
    hist_ref[:, cols] = u[tt - SUBLANES:tt, :]


def _mix_kernel(alpha, x_token_major,
                x_ref, win_ref, caw_ref, wa_ref, crw_ref, crb_ref, wrg_ref, brga_ref, brgx_ref,
                lam_ref, wr_ref, wo_ref, g1_ref, b1_ref, wrt_hi_ref, wrt_lo_ref, brt_ref,
                xe_ref, route_ref,
                uhist, xrhist, cabuf, xcbuf, abuf, bbuf, hbuf, hcarry, latebuf, vabuf):
    t = pl.program_id(1)
    tt = route_ref.shape[1]

    @pl.when(t == 0)
    def _():
        uhist[...] = jnp.zeros_like(uhist)
        xrhist[...] = jnp.zeros_like(xrhist)
        hcarry[...] = jnp.zeros_like(hcarry)

    x = _load_token_major(x_ref, tt, X_PITCH) if x_token_major else x_ref[...]
    xb = x.astype(_BF)

    def proj(off, width):
        return jnp.dot(xb, win_ref[:, off:off + width], preferred_element_type=_F32)

    def mixer_a_chunk(c):
        cols = slice(c * LATE_CHUNK, (c + 1) * LATE_CHUNK)
        u = proj(OFF_CA + c * LATE_CHUNK, LATE_CHUNK) * proj(OFF_HA + c * LATE_CHUNK, LATE_CHUNK)
        _causal_conv(u, uhist, caw_ref, CONV_A_WIDTH, cabuf, cols=cols)
        vabuf[:, cols] = (proj(OFF_BA + c * LATE_CHUNK, LATE_CHUNK) * cabuf[:, cols]).astype(_BF)

    def late_chunk(j):
        latebuf[:, j * LATE_CHUNK:(j + 1) * LATE_CHUNK] = proj(OFF_YR + j * LATE_CHUNK, LATE_CHUNK)

    def xr_chunk(c):
        cols = slice(c * LATE_CHUNK, (c + 1) * LATE_CHUNK)
        _causal_conv(proj(OFF_XR + c * LATE_CHUNK, LATE_CHUNK), xrhist, crw_ref, CONV_R_WIDTH, xcbuf,
                     cols=cols, bias=crb_ref[:, cols])

    heads_per_chunk = LATE_CHUNK // RG_BLOCK
    z = -lam_ref[...]
    softplus_neg_lam = jnp.maximum(z, 0.0) + jnp.log1p(jnp.exp(-jnp.abs(z)))
    c_half = (-0.5 * RG_C) * softplus_neg_lam
    row = lax.broadcasted_iota(jnp.int32, (tt // SUBLANES, SUBLANES, RG_BLOCK), 1)
    side_work = [functools.partial(mixer_a_chunk, c) for c in range(D_A // LATE_CHUNK)]
    side_work += [functools.partial(late_chunk, j) for j in range(LATE_COLS // LATE_CHUNK)]
    side_cost = [3] * (D_A // LATE_CHUNK) + [1] * (LATE_COLS // LATE_CHUNK)
    per_head = sum(side_cost) / N_RG_BLOCKS
    done, spent = 0, 0.0
    for h in range(N_RG_BLOCKS):
        if h % heads_per_chunk == 0:
            xr_chunk(h // heads_per_chunk)
        while done < len(side_work) and (spent < (h + 1) * per_head or h == N_RG_BLOCKS - 1):
            side_work[done]()
            spent += side_cost[done]
            done += 1
        sl = slice(h * RG_BLOCK, (h + 1) * RG_BLOCK)
        xc = xcbuf[:, sl]
        gh = jnp.dot(xc.astype(_BF), wrg_ref[h], preferred_element_type=_F32)
        t_r = jnp.tanh(gh[:, :RG_BLOCK] + brga_ref[:, sl])
        t_i = jnp.tanh(gh[:, RG_BLOCK:] + brgx_ref[:, sl])
        log_a = c_half[:, sl] * t_r + c_half[:, sl]
        a_t = jnp.exp(log_a)
        b_t = jnp.exp(0.5 * jnp.log(1.0 - a_t * a_t)) * (t_i * xc + xc)
        a3 = a_t.reshape(tt // SUBLANES, SUBLANES, RG_BLOCK)
        b3 = b_t.reshape(tt // SUBLANES, SUBLANES, RG_BLOCK)
        for k in (1, 2, 4):
            keep = row >= k
            b3 = jnp.where(keep, a3 * pltpu.roll(b3, k, 1) + b3, b3)
            a3 = jnp.where(keep, a3 * pltpu.roll(a3, k, 1), a3)
        abuf[:, sl] = a3.reshape(tt, RG_BLOCK)
        bbuf[:, sl] = b3.reshape(tt, RG_BLOCK)

    def carry_groups(g, h_prev):
        o = pl.multiple_of(g * SUBLANES, SUBLANES)
        hg = abuf[pl.ds(o, SUBLANES), :] * h_prev + bbuf[pl.ds(o, SUBLANES), :]
        hbuf[pl.ds(o, SUBLANES), :] = hg
        return hg[SUBLANES - 1:SUBLANES, :]

    hcarry[...] = lax.fori_loop(0, tt // SUBLANES, carry_groups, hcarry[...], unroll=4)
    ya = jnp.dot(vabuf[...], wa_ref[...], preferred_element_type=_F32)
    v = latebuf[:, :D_R]
    th = jnp.tanh(v * (GELU_K2 * (v * v) + GELU_K1))
    vr = (hbuf[...] * (v * th + v)).astype(_BF)
    yr = jnp.dot(vr, wr_ref[...], preferred_element_type=_F32)

    t_a = jnp.tanh(latebuf[:, OFF_GA - OFF_YR:OFF_GR - OFF_YR])
    t_g = jnp.tanh(latebuf[:, OFF_GR - OFF_YR:])
    merged2 = (t_a * ya + ya) + (t_g * yr + yr)
    o = jnp.dot(merged2.astype(_BF), wo_ref[...], preferred_element_type=_F32)
    x1 = _layer_norm(alpha * x + o, g1_ref[...], b1_ref[...])

    x1_hi = x1.astype(_BF)
    x1_lo = (x1 - x1_hi.astype(_F32)).astype(_BF)
    nt_dims = (((1,), (1,)), ((), ()))
    logits_t = (lax.dot_general(wrt_hi_ref[...], x1_hi, nt_dims, preferred_element_type=_F32)
                + lax.dot_general(wrt_hi_ref[...], x1_lo, nt_dims, preferred_element_type=_F32)
                + lax.dot_general(wrt_lo_ref[...], x1_hi, nt_dims, preferred_element_type=_F32)
                + brt_ref[...])
    cls, g_lo, g_hi = _route(logits_t)
    rsel = lax.broadcasted_iota(jnp.int32, (SUBLANES, tt), 0)
    route_ref[...] = jnp.where(rsel == 0, cls, jnp.where(rsel == 1, g_lo, jnp.where(rsel == 2, g_hi, 0.0)))
    esel = lax.broadcasted_iota(jnp.int32, (EXT, tt), 0)
    ext_t = jnp.where(esel == 0, cls, jnp.where(esel == 1, g_lo, jnp.where(esel == 2, g_hi, 0.0)))
    _store_token_major(xe_ref, jnp.concatenate([x1, ext_t.T], axis=1), XE_PITCH)


def _mix_call(alpha, layer, x2d, x_token_major, n_seq, n_tok, w):
    seq = n_tok // n_seq
    nt = seq // TT

    def layer_const(shape):
        zeros = (0,) * len(shape)
        return pl.BlockSpec((None,) + shape, lambda b, t: (layer,) + zeros, pipeline_mode=pl.Buffered(1))

    def const(shape):
        zeros = (0,) * len(shape)
        return pl.BlockSpec(shape, lambda b, t: zeros, pipeline_mode=pl.Buffered(1))

    in_specs = [
        (pl.BlockSpec((TT * X_PITCH, LANES), lambda b, t: (b * nt + t, 0)) if x_token_major
         else pl.BlockSpec((TT, D_MODEL), lambda b, t: (b * nt + t, 0))),
        layer_const((D_MODEL, D_IN)),
        layer_const((CONV_A_WIDTH, D_A)),
        layer_const((D_A, D_MODEL)),
        layer_const((CONV_R_WIDTH, D_R)),
        layer_const((1, D_R)),
        layer_const((N_RG_BLOCKS, RG_BLOCK, 2 * RG_BLOCK)),
        layer_const((1, D_R)),
        layer_const((1, D_R)),
        layer_const((1, D_R)),
        layer_const((D_R, D_MODEL)),
        layer_const((D_MODEL, D_MODEL)),
        layer_const((1, D_MODEL)),
        layer_const((1, D_MODEL)),
        const((N_EXPERTS, D_MODEL)),
        const((N_EXPERTS, D_MODEL)),
        const((N_EXPERTS, 1)),
    ]
    out_specs = [
        pl.BlockSpec((TT * XE_PITCH, LANES), lambda b, t: (b * nt + t, 0)),
        pl.BlockSpec((SUBLANES, TT), lambda b, t: (0, b * nt + t)),
    ]
    out_shape = [
        jax.ShapeDtypeStruct((n_tok * XE_PITCH, LANES), _F32),
        jax.ShapeDtypeStruct((SUBLANES, n_tok), _F32),
    ]
    scratch = [
        pltpu.VMEM((SUBLANES, D_A), _F32),
        pltpu.VMEM((SUBLANES, D_R), _F32),
        pltpu.VMEM((TT, D_A), _F32),
        pltpu.VMEM((TT, D_R), _F32),
        pltpu.VMEM((TT, D_R), _F32),
        pltpu.VMEM((TT, D_R), _F32),
        pltpu.VMEM((TT, D_R), _F32),
        pltpu.VMEM((1, D_R), _F32),
        pltpu.VMEM((TT, LATE_COLS), _F32),
        pltpu.VMEM((TT, D_A), _BF),
    ]
    return pl.pallas_call(
        functools.partial(_mix_kernel, alpha, x_token_major),
        grid=(n_seq, nt),
        in_specs=in_specs,
        out_specs=out_specs,
        out_shape=out_shape,
        scratch_shapes=scratch,
        compiler_params=pltpu.CompilerParams(
            dimension_semantics=("arbitrary", "arbitrary"), vmem_limit_bytes=VMEM_LIMIT),
        name="mix",
    )(x2d, *w)


def _moe_kernel(alpha, n_tiles, out_token_major,
                elo_ref, ehi_ref, nv_ref, off_ref, order_ref,
                xe_hbm,
                wg_lo, wu_lo, wd_lo, wg_hi, wu_hi, wd_hi, g2_ref, b2_ref,
                out_hbm,
                gbuf0, gbuf1, obuf0, obuf1, gsem, ssem, pend):
    gbuf = (gbuf0, gbuf1)
    obuf = (obuf0, obuf1)
    i = pl.program_id(0)
    nv = nv_ref[i]
    o_pitch = X_PITCH if out_token_major else 1

    def gather_copy(tok, r, s):
        return pltpu.make_async_copy(xe_hbm.at[pl.ds(tok * XE_PITCH, XE_PITCH), :],
                                     gbuf[s].at[pl.ds(r * XE_PITCH, XE_PITCH), :], gsem.at[s])

    def scatter_copy(tok, r, s, n_rows=1, group=None):
        dst = out_hbm.at[pl.ds(tok * o_pitch, n_rows * o_pitch), :]
        if out_token_major:
            src = r * o_pitch if isinstance(r, int) else pl.multiple_of(r * o_pitch, o_pitch)
            return pltpu.make_async_copy(obuf[s].at[pl.ds(src, n_rows * o_pitch), :], dst, ssem.at[s])
        if group is None:
            group, r = lax.shift_right_logical(r, 3), jnp.bitwise_and(r, SUBLANES - 1)
        return pltpu.make_async_copy(obuf[s].at[group, pl.ds(r, n_rows), :], dst, ssem.at[s])

    def scatter_wait_copy(s, n_rows):
        if out_token_major:
            return scatter_copy(0, 0, s, n_rows=n_rows)
        if n_rows < SUBLANES:
            block = obuf[s].at[0, pl.ds(0, n_rows), :]
        else:
            block = obuf[s].at[pl.ds(0, n_rows // SUBLANES)]
        return pltpu.make_async_copy(block, block, ssem.at[s])

    def start_gather(tile, s):
        base = off_ref[tile]
        for r in range(TM):
            gather_copy(order_ref[base + r], r, s).start(priority=r % 2)

    def wait_gather(s):
        pltpu.make_async_copy(xe_hbm.at[pl.ds(0, TM * XE_PITCH), :], gbuf[s], gsem.at[s]).wait()

    def start_scatter(tile, s, n_rows):
        base = off_ref[tile]
        n_chunks = lax.shift_right_logical(n_rows, DMA_UNROLL.bit_length() - 1)

        def chunk(c, carry):
            r0 = c * DMA_UNROLL
            for j in range(DMA_UNROLL):
                if out_token_major:
                    copy = scatter_copy(order_ref[base + r0 + j], r0 + j, s)
                else:
                    copy = scatter_copy(order_ref[base + r0 + j], j, s, group=c)
                copy.start(priority=j % 2)
            return carry
        lax.fori_loop(0, n_chunks, chunk, 0)

        def tail(r, carry):
            scatter_copy(order_ref[base + r], r, s).start()
            return carry
        lax.fori_loop(n_chunks * DMA_UNROLL, n_rows, tail, 0)

    def wait_scatter(s):
        n_rows = pend[s]
        width = TM
        while width >= 1:
            @pl.when(jnp.bitwise_and(n_rows, width) != 0)
            def _(width=width):
                scatter_wait_copy(s, width).wait()
            width //= 2
        pend[s] = 0

    @pl.when(i == 0)
    def _():
        pend[0] = 0
        pend[1] = 0
        start_gather(0, 0)

    nxt = jnp.minimum(i + 1, n_tiles - 1)
    next_valid = jnp.logical_and(i + 1 < n_tiles, nv_ref[nxt] > 0)

    def tile_body(s):
        wait_gather(s)
        wait_scatter(s)

        @pl.when(next_valid)
        def _():
            start_gather(nxt, 1 - s)

        xf = _load_token_major(gbuf[s], TM, XE_PITCH)
        ext = gbuf[s][pl.ds(D_MODEL // LANES, TM, stride=XE_PITCH), :]
        gate_lo = ext[:, 1:2]
        gate_hi = ext[:, 2:3]
        xb = xf.astype(_BF)

        def expert(wg, wu, wd, gate):
            a2 = jnp.dot(xb, wg[...], preferred_element_type=_F32)
            u = jnp.dot(xb, wu[...], preferred_element_type=_F32)
            hgt = ((a2 * jnp.tanh(a2) + a2) * (u * gate)).astype(_BF)
            return jnp.dot(hgt, wd[...], preferred_element_type=_F32)

        moe = expert(wg_lo, wu_lo, wd_lo, gate_lo) + expert(wg_hi, wu_hi, wd_hi, gate_hi)
        y = _layer_norm(alpha * xf + moe, g2_ref[...], b2_ref[...])
        if out_token_major:
            _store_token_major(obuf[s], y, X_PITCH)
        else:
            obuf[s][...] = y.reshape(TM // SUBLANES, SUBLANES, D_MODEL)
        start_scatter(i, s, nv)
        pend[s] = nv

    for s in range(2):
        @pl.when(jnp.logical_and(nv > 0, i % 2 == s))
        def _(s=s):
            tile_body(s)

    @pl.when(i == n_tiles - 1)
    def _():
        wait_scatter(0)
        wait_scatter(1)


def _moe_call(alpha, layer, xe, plan, w, n_tok, out_token_major):
    elo, ehi, nvalid, off, order = plan
    n_tiles = elo.shape[0]
    wg, wu, wd, g2, b2 = w
    if out_token_major:
        obuf_shape, out_rows, out_cols = (TM * X_PITCH, LANES), n_tok * X_PITCH, LANES
    else:
        obuf_shape, out_rows, out_cols = (TM // SUBLANES, SUBLANES, D_MODEL), n_tok, D_MODEL

    def wspec(shape, which):
        def imap(i, elo_r, ehi_r, *_):
            return (layer, (elo_r, ehi_r)[which][i], 0, 0)
        return pl.BlockSpec((None, None) + shape, imap)

    grid_spec = pltpu.PrefetchScalarGridSpec(
        num_scalar_prefetch=5,
        grid=(n_tiles,),
        in_specs=[
            pl.BlockSpec(memory_space=pl.ANY),
            wspec((D_MODEL, D_FF), 0), wspec((D_MODEL, D_FF), 0), wspec((D_FF, D_MODEL), 0),
            wspec((D_MODEL, D_FF), 1), wspec((D_MODEL, D_FF), 1), wspec((D_FF, D_MODEL), 1),
            pl.BlockSpec((None, 1, D_MODEL), lambda i, *_: (layer, 0, 0)),
            pl.BlockSpec((None, 1, D_MODEL), lambda i, *_: (layer, 0, 0)),
        ],
        out_specs=pl.BlockSpec(memory_space=pl.ANY),
        scratch_shapes=[
            pltpu.VMEM((TM * XE_PITCH, LANES), _F32),
            pltpu.VMEM((TM * XE_PITCH, LANES), _F32),
            pltpu.VMEM(obuf_shape, _F32),
            pltpu.VMEM(obuf_shape, _F32),
            pltpu.SemaphoreType.DMA((2,)),
            pltpu.SemaphoreType.DMA((2,)),
            pltpu.SMEM((2,), jnp.int32),
        ],
    )
    return pl.pallas_call(
        functools.partial(_moe_kernel, alpha, n_tiles, out_token_major),
        grid_spec=grid_spec,
        out_shape=jax.ShapeDtypeStruct((out_rows, out_cols), _F32),
        compiler_params=pltpu.CompilerParams(
            dimension_semantics=("arbitrary",), vmem_limit_bytes=VMEM_LIMIT),
        name="moe",
    )(elo, ehi, nvalid, off, order, xe, wg, wu, wd, wg, wu, wd, g2, b2)


_PAIR_LO = (0, 0, 0, 1, 1, 2)
_PAIR_HI = (1, 2, 3, 2, 3, 3)


def _plan(cls, n_tiles):
    n_tok = cls.shape[0]
    tok = jnp.arange(n_tok, dtype=jnp.int32)
    order = jnp.sort(cls * n_tok + tok) % n_tok
    order = jnp.concatenate([order, jnp.zeros((TM,), jnp.int32)])
    counts = jnp.sum((cls[:, None] == jnp.arange(N_CLASSES, dtype=jnp.int32)[None, :]).astype(jnp.int32), axis=0)
    cstart = jnp.cumsum(counts) - counts
    tiles = (counts + TM - 1) // TM
    tile_end = jnp.cumsum(tiles)
    tile_start = tile_end - tiles
    ti = jnp.arange(n_tiles, dtype=jnp.int32)
    n_used = tile_end[-1]
    tsel = jnp.minimum(ti, n_used - 1)
    tcls = jnp.sum((tile_end[None, :] <= tsel[:, None]).astype(jnp.int32), axis=1)
    within = (tsel - tile_start[tcls]) * TM
    nvalid = jnp.where(ti < n_used, jnp.clip(counts[tcls] - within, 0, TM), 0).astype(jnp.int32)
    off = (cstart[tcls] + within).astype(jnp.int32)
    grp = tcls // N_PAIRS
    pair = tcls % N_PAIRS
    elo = grp * EXPERTS_PER_GROUP + jnp.asarray(_PAIR_LO, jnp.int32)[pair]
    ehi = grp * EXPERTS_PER_GROUP + jnp.asarray(_PAIR_HI, jnp.int32)[pair]
    return elo.astype(jnp.int32), ehi.astype(jnp.int32), nvalid, off, order


def kernel(x, w_in, conv_a_w, w_branch_a, conv_r_w, conv_r_b, w_rg_a, b_rg_a, w_rg_x, b_rg_x, rg_lambda, w_branch_r, w_out, ln1_g, ln1_b, w_router, b_router, w_exp_gate, w_exp_up, w_exp_down, ln2_g, ln2_b):
    n_seq, seq, _ = x.shape
    depth = w_in.shape[0]
    n_tok = n_seq * seq
    assert seq % TT == 0 and n_tok % TM == 0
    n_tiles = n_tok // TM + N_CLASSES
    alpha = (2.0 * depth) ** 0.25

    wrt = w_router.T.astype(_F32)
    wrt_hi = wrt.astype(_BF)
    wrt_lo = (wrt - wrt_hi.astype(_F32)).astype(_BF)
    brt = b_router.astype(_F32).reshape(N_EXPERTS, 1)

    in_scale = jnp.concatenate([jnp.ones((OFF_GA,), _F32), jnp.full((D_IN - OFF_GA,), 0.5, _F32)])
    mix_w = (
        (w_in * in_scale).astype(_BF),
        conv_a_w,
        w_branch_a.astype(_BF),
        conv_r_w,
        conv_r_b.reshape(depth, 1, D_R),
        (0.5 * jnp.concatenate([w_rg_a, w_rg_x], axis=-1)).astype(_BF),
        0.5 * b_rg_a.reshape(depth, 1, D_R),
        0.5 * b_rg_x.reshape(depth, 1, D_R),
        rg_lambda.reshape(depth, 1, D_R),
        (0.25 * w_branch_r).astype(_BF),
        (0.5 * w_out).astype(_BF),
        ln1_g.reshape(depth, 1, D_MODEL),
        ln1_b.reshape(depth, 1, D_MODEL),
        wrt_hi, wrt_lo, brt,
    )
    moe_w = (
        (0.5 * w_exp_gate).astype(_BF), w_exp_up.astype(_BF), w_exp_down.astype(_BF),
        ln2_g.reshape(depth, 1, D_MODEL), ln2_b.reshape(depth, 1, D_MODEL),
    )

    h = x.reshape(n_tok, D_MODEL)
    for l in range(depth):
        xe, route = _mix_call(alpha, l, h, l > 0, n_seq, n_tok, mix_w)
        plan = _plan(route[0].astype(jnp.int32), n_tiles)
        h = _moe_call(alpha, l, xe, plan, moe_w, n_tok, l < depth - 1)
    return h.reshape(n_seq, seq, D_MODEL)
```

```python
import functools

import jax
import jax.numpy as jnp
from jax import lax
from jax.experimental import pallas as pl
from jax.experimental.pallas import tpu as pltpu

D_MODEL = 1024
D_A = 1024
D_R = 1280
N_RG_BLOCKS = 10
RG_BLOCK = 128
RG_C = 8.0
CONV_A_WIDTH = 3
CONV_R_WIDTH = 4
D_IN = 3 * D_A + 2 * D_R + 2 * D_MODEL
N_EXPERTS = 16
N_GROUPS = 4
EXPERTS_PER_GROUP = 4
N_PAIRS = 6
N_CLASSES = N_GROUPS * N_PAIRS
D_FF = 512
LN_EPS = 1e-5

OFF_HA, OFF_BA, OFF_CA = 0, D_A, 2 * D_A
OFF_XR = 3 * D_A
OFF_YR = OFF_XR + D_R
OFF_GA = OFF_YR + D_R
OFF_GR = OFF_GA + D_MODEL
LATE_COLS = D_IN - OFF_YR
LATE_CHUNK = 256

SUBLANES = 8
LANES = 128
EXT = LANES
X_PITCH = D_MODEL // LANES
XE_PITCH = (D_MODEL + EXT) // LANES
TT = 512
TM = 256
DMA_UNROLL = 8
GATHER_AHEAD = 2
N_SLOTS = GATHER_AHEAD + 1
VMEM_LIMIT = 56 * 1024 * 1024

_BF = jnp.bfloat16
_F32 = jnp.float32

GELU_K1 = 0.7978845608028654
GELU_K2 = GELU_K1 * 0.044715


def _layer_norm(y, g, b):
    mu = jnp.mean(y, axis=-1, keepdims=True)
    yc = y - mu
    var = jnp.mean(yc * yc, axis=-1, keepdims=True)
    return yc * lax.rsqrt(var + LN_EPS) * g + b


def _first_index_of(vals, target):
    idx = jnp.full(target.shape, float(len(vals) - 1), _F32)
    for j in range(len(vals) - 2, -1, -1):
        idx = jnp.where(vals[j] == target, float(j), idx)
    return idx


def _route(logits_t):
    m = jnp.max(logits_t, axis=0, keepdims=True)
    e = jnp.exp(logits_t - m)
    p = e / jnp.sum(e, axis=0, keepdims=True)
    neg = jnp.full((1, logits_t.shape[1]), -jnp.inf, _F32)
    scores, m1s, i1s, m2s, i2s = [], [], [], [], []
    for g in range(N_GROUPS):
        v = [p[g * EXPERTS_PER_GROUP + j:g * EXPERTS_PER_GROUP + j + 1, :] for j in range(EXPERTS_PER_GROUP)]
        m1 = jnp.maximum(jnp.maximum(v[0], v[1]), jnp.maximum(v[2], v[3]))
        i1 = _first_index_of(v, m1)
        w = [jnp.where(i1 == float(j), neg, v[j]) for j in range(EXPERTS_PER_GROUP)]
        m2 = jnp.maximum(jnp.maximum(w[0], w[1]), jnp.maximum(w[2], w[3]))
        i2 = _first_index_of(w, m2)
        scores.append(m1 + m2)
        m1s.append(m1); i1s.append(i1); m2s.append(m2); i2s.append(i2)
    best = jnp.maximum(jnp.maximum(scores[0], scores[1]), jnp.maximum(scores[2], scores[3]))
    gsel = _first_index_of(scores, best)

    def pick(xs):
        out = xs[N_GROUPS - 1]
        for g in range(N_GROUPS - 2, -1, -1):
            out = jnp.where(gsel == float(g), xs[g], out)
        return out

    m1, i1, m2, i2 = pick(m1s), pick(i1s), pick(m2s), pick(i2s)
    den = m1 + m2
    gate1, gate2 = m1 / den, m2 / den
    first_is_lo = i1 < i2
    lo = jnp.where(first_is_lo, i1, i2)
    hi = jnp.where(first_is_lo, i2, i1)
    g_lo = jnp.where(first_is_lo, gate1, gate2)
    g_hi = jnp.where(first_is_lo, gate2, gate1)
    pair = lo * (7.0 - lo) * 0.5 + (hi - lo - 1.0)
    cls = gsel * float(N_PAIRS) + pair
    return cls, g_lo, g_hi


def _load_token_major(ref, n_rows, pitch, lead=()):
    cols = [ref[lead + (pl.ds(c, n_rows, stride=pitch), slice(None))] for c in range(D_MODEL // LANES)]
    return jnp.concatenate(cols, axis=1)


def _store_token_major(ref, val, pitch, lead=()):
    n_rows = val.shape[0]
    for c in range(val.shape[1] // LANES):
        ref[lead + (pl.ds(c, n_rows, stride=pitch), slice(None))] = val[:, c * LANES:(c + 1) * LANES]


def _causal_conv(u, hist_ref, w_ref, n_taps, out_ref, cols=slice(None), bias=None):
    tt = u.shape[0]

    def tap(d):
        return w_ref[n_taps - 1 - d:n_taps - d, cols]

    def tap0(val):
        out = val * tap(0)
        return out if bias is None else out + bias

    acc = tap0(u)
    for d in range(1, n_taps):
        acc = acc + pltpu.roll(u, d, 0) * tap(d)
    out_ref[:, cols] = acc
    head = u[0:SUBLANES, :]
    hist = hist_ref[:, cols]
    row = lax.broadcasted_iota(jnp.int32, head.shape, 0)
    acc_head = tap0(head)
    for d in range(1, n_taps):
        shifted = jnp.where(row < d, pltpu.roll(hist, d, 0), pltpu.roll(head, d, 0))
        acc_head = acc_head + shifted * tap(d)
    out_ref[pl.ds(0, SUBLANES), cols] = acc_head
    hist_ref[:, cols] = u[tt - SUBLANES:tt, :]


def _mix_kernel(alpha, x_token_major,
                x_ref, win_ref, caw_ref, wa_ref, crw_ref, crb_ref, wrg_ref, brga_ref, brgx_ref,
                lam_ref, wr_ref, wo_ref, g1_ref, b1_ref, wrt_hi_ref, wrt_lo_ref, brt_ref,
                xe_ref, route_ref,
                uhist, xrhist, cabuf, xcbuf, abuf, bbuf, hbuf, hcarry, latebuf, vabuf):
    t = pl.program_id(1)
    tt = route_ref.shape[1]

    @pl.when(t == 0)
    def _():
        uhist[...] = jnp.zeros_like(uhist)
        xrhist[...] = jnp.zeros_like(xrhist)
        hcarry[...] = jnp.zeros_like(hcarry)

    x = _load_token_major(x_ref, tt, X_PITCH) if x_token_major else x_ref[...]
    xb = x.astype(_BF)

    def proj(off, width):
        return jnp.dot(xb, win_ref[:, off:off + width], preferred_element_type=_F32)

    def mixer_a_chunk(c):
        cols = slice(c * LATE_CHUNK, (c + 1) * LATE_CHUNK)
        u = proj(OFF_CA + c * LATE_CHUNK, LATE_CHUNK) * proj(OFF_HA + c * LATE_CHUNK, LATE_CHUNK)
        _causal_conv(u, uhist, caw_ref, CONV_A_WIDTH, cabuf, cols=cols)
        vabuf[:, cols] = (proj(OFF_BA + c * LATE_CHUNK, LATE_CHUNK) * cabuf[:, cols]).astype(_BF)

    def late_chunk(j):
        latebuf[:, j * LATE_CHUNK:(j + 1) * LATE_CHUNK] = proj(OFF_YR + j * LATE_CHUNK, LATE_CHUNK)

    def xr_chunk(c):
        cols = slice(c * LATE_CHUNK, (c + 1) * LATE_CHUNK)
        _causal_conv(proj(OFF_XR + c * LATE_CHUNK, LATE_CHUNK), xrhist, crw_ref, CONV_R_WIDTH, xcbuf,
                     cols=cols, bias=crb_ref[:, cols])

    heads_per_chunk = LATE_CHUNK // RG_BLOCK
    z = -lam_ref[...]
    softplus_neg_lam = jnp.maximum(z, 0.0) + jnp.log1p(jnp.exp(-jnp.abs(z)))
    c_half = (-0.5 * RG_C) * softplus_neg_lam
    row = lax.broadcasted_iota(jnp.int32, (tt // SUBLANES, SUBLANES, RG_BLOCK), 1)
    side_work = [functools.partial(mixer_a_chunk, c) for c in range(D_A // LATE_CHUNK)]
    side_work += [functools.partial(late_chunk, j) for j in range(LATE_COLS // LATE_CHUNK)]
    side_cost = [3] * (D_A // LATE_CHUNK) + [1] * (LATE_COLS // LATE_CHUNK)
    per_head = sum(side_cost) / N_RG_BLOCKS
    done, spent = 0, 0.0
    for h in range(N_RG_BLOCKS):
        if h % heads_per_chunk == 0:
            xr_chunk(h // heads_per_chunk)
        while done < len(side_work) and (spent < (h + 1) * per_head or h == N_RG_BLOCKS - 1):
            side_work[done]()
            spent += side_cost[done]
            done += 1
        sl = slice(h * RG_BLOCK, (h + 1) * RG_BLOCK)
        xc = xcbuf[:, sl]
        gh = jnp.dot(xc.astype(_BF), wrg_ref[h], preferred_element_type=_F32)
        t_r = jnp.tanh(gh[:, :RG_BLOCK] + brga_ref[:, sl])
        t_i = jnp.tanh(gh[:, RG_BLOCK:] + brgx_ref[:, sl])
        log_a = c_half[:, sl] * t_r + c_half[:, sl]
        a_t = jnp.exp(log_a)
        b_t = jnp.exp(0.5 * jnp.log(1.0 - a_t * a_t)) * (t_i * xc + xc)
        a3 = a_t.reshape(tt // SUBLANES, SUBLANES, RG_BLOCK)
        b3 = b_t.reshape(tt // SUBLANES, SUBLANES, RG_BLOCK)
        for k in (1, 2, 4):
            keep = row >= k
            b3 = jnp.where(keep, a3 * pltpu.roll(b3, k, 1) + b3, b3)
            a3 = jnp.where(keep, a3 * pltpu.roll(a3, k, 1), a3)
        abuf[:, sl] = a3.reshape(tt, RG_BLOCK)
        bbuf[:, sl] = b3.reshape(tt, RG_BLOCK)

    def carry_groups(g, h_prev):
        o = pl.multiple_of(g * SUBLANES, SUBLANES)
        hg = abuf[pl.ds(o, SUBLANES), :] * h_prev + bbuf[pl.ds(o, SUBLANES), :]
        hbuf[pl.ds(o, SUBLANES), :] = hg
        return hg[SUBLANES - 1:SUBLANES, :]

    hcarry[...] = lax.fori_loop(0, tt // SUBLANES, carry_groups, hcarry[...], unroll=4)
    ya = jnp.dot(vabuf[...], wa_ref[...], preferred_element_type=_F32)
    v = latebuf[:, :D_R]
    th = jnp.tanh(v * (GELU_K2 * (v * v) + GELU_K1))
    vr = (hbuf[...] * (v * th + v)).astype(_BF)
    yr = jnp.dot(vr, wr_ref[...], preferred_element_type=_F32)

    t_a = jnp.tanh(latebuf[:, OFF_GA - OFF_YR:OFF_GR - OFF_YR])
    t_g = jnp.tanh(latebuf[:, OFF_GR - OFF_YR:])
    merged2 = (t_a * ya + ya) + (t_g * yr + yr)
    o = jnp.dot(merged2.astype(_BF), wo_ref[...], preferred_element_type=_F32)
    x1 = _layer_norm(alpha * x + o, g1_ref[...], b1_ref[...])

    x1_hi = x1.astype(_BF)
    x1_lo = (x1 - x1_hi.astype(_F32)).astype(_BF)
    nt_dims = (((1,), (1,)), ((), ()))
    logits_t = (lax.dot_general(wrt_hi_ref[...], x1_hi, nt_dims, preferred_element_type=_F32)
                + lax.dot_general(wrt_hi_ref[...], x1_lo, nt_dims, preferred_element_type=_F32)
                + lax.dot_general(wrt_lo_ref[...], x1_hi, nt_dims, preferred_element_type=_F32)
                + brt_ref[...])
    cls, g_lo, g_hi = _route(logits_t)
    rsel = lax.broadcasted_iota(jnp.int32, (SUBLANES, tt), 0)
    route_ref[...] = jnp.where(rsel == 0, cls, jnp.where(rsel == 1, g_lo, jnp.where(rsel == 2, g_hi, 0.0)))
    esel = lax.broadcasted_iota(jnp.int32, (EXT, tt), 0)
    ext_t = jnp.where(esel == 0, cls, jnp.where(esel == 1, g_lo, jnp.where(esel == 2, g_hi, 0.0)))
    _store_token_major(xe_ref, jnp.concatenate([x1, ext_t.T], axis=1), XE_PITCH)


def _mix_call(alpha, layer, x2d, x_token_major, n_seq, n_tok, w):
    seq = n_tok // n_seq
    nt = seq // TT

    def layer_const(shape):
        zeros = (0,) * len(shape)
        return pl.BlockSpec((None,) + shape, lambda b, t: (layer,) + zeros, pipeline_mode=pl.Buffered(1))

    def const(shape):
        zeros = (0,) * len(shape)
        return pl.BlockSpec(shape, lambda b, t: zeros, pipeline_mode=pl.Buffered(1))

    in_specs = [
        (pl.BlockSpec((TT * X_PITCH, LANES), lambda b, t: (b * nt + t, 0)) if x_token_major
         else pl.BlockSpec((TT, D_MODEL), lambda b, t: (b * nt + t, 0))),
        layer_const((D_MODEL, D_IN)),
        layer_const((CONV_A_WIDTH, D_A)),
        layer_const((D_A, D_MODEL)),
        layer_const((CONV_R_WIDTH, D_R)),
        layer_const((1, D_R)),
        layer_const((N_RG_BLOCKS, RG_BLOCK, 2 * RG_BLOCK)),
        layer_const((1, D_R)),
        layer_const((1, D_R)),
        layer_const((1, D_R)),
        layer_const((D_R, D_MODEL)),
        layer_const((D_MODEL, D_MODEL)),
        layer_const((1, D_MODEL)),
        layer_const((1, D_MODEL)),
        const((N_EXPERTS, D_MODEL)),
        const((N_EXPERTS, D_MODEL)),
        const((N_EXPERTS, 1)),
    ]
    out_specs = [
        pl.BlockSpec((TT * XE_PITCH, LANES), lambda b, t: (b * nt + t, 0)),
        pl.BlockSpec((SUBLANES, TT), lambda b, t: (0, b * nt + t)),
    ]
    out_shape = [
        jax.ShapeDtypeStruct((n_tok * XE_PITCH, LANES), _F32),
        jax.ShapeDtypeStruct((SUBLANES, n_tok), _F32),
    ]
    scratch = [
        pltpu.VMEM((SUBLANES, D_A), _F32),
        pltpu.VMEM((SUBLANES, D_R), _F32),
        pltpu.VMEM((TT, D_A), _F32),
        pltpu.VMEM((TT, D_R), _F32),
        pltpu.VMEM((TT, D_R), _F32),
        pltpu.VMEM((TT, D_R), _F32),
        pltpu.VMEM((TT, D_R), _F32),
        pltpu.VMEM((1, D_R), _F32),
        pltpu.VMEM((TT, LATE_COLS), _F32),
        pltpu.VMEM((TT, D_A), _BF),
    ]
    return pl.pallas_call(
        functools.partial(_mix_kernel, alpha, x_token_major),
        grid=(n_seq, nt),
        in_specs=in_specs,
        out_specs=out_specs,
        out_shape=out_shape,
        scratch_shapes=scratch,
        compiler_params=pltpu.CompilerParams(
            dimension_semantics=("arbitrary", "arbitrary"), vmem_limit_bytes=VMEM_LIMIT),
        name="mix",
    )(x2d, *w)


def _moe_kernel(alpha, n_tiles, out_token_major,
                elo_ref, ehi_ref, nv_ref, off_ref, order_ref,
                xe_hbm,
                wg_lo, wu_lo, wd_lo, wg_hi, wu_hi, wd_hi, g2_ref, b2_ref,
                out_hbm,
                gbuf0, gbuf1, gbuf2, obuf0, obuf1, obuf2, gsem, ssem, pend):
    gbuf = (gbuf0, gbuf1, gbuf2)
    obuf = (obuf0, obuf1, obuf2)
    i = pl.program_id(0)
    nv = nv_ref[i]
    o_pitch = X_PITCH if out_token_major else 1

    def gather_copy(tok, r, s):
        return pltpu.make_async_copy(xe_hbm.at[pl.ds(tok * XE_PITCH, XE_PITCH), :],
                                     gbuf[s].at[pl.ds(r * XE_PITCH, XE_PITCH), :], gsem.at[s])

    def scatter_copy(tok, r, s, n_rows=1, group=None):
        dst = out_hbm.at[pl.ds(tok * o_pitch, n_rows * o_pitch), :]
        if out_token_major:
            src = r * o_pitch if isinstance(r, int) else pl.multiple_of(r * o_pitch, o_pitch)
            return pltpu.make_async_copy(obuf[s].at[pl.ds(src, n_rows * o_pitch), :], dst, ssem.at[s])
        if group is None:
            group, r = lax.shift_right_logical(r, 3), jnp.bitwise_and(r, SUBLANES - 1)
        return pltpu.make_async_copy(obuf[s].at[group, pl.ds(r, n_rows), :], dst, ssem.at[s])

    def scatter_wait_copy(s, n_rows):
        if out_token_major:
            return scatter_copy(0, 0, s, n_rows=n_rows)
        if n_rows < SUBLANES:
            block = obuf[s].at[0, pl.ds(0, n_rows), :]
        else:
            block = obuf[s].at[pl.ds(0, n_rows // SUBLANES)]
        return pltpu.make_async_copy(block, block, ssem.at[s])

    def start_gather(tile, s):
        base = off_ref[tile]
        for r in range(TM):
            gather_copy(order_ref[base + r], r, s).start(priority=r % 2)

    def wait_gather(s):
        pltpu.make_async_copy(xe_hbm.at[pl.ds(0, TM * XE_PITCH), :], gbuf[s], gsem.at[s]).wait()

    def start_scatter(tile, s, n_rows):
        base = off_ref[tile]
        n_chunks = lax.shift_right_logical(n_rows, DMA_UNROLL.bit_length() - 1)

        def chunk(c, carry):
            r0 = c * DMA_UNROLL
            for j in range(DMA_UNROLL):
                if out_token_major:
                    copy = scatter_copy(order_ref[base + r0 + j], r0 + j, s)
                else:
                    copy = scatter_copy(order_ref[base + r0 + j], j, s, group=c)
                copy.start(priority=j % 2)
            return carry
        lax.fori_loop(0, n_chunks, chunk, 0)

        def tail(r, carry):
            scatter_copy(order_ref[base + r], r, s).start()
            return carry
        lax.fori_loop(n_chunks * DMA_UNROLL, n_rows, tail, 0)

    def wait_scatter(s):
        n_rows = pend[s]
        width = TM
        while width >= 1:
            @pl.when(jnp.bitwise_and(n_rows, width) != 0)
            def _(width=width):
                scatter_wait_copy(s, width).wait()
            width //= 2
        pend[s] = 0

    @pl.when(i == 0)
    def _():
        for s in range(N_SLOTS):
            pend[s] = 0
        for ahead in range(GATHER_AHEAD):
            start_gather(ahead, ahead)

    def tile_body(s):
        wait_gather(s)
        wait_scatter(s)
        xf = _load_token_major(gbuf[s], TM, XE_PITCH)
        ext = gbuf[s][pl.ds(D_MODEL // LANES, TM, stride=XE_PITCH), :]
        gate_lo = ext[:, 1:2]
        gate_hi = ext[:, 2:3]
        xb = xf.astype(_BF)
        start_gather(i + GATHER_AHEAD, (s + GATHER_AHEAD) % N_SLOTS)

        def expert(wg, wu, wd, gate):
            a2 = jnp.dot(xb, wg[...], preferred_element_type=_F32)
            u = jnp.dot(xb, wu[...], preferred_element_type=_F32)
            hgt = ((a2 * jnp.tanh(a2) + a2) * (u * gate)).astype(_BF)
            return jnp.dot(hgt, wd[...], preferred_element_type=_F32)

        moe = expert(wg_lo, wu_lo, wd_lo, gate_lo) + expert(wg_hi, wu_hi, wd_hi, gate_hi)
        y = _layer_norm(alpha * xf + moe, g2_ref[...], b2_ref[...])
        if out_token_major:
            _store_token_major(obuf[s], y, X_PITCH)
        else:
            obuf[s][...] = y.reshape(TM // SUBLANES, SUBLANES, D_MODEL)
        start_scatter(i, s, nv)
        pend[s] = nv

    slot = lax.rem(i, N_SLOTS)
    issuer = jnp.maximum(i - GATHER_AHEAD, 0)
    gathered_unused = jnp.logical_and(nv == 0, jnp.logical_or(i < GATHER_AHEAD, nv_ref[issuer] > 0))
    for s in range(N_SLOTS):
        @pl.when(jnp.logical_and(nv > 0, slot == s))
        def _(s=s):
            tile_body(s)

        @pl.when(jnp.logical_and(gathered_unused, slot == s))
        def _(s=s):
            wait_gather(s)

    @pl.when(i == n_tiles - 1)
    def _():
        for s in range(N_SLOTS):
            wait_scatter(s)


def _moe_call(alpha, layer, xe, plan, w, n_tok, out_token_major):
    elo, ehi, nvalid, off, order = plan
    n_tiles = elo.shape[0] - GATHER_AHEAD
    wg, wu, wd, g2, b2 = w
    if out_token_major:
        obuf_shape, out_rows, out_cols = (TM * X_PITCH, LANES), n_tok * X_PITCH, LANES
    else:
        obuf_shape, out_rows, out_cols = (TM // SUBLANES, SUBLANES, D_MODEL), n_tok, D_MODEL

    def wspec(shape, which):
        def imap(i, elo_r, ehi_r, *_):
            return (layer, (elo_r, ehi_r)[which][i], 0, 0)
        return pl.BlockSpec((None, None) + shape, imap)

    grid_spec = pltpu.PrefetchScalarGridSpec(
        num_scalar_prefetch=5,
        grid=(n_tiles,),
        in_specs=[
            pl.BlockSpec(memory_space=pl.ANY),
            wspec((D_MODEL, D_FF), 0), wspec((D_MODEL, D_FF), 0), wspec((D_FF, D_MODEL), 0),
            wspec((D_MODEL, D_FF), 1), wspec((D_MODEL, D_FF), 1), wspec((D_FF, D_MODEL), 1),
            pl.BlockSpec((None, 1, D_MODEL), lambda i, *_: (layer, 0, 0)),
            pl.BlockSpec((None, 1, D_MODEL), lambda i, *_: (layer, 0, 0)),
        ],
        out_specs=pl.BlockSpec(memory_space=pl.ANY),
        scratch_shapes=[
            *[pltpu.VMEM((TM * XE_PITCH, LANES), _F32) for _ in range(N_SLOTS)],
            *[pltpu.VMEM(obuf_shape, _F32) for _ in range(N_SLOTS)],
            pltpu.SemaphoreType.DMA((N_SLOTS,)),
            pltpu.SemaphoreType.DMA((N_SLOTS,)),
            pltpu.SMEM((N_SLOTS,), jnp.int32),
        ],
    )
    return pl.pallas_call(
        functools.partial(_moe_kernel, alpha, n_tiles, out_token_major),
        grid_spec=grid_spec,
        out_shape=jax.ShapeDtypeStruct((out_rows, out_cols), _F32),
        compiler_params=pltpu.CompilerParams(
            dimension_semantics=("arbitrary",), vmem_limit_bytes=VMEM_LIMIT),
        name="moe",
    )(elo, ehi, nvalid, off, order, xe, wg, wu, wd, wg, wu, wd, g2, b2)


_PAIR_LO = (0, 0, 0, 1, 1, 2)
_PAIR_HI = (1, 2, 3, 2, 3, 3)


def _plan(cls, n_tiles):
    n_tok = cls.shape[0]
    tok = jnp.arange(n_tok, dtype=jnp.int32)
    order = jnp.sort(cls * n_tok + tok) % n_tok
    order = jnp.concatenate([order, jnp.zeros((TM,), jnp.int32)])
    counts = jnp.sum((cls[:, None] == jnp.arange(N_CLASSES, dtype=jnp.int32)[None, :]).astype(jnp.int32), axis=0)
    cstart = jnp.cumsum(counts) - counts
    tiles = (counts + TM - 1) // TM
    tile_end = jnp.cumsum(tiles)
    tile_start = tile_end - tiles
    ti = jnp.arange(n_tiles, dtype=jnp.int32)
    n_used = tile_end[-1]
    tsel = jnp.minimum(ti, n_used - 1)
    tcls = jnp.sum((tile_end[None, :] <= tsel[:, None]).astype(jnp.int32), axis=1)
    within = (tsel - tile_start[tcls]) * TM
    nvalid = jnp.where(ti < n_used, jnp.clip(counts[tcls] - within, 0, TM), 0).astype(jnp.int32)
    off = (cstart[tcls] + within).astype(jnp.int32)
    grp = tcls // N_PAIRS
    pair = tcls % N_PAIRS
    elo = grp * EXPERTS_PER_GROUP + jnp.asarray(_PAIR_LO, jnp.int32)[pair]
    ehi = grp * EXPERTS_PER_GROUP + jnp.asarray(_PAIR_HI, jnp.int32)[pair]
    return elo.astype(jnp.int32), ehi.astype(jnp.int32), nvalid, off, order


def kernel(x, w_in, conv_a_w, w_branch_a, conv_r_w, conv_r_b, w_rg_a, b_rg_a, w_rg_x, b_rg_x, rg_lambda, w_branch_r, w_out, ln1_g, ln1_b, w_router, b_router, w_exp_gate, w_exp_up, w_exp_down, ln2_g, ln2_b):
    n_seq, seq, _ = x.shape
    depth = w_in.shape[0]
    n_tok = n_seq * seq
    assert seq % TT == 0 and n_tok % TM == 0
    n_tiles = (n_tok + N_CLASSES * (TM - 1)) // TM + GATHER_AHEAD
    alpha = (2.0 * depth) ** 0.25

    wrt = w_router.T.astype(_F32)
    wrt_hi = wrt.astype(_BF)
    wrt_lo = (wrt - wrt_hi.astype(_F32)).astype(_BF)
    brt = b_router.astype(_F32).reshape(N_EXPERTS, 1)

    in_scale = jnp.concatenate([jnp.ones((OFF_GA,), _F32), jnp.full((D_IN - OFF_GA,), 0.5, _F32)])
    mix_w = (
        (w_in * in_scale).astype(_BF),
        conv_a_w,
        w_branch_a.astype(_BF),
        conv_r_w,
        conv_r_b.reshape(depth, 1, D_R),
        (0.5 * jnp.concatenate([w_rg_a, w_rg_x], axis=-1)).astype(_BF),
        0.5 * b_rg_a.reshape(depth, 1, D_R),
        0.5 * b_rg_x.reshape(depth, 1, D_R),
        rg_lambda.reshape(depth, 1, D_R),
        (0.25 * w_branch_r).astype(_BF),
        (0.5 * w_out).astype(_BF),
        ln1_g.reshape(depth, 1, D_MODEL),
        ln1_b.reshape(depth, 1, D_MODEL),
        wrt_hi, wrt_lo, brt,
    )
    moe_w = (
        (0.5 * w_exp_gate).astype(_BF), w_exp_up.astype(_BF), w_exp_down.astype(_BF),
        ln2_g.reshape(depth, 1, D_MODEL), ln2_b.reshape(depth, 1, D_MODEL),
    )

    h = x.reshape(n_tok, D_MODEL)
    for l in range(depth):
        xe, route = _mix_call(alpha, l, h, l > 0, n_seq, n_tok, mix_w)
        plan = _plan(route[0].astype(jnp.int32), n_tiles + GATHER_AHEAD)
        h = _moe_call(alpha, l, xe, plan, moe_w, n_tok, l < depth - 1)
    return h.reshape(n_seq, seq, D_MODEL)
```

```python
import functools

import jax
import jax.numpy as jnp
from jax import lax
from jax.experimental import pallas as pl
from jax.experimental.pallas import tpu as pltpu

D_MODEL = 1024
D_A = 1024
D_R = 1280
N_RG_BLOCKS = 10
RG_BLOCK = 128
RG_C = 8.0
CONV_A_WIDTH = 3
CONV_R_WIDTH = 4
D_IN = 3 * D_A + 2 * D_R + 2 * D_MODEL
N_EXPERTS = 16
N_GROUPS = 4
EXPERTS_PER_GROUP = 4
N_PAIRS = 6
N_CLASSES = N_GROUPS * N_PAIRS
D_FF = 512
LN_EPS = 1e-5

OFF_HA, OFF_BA, OFF_CA = 0, D_A, 2 * D_A
OFF_XR = 3 * D_A
OFF_YR = OFF_XR + D_R
OFF_GA = OFF_YR + D_R
OFF_GR = OFF_GA + D_MODEL
LATE_COLS = D_IN - OFF_YR
LATE_CHUNK = 256

SUBLANES = 8
LANES = 128
EXT = LANES
X_PITCH = D_MODEL // LANES
XE_PITCH = (D_MODEL + EXT) // LANES
TT = 512
TM = 256
DMA_UNROLL = 8
GATHER_AHEAD = 2
N_SLOTS = GATHER_AHEAD + 1
VMEM_LIMIT = 56 * 1024 * 1024

_BF = jnp.bfloat16
_F32 = jnp.float32

GELU_K1 = 0.7978845608028654
GELU_K2 = GELU_K1 * 0.044715


def _layer_norm(y, g, b):
    mu = jnp.mean(y, axis=-1, keepdims=True)
    yc = y - mu
    var = jnp.mean(yc * yc, axis=-1, keepdims=True)
    return yc * lax.rsqrt(var + LN_EPS) * g + b


def _first_index_of(vals, target):
    idx = jnp.full(target.shape, float(len(vals) - 1), _F32)
    for j in range(len(vals) - 2, -1, -1):
        idx = jnp.where(vals[j] == target, float(j), idx)
    return idx


def _route(logits_t):
    m = jnp.max(logits_t, axis=0, keepdims=True)
    e = jnp.exp(logits_t - m)
    p = e / jnp.sum(e, axis=0, keepdims=True)
    neg = jnp.full((1, logits_t.shape[1]), -jnp.inf, _F32)
    scores, m1s, i1s, m2s, i2s = [], [], [], [], []
    for g in range(N_GROUPS):
        v = [p[g * EXPERTS_PER_GROUP + j:g * EXPERTS_PER_GROUP + j + 1, :] for j in range(EXPERTS_PER_GROUP)]
        m1 = jnp.maximum(jnp.maximum(v[0], v[1]), jnp.maximum(v[2], v[3]))
        i1 = _first_index_of(v, m1)
        w = [jnp.where(i1 == float(j), neg, v[j]) for j in range(EXPERTS_PER_GROUP)]
        m2 = jnp.maximum(jnp.maximum(w[0], w[1]), jnp.maximum(w[2], w[3]))
        i2 = _first_index_of(w, m2)
        scores.append(m1 + m2)
        m1s.append(m1); i1s.append(i1); m2s.append(m2); i2s.append(i2)
    best = jnp.maximum(jnp.maximum(scores[0], scores[1]), jnp.maximum(scores[2], scores[3]))
    gsel = _first_index_of(scores, best)

    def pick(xs):
        out = xs[N_GROUPS - 1]
        for g in range(N_GROUPS - 2, -1, -1):
            out = jnp.where(gsel == float(g), xs[g], out)
        return out

    m1, i1, m2, i2 = pick(m1s), pick(i1s), pick(m2s), pick(i2s)
    den = m1 + m2
    gate1, gate2 = m1 / den, m2 / den
    first_is_lo = i1 < i2
    lo = jnp.where(first_is_lo, i1, i2)
    hi = jnp.where(first_is_lo, i2, i1)
    g_lo = jnp.where(first_is_lo, gate1, gate2)
    g_hi = jnp.where(first_is_lo, gate2, gate1)
    pair = lo * (7.0 - lo) * 0.5 + (hi - lo - 1.0)
    cls = gsel * float(N_PAIRS) + pair
    return cls, g_lo, g_hi


def _load_token_major(ref, n_rows, pitch, lead=()):
    cols = [ref[lead + (pl.ds(c, n_rows, stride=pitch), slice(None))] for c in range(D_MODEL // LANES)]
    return jnp.concatenate(cols, axis=1)


def _store_token_major(ref, val, pitch, lead=()):
    n_rows = val.shape[0]
    for c in range(val.shape[1] // LANES):
        ref[lead + (pl.ds(c, n_rows, stride=pitch), slice(None))] = val[:, c * LANES:(c + 1) * LANES]


def _causal_conv(u, hist_ref, w_ref, n_taps, out_ref, cols=slice(None), bias=None):
    tt = u.shape[0]

    def tap(d):
        return w_ref[n_taps - 1 - d:n_taps - d, cols]

    def tap0(val):
        out = val * tap(0)
        return out if bias is None else out + bias

    acc = tap0(u)
    for d in range(1, n_taps):
        acc = acc + pltpu.roll(u, d, 0) * tap(d)
    out_ref[:, cols] = acc
    head = u[0:SUBLANES, :]
    hist = hist_ref[:, cols]
    row = lax.broadcasted_iota(jnp.int32, head.shape, 0)
    acc_head = tap0(head)
    for d in range(1, n_taps):
        shifted = jnp.where(row < d, pltpu.roll(hist, d, 0), pltpu.roll(head, d, 0))
        acc_head = acc_head + shifted * tap(d)
    out_ref[pl.ds(0, SUBLANES), cols] = acc_head
    hist_ref[:, cols] = u[tt - SUBLANES:tt, :]


def _mix_kernel(alpha, x_token_major,
                x_ref, win_ref, caw_ref, wa_ref, crw_ref, crb_ref, wrg_ref, brga_ref, brgx_ref,
                lam_ref, wr_ref, wo_ref, g1_ref, b1_ref, wrt_hi_ref, wrt_lo_ref, brt_ref,
                xe_ref, route_ref,
                uhist, xrhist, cabuf, xcbuf, abuf, bbuf, hbuf, hcarry, latebuf, vabuf):
    t = pl.program_id(1)
    tt = route_ref.shape[1]

    @pl.when(t == 0)
    def _():
        uhist[...] = jnp.zeros_like(uhist)
        xrhist[...] = jnp.zeros_like(xrhist)
        hcarry[...] = jnp.zeros_like(hcarry)

    x = _load_token_major(x_ref, tt, X_PITCH) if x_token_major else x_ref[...]
    xb = x.astype(_BF)

    def proj(off, width):
        return jnp.dot(xb, win_ref[:, off:off + width], preferred_element_type=_F32)

    def mixer_a_chunk(c):
        cols = slice(c * LATE_CHUNK, (c + 1) * LATE_CHUNK)
        u = proj(OFF_CA + c * LATE_CHUNK, LATE_CHUNK) * proj(OFF_HA + c * LATE_CHUNK, LATE_CHUNK)
        _causal_conv(u, uhist, caw_ref, CONV_A_WIDTH, cabuf, cols=cols)
        vabuf[:, cols] = (proj(OFF_BA + c * LATE_CHUNK, LATE_CHUNK) * cabuf[:, cols]).astype(_BF)

    def late_chunk(j):
        latebuf[:, j * LATE_CHUNK:(j + 1) * LATE_CHUNK] = proj(OFF_YR + j * LATE_CHUNK, LATE_CHUNK)

    def xr_chunk(c):
        cols = slice(c * LATE_CHUNK, (c + 1) * LATE_CHUNK)
        _causal_conv(proj(OFF_XR + c * LATE_CHUNK, LATE_CHUNK), xrhist, crw_ref, CONV_R_WIDTH, xcbuf,
                     cols=cols, bias=crb_ref[:, cols])

    heads_per_chunk = LATE_CHUNK // RG_BLOCK
    z = -lam_ref[...]
    softplus_neg_lam = jnp.maximum(z, 0.0) + jnp.log1p(jnp.exp(-jnp.abs(z)))
    c_half = (-0.5 * RG_C) * softplus_neg_lam
    row = lax.broadcasted_iota(jnp.int32, (tt // SUBLANES, SUBLANES, RG_BLOCK), 1)
    side_work = [functools.partial(mixer_a_chunk, c) for c in range(D_A // LATE_CHUNK)]
    side_work += [functools.partial(late_chunk, j) for j in range(LATE_COLS // LATE_CHUNK)]
    side_cost = [3] * (D_A // LATE_CHUNK) + [1] * (LATE_COLS // LATE_CHUNK)
    per_head = sum(side_cost) / N_RG_BLOCKS
    done, spent = 0, 0.0
    for h in range(N_RG_BLOCKS):
        if h % heads_per_chunk == 0:
            xr_chunk(h // heads_per_chunk)
        while done < len(side_work) and (spent < (h + 1) * per_head or h == N_RG_BLOCKS - 1):
            side_work[done]()
            spent += side_cost[done]
            done += 1
        sl = slice(h * RG_BLOCK, (h + 1) * RG_BLOCK)
        xc = xcbuf[:, sl]
        gh = jnp.dot(xc.astype(_BF), wrg_ref[h], preferred_element_type=_F32)
        t_r = jnp.tanh(gh[:, :RG_BLOCK] + brga_ref[:, sl])
        t_i = jnp.tanh(gh[:, RG_BLOCK:] + brgx_ref[:, sl])
        log_a = c_half[:, sl] * t_r + c_half[:, sl]
        a_t = jnp.exp(log_a)
        b_t = jnp.exp(0.5 * jnp.log(1.0 - a_t * a_t)) * (t_i * xc + xc)
        a3 = a_t.reshape(tt // SUBLANES, SUBLANES, RG_BLOCK)
        b3 = b_t.reshape(tt // SUBLANES, SUBLANES, RG_BLOCK)
        for k in (1, 2, 4):
            keep = row >= k
            b3 = jnp.where(keep, a3 * pltpu.roll(b3, k, 1) + b3, b3)
            a3 = jnp.where(keep, a3 * pltpu.roll(a3, k, 1), a3)
        abuf[:, sl] = a3.reshape(tt, RG_BLOCK)
        bbuf[:, sl] = b3.reshape(tt, RG_BLOCK)

    h_prev = hcarry[...]
    for g in range(tt // SUBLANES):
        rows = pl.ds(g * SUBLANES, SUBLANES)
        hg = abuf[rows, :] * h_prev + bbuf[rows, :]
        hbuf[rows, :] = hg
        h_prev = hg[SUBLANES - 1:SUBLANES, :]
    hcarry[...] = h_prev
    ya = jnp.dot(vabuf[...], wa_ref[...], preferred_element_type=_F32)
    v = latebuf[:, :D_R]
    th = jnp.tanh(v * (GELU_K2 * (v * v) + GELU_K1))
    vr = (hbuf[...] * (v * th + v)).astype(_BF)
    yr = jnp.dot(vr, wr_ref[...], preferred_element_type=_F32)

    t_a = jnp.tanh(latebuf[:, OFF_GA - OFF_YR:OFF_GR - OFF_YR])
    t_g = jnp.tanh(latebuf[:, OFF_GR - OFF_YR:])
    merged2 = (t_a * ya + ya) + (t_g * yr + yr)
    o = jnp.dot(merged2.astype(_BF), wo_ref[...], preferred_element_type=_F32)
    x1 = _layer_norm(alpha * x + o, g1_ref[...], b1_ref[...])

    x1_hi = x1.astype(_BF)
    x1_lo = (x1 - x1_hi.astype(_F32)).astype(_BF)
    nt_dims = (((1,), (1,)), ((), ()))
    logits_t = (lax.dot_general(wrt_hi_ref[...], x1_hi, nt_dims, preferred_element_type=_F32)
                + lax.dot_general(wrt_hi_ref[...], x1_lo, nt_dims, preferred_element_type=_F32)
                + lax.dot_general(wrt_lo_ref[...], x1_hi, nt_dims, preferred_element_type=_F32)
                + brt_ref[...])
    cls, g_lo, g_hi = _route(logits_t)
    rsel = lax.broadcasted_iota(jnp.int32, (SUBLANES, tt), 0)
    route_ref[...] = jnp.where(rsel == 0, cls, jnp.where(rsel == 1, g_lo, jnp.where(rsel == 2, g_hi, 0.0)))
    esel = lax.broadcasted_iota(jnp.int32, (EXT, tt), 0)
    ext_t = jnp.where(esel == 0, cls, jnp.where(esel == 1, g_lo, jnp.where(esel == 2, g_hi, 0.0)))
    _store_token_major(xe_ref, jnp.concatenate([x1, ext_t.T], axis=1), XE_PITCH)


def _mix_call(alpha, layer, x2d, x_token_major, n_seq, n_tok, w):
    seq = n_tok // n_seq
    nt = seq // TT

    def layer_const(shape):
        zeros = (0,) * len(shape)
        return pl.BlockSpec((None,) + shape, lambda b, t: (layer,) + zeros, pipeline_mode=pl.Buffered(1))

    def const(shape):
        zeros = (0,) * len(shape)
        return pl.BlockSpec(shape, lambda b, t: zeros, pipeline_mode=pl.Buffered(1))

    in_specs = [
        (pl.BlockSpec((TT * X_PITCH, LANES), lambda b, t: (b * nt + t, 0)) if x_token_major
         else pl.BlockSpec((TT, D_MODEL), lambda b, t: (b * nt + t, 0))),
        layer_const((D_MODEL, D_IN)),
        layer_const((CONV_A_WIDTH, D_A)),
        layer_const((D_A, D_MODEL)),
        layer_const((CONV_R_WIDTH, D_R)),
        layer_const((1, D_R)),
        layer_const((N_RG_BLOCKS, RG_BLOCK, 2 * RG_BLOCK)),
        layer_const((1, D_R)),
        layer_const((1, D_R)),
        layer_const((1, D_R)),
        layer_const((D_R, D_MODEL)),
        layer_const((D_MODEL, D_MODEL)),
        layer_const((1, D_MODEL)),
        layer_const((1, D_MODEL)),
        const((N_EXPERTS, D_MODEL)),
        const((N_EXPERTS, D_MODEL)),
        const((N_EXPERTS, 1)),
    ]
    out_specs = [
        pl.BlockSpec((TT * XE_PITCH, LANES), lambda b, t: (b * nt + t, 0)),
        pl.BlockSpec((SUBLANES, TT), lambda b, t: (0, b * nt + t)),
    ]
    out_shape = [
        jax.ShapeDtypeStruct((n_tok * XE_PITCH, LANES), _F32),
        jax.ShapeDtypeStruct((SUBLANES, n_tok), _F32),
    ]
    scratch = [
        pltpu.VMEM((SUBLANES, D_A), _F32),
        pltpu.VMEM((SUBLANES, D_R), _F32),
        pltpu.VMEM((TT, D_A), _F32),
        pltpu.VMEM((TT, D_R), _F32),
        pltpu.VMEM((TT, D_R), _F32),
        pltpu.VMEM((TT, D_R), _F32),
        pltpu.VMEM((TT, D_R), _F32),
        pltpu.VMEM((1, D_R), _F32),
        pltpu.VMEM((TT, LATE_COLS), _F32),
        pltpu.VMEM((TT, D_A), _BF),
    ]
    return pl.pallas_call(
        functools.partial(_mix_kernel, alpha, x_token_major),
        grid=(n_seq, nt),
        in_specs=in_specs,
        out_specs=out_specs,
        out_shape=out_shape,
        scratch_shapes=scratch,
        compiler_params=pltpu.CompilerParams(
            dimension_semantics=("arbitrary", "arbitrary"), vmem_limit_bytes=VMEM_LIMIT),
        name="mix",
    )(x2d, *w)


def _moe_kernel(alpha, n_tiles, out_token_major,
                elo_ref, ehi_ref, nv_ref, off_ref, order_ref,
                xe_hbm,
                wg_lo, wu_lo, wd_lo, wg_hi, wu_hi, wd_hi, g2_ref, b2_ref,
                out_hbm,
                gbuf0, gbuf1, gbuf2, obuf0, obuf1, obuf2, gsem, ssem, pend):
    gbuf = (gbuf0, gbuf1, gbuf2)
    obuf = (obuf0, obuf1, obuf2)
    i = pl.program_id(0)
    nv = nv_ref[i]
    o_pitch = X_PITCH if out_token_major else 1

    def gather_copy(tok, r, s):
        return pltpu.make_async_copy(xe_hbm.at[pl.ds(tok * XE_PITCH, XE_PITCH), :],
                                     gbuf[s].at[pl.ds(r * XE_PITCH, XE_PITCH), :], gsem.at[s])

    def scatter_copy(tok, r, s, n_rows=1, group=None):
        dst = out_hbm.at[pl.ds(tok * o_pitch, n_rows * o_pitch), :]
        if out_token_major:
            src = r * o_pitch if isinstance(r, int) else pl.multiple_of(r * o_pitch, o_pitch)
            return pltpu.make_async_copy(obuf[s].at[pl.ds(src, n_rows * o_pitch), :], dst, ssem.at[s])
        if group is None:
            group, r = lax.shift_right_logical(r, 3), jnp.bitwise_and(r, SUBLANES - 1)
        return pltpu.make_async_copy(obuf[s].at[group, pl.ds(r, n_rows), :], dst, ssem.at[s])

    def scatter_wait_copy(s, n_rows):
        if out_token_major:
            return scatter_copy(0, 0, s, n_rows=n_rows)
        if n_rows < SUBLANES:
            block = obuf[s].at[0, pl.ds(0, n_rows), :]
        else:
            block = obuf[s].at[pl.ds(0, n_rows // SUBLANES)]
        return pltpu.make_async_copy(block, block, ssem.at[s])

    def start_gather(tile, s):
        base = off_ref[tile]
        for r in range(TM):
            gather_copy(order_ref[base + r], r, s).start(priority=r % 2)

    def wait_gather(s):
        pltpu.make_async_copy(xe_hbm.at[pl.ds(0, TM * XE_PITCH), :], gbuf[s], gsem.at[s]).wait()

    def start_scatter(tile, s, n_rows):
        base = off_ref[tile]
        n_chunks = lax.shift_right_logical(n_rows, DMA_UNROLL.bit_length() - 1)

        def chunk(c, carry):
            r0 = c * DMA_UNROLL
            for j in range(DMA_UNROLL):
                if out_token_major:
                    copy = scatter_copy(order_ref[base + r0 + j], r0 + j, s)
                else:
                    copy = scatter_copy(order_ref[base + r0 + j], j, s, group=c)
                copy.start(priority=j % 2)
            return carry
        lax.fori_loop(0, n_chunks, chunk, 0)

        def tail(r, carry):
            scatter_copy(order_ref[base + r], r, s).start()
            return carry
        lax.fori_loop(n_chunks * DMA_UNROLL, n_rows, tail, 0)

    def wait_scatter(s):
        n_rows = pend[s]
        width = TM
        while width >= 1:
            @pl.when(jnp.bitwise_and(n_rows, width) != 0)
            def _(width=width):
                scatter_wait_copy(s, width).wait()
            width //= 2
        pend[s] = 0

    @pl.when(i == 0)
    def _():
        for s in range(N_SLOTS):
            pend[s] = 0
        for ahead in range(GATHER_AHEAD):
            start_gather(ahead, ahead)

    def tile_body(s):
        wait_gather(s)
        wait_scatter(s)
        xf = _load_token_major(gbuf[s], TM, XE_PITCH)
        ext = gbuf[s][pl.ds(D_MODEL // LANES, TM, stride=XE_PITCH), :]
        gate_lo = ext[:, 1:2]
        gate_hi = ext[:, 2:3]
        xb = xf.astype(_BF)
        start_gather(i + GATHER_AHEAD, (s + GATHER_AHEAD) % N_SLOTS)

        def expert(wg, wu, wd, gate):
            a2 = jnp.dot(xb, wg[...], preferred_element_type=_F32)
            u = jnp.dot(xb, wu[...], preferred_element_type=_F32)
            hgt = ((a2 * jnp.tanh(a2) + a2) * (u * gate)).astype(_BF)
            return jnp.dot(hgt, wd[...], preferred_element_type=_F32)

        moe = expert(wg_lo, wu_lo, wd_lo, gate_lo) + expert(wg_hi, wu_hi, wd_hi, gate_hi)
        y = _layer_norm(alpha * xf + moe, g2_ref[...], b2_ref[...])
        if out_token_major:
            _store_token_major(obuf[s], y, X_PITCH)
        else:
            obuf[s][...] = y.reshape(TM // SUBLANES, SUBLANES, D_MODEL)
        start_scatter(i, s, nv)
        pend[s] = nv

    slot = lax.rem(i, N_SLOTS)
    issuer = jnp.maximum(i - GATHER_AHEAD, 0)
    gathered_unused = jnp.logical_and(nv == 0, jnp.logical_or(i < GATHER_AHEAD, nv_ref[issuer] > 0))
    for s in range(N_SLOTS):
        @pl.when(jnp.logical_and(nv > 0, slot == s))
        def _(s=s):
            tile_body(s)

        @pl.when(jnp.logical_and(gathered_unused, slot == s))
        def _(s=s):
            wait_gather(s)

    @pl.when(i == n_tiles - 1)
    def _():
        for s in range(N_SLOTS):
            wait_scatter(s)


def _moe_call(alpha, layer, xe, plan, w, n_tok, out_token_major):
    elo, ehi, nvalid, off, order = plan
    n_tiles = elo.shape[0] - GATHER_AHEAD
    wg, wu, wd, g2, b2 = w
    if out_token_major:
        obuf_shape, out_rows, out_cols = (TM * X_PITCH, LANES), n_tok * X_PITCH, LANES
    else:
        obuf_shape, out_rows, out_cols = (TM // SUBLANES, SUBLANES, D_MODEL), n_tok, D_MODEL

    def wspec(shape, which):
        def imap(i, elo_r, ehi_r, *_):
            return (layer, (elo_r, ehi_r)[which][i], 0, 0)
        return pl.BlockSpec((None, None) + shape, imap)

    grid_spec = pltpu.PrefetchScalarGridSpec(
        num_scalar_prefetch=5,
        grid=(n_tiles,),
        in_specs=[
            pl.BlockSpec(memory_space=pl.ANY),
            wspec((D_MODEL, D_FF), 0), wspec((D_MODEL, D_FF), 0), wspec((D_FF, D_MODEL), 0),
            wspec((D_MODEL, D_FF), 1), wspec((D_MODEL, D_FF), 1), wspec((D_FF, D_MODEL), 1),
            pl.BlockSpec((None, 1, D_MODEL), lambda i, *_: (layer, 0, 0)),
            pl.BlockSpec((None, 1, D_MODEL), lambda i, *_: (layer, 0, 0)),
        ],
        out_specs=pl.BlockSpec(memory_space=pl.ANY),
        scratch_shapes=[
            *[pltpu.VMEM((TM * XE_PITCH, LANES), _F32) for _ in range(N_SLOTS)],
            *[pltpu.VMEM(obuf_shape, _F32) for _ in range(N_SLOTS)],
            pltpu.SemaphoreType.DMA((N_SLOTS,)),
            pltpu.SemaphoreType.DMA((N_SLOTS,)),
            pltpu.SMEM((N_SLOTS,), jnp.int32),
        ],
    )
    return pl.pallas_call(
        functools.partial(_moe_kernel, alpha, n_tiles, out_token_major),
        grid_spec=grid_spec,
        out_shape=jax.ShapeDtypeStruct((out_rows, out_cols), _F32),
        compiler_params=pltpu.CompilerParams(
            dimension_semantics=("arbitrary",), vmem_limit_bytes=VMEM_LIMIT),
        name="moe",
    )(elo, ehi, nvalid, off, order, xe, wg, wu, wd, wg, wu, wd, g2, b2)


_PAIR_LO = (0, 0, 0, 1, 1, 2)
_PAIR_HI = (1, 2, 3, 2, 3, 3)


def _plan(cls, n_tiles):
    n_tok = cls.shape[0]
    tok = jnp.arange(n_tok, dtype=jnp.int32)
    order = jnp.sort(cls * n_tok + tok) % n_tok
    order = jnp.concatenate([order, jnp.zeros((TM,), jnp.int32)])
    counts = jnp.sum((cls[:, None] == jnp.arange(N_CLASSES, dtype=jnp.int32)[None, :]).astype(jnp.int32), axis=0)
    cstart = jnp.cumsum(counts) - counts
    tiles = (counts + TM - 1) // TM
    tile_end = jnp.cumsum(tiles)
    tile_start = tile_end - tiles
    ti = jnp.arange(n_tiles, dtype=jnp.int32)
    n_used = tile_end[-1]
    tsel = jnp.minimum(ti, n_used - 1)
    tcls = jnp.sum((tile_end[None, :] <= tsel[:, None]).astype(jnp.int32), axis=1)
    within = (tsel - tile_start[tcls]) * TM
    nvalid = jnp.where(ti < n_used, jnp.clip(counts[tcls] - within, 0, TM), 0).astype(jnp.int32)
    off = (cstart[tcls] + within).astype(jnp.int32)
    grp = tcls // N_PAIRS
    pair = tcls % N_PAIRS
    elo = grp * EXPERTS_PER_GROUP + jnp.asarray(_PAIR_LO, jnp.int32)[pair]
    ehi = grp * EXPERTS_PER_GROUP + jnp.asarray(_PAIR_HI, jnp.int32)[pair]
    return elo.astype(jnp.int32), ehi.astype(jnp.int32), nvalid, off, order


def kernel(x, w_in, conv_a_w, w_branch_a, conv_r_w, conv_r_b, w_rg_a, b_rg_a, w_rg_x, b_rg_x, rg_lambda, w_branch_r, w_out, ln1_g, ln1_b, w_router, b_router, w_exp_gate, w_exp_up, w_exp_down, ln2_g, ln2_b):
    n_seq, seq, _ = x.shape
    depth = w_in.shape[0]
    n_tok = n_seq * seq
    assert seq % TT == 0 and n_tok % TM == 0
    n_tiles = (n_tok + N_CLASSES * (TM - 1)) // TM + GATHER_AHEAD
    alpha = (2.0 * depth) ** 0.25

    wrt = w_router.T.astype(_F32)
    wrt_hi = wrt.astype(_BF)
    wrt_lo = (wrt - wrt_hi.astype(_F32)).astype(_BF)
    brt = b_router.astype(_F32).reshape(N_EXPERTS, 1)

    in_scale = jnp.concatenate([jnp.ones((OFF_GA,), _F32), jnp.full((D_IN - OFF_GA,), 0.5, _F32)])
    mix_w = (
        (w_in * in_scale).astype(_BF),
        conv_a_w,
        w_branch_a.astype(_BF),
        conv_r_w,
        conv_r_b.reshape(depth, 1, D_R),
        (0.5 * jnp.concatenate([w_rg_a, w_rg_x], axis=-1)).astype(_BF),
        0.5 * b_rg_a.reshape(depth, 1, D_R),
        0.5 * b_rg_x.reshape(depth, 1, D_R),
        rg_lambda.reshape(depth, 1, D_R),
        (0.25 * w_branch_r).astype(_BF),
        (0.5 * w_out).astype(_BF),
        ln1_g.reshape(depth, 1, D_MODEL),
        ln1_b.reshape(depth, 1, D_MODEL),
        wrt_hi, wrt_lo, brt,
    )
    moe_w = (
        (0.5 * w_exp_gate).astype(_BF), w_exp_up.astype(_BF), w_exp_down.astype(_BF),
        ln2_g.reshape(depth, 1, D_MODEL), ln2_b.reshape(depth, 1, D_MODEL),
    )

    h = x.reshape(n_tok, D_MODEL)
    for l in range(depth):
        xe, route = _mix_call(alpha, l, h, l > 0, n_seq, n_tok, mix_w)
        plan = _plan(route[0].astype(jnp.int32), n_tiles + GATHER_AHEAD)
        h = _moe_call(alpha, l, xe, plan, moe_w, n_tok, l < depth - 1)
    return h.reshape(n_seq, seq, D_MODEL)
```

```python
import functools

import jax
import jax.numpy as jnp
from jax import lax
from jax.experimental import pallas as pl
from jax.experimental.pallas import tpu as pltpu

D_MODEL = 1024
D_A = 1024
D_R = 1280
N_RG_BLOCKS = 10
RG_BLOCK = 128
RG_C = 8.0
CONV_A_WIDTH = 3
CONV_R_WIDTH = 4
D_IN = 3 * D_A + 2 * D_R + 2 * D_MODEL
N_EXPERTS = 16
N_GROUPS = 4
EXPERTS_PER_GROUP = 4
N_PAIRS = 6
N_CLASSES = N_GROUPS * N_PAIRS
D_FF = 512
LN_EPS = 1e-5

OFF_HA, OFF_BA, OFF_CA = 0, D_A, 2 * D_A
OFF_XR = 3 * D_A
OFF_YR = OFF_XR + D_R
OFF_GA = OFF_YR + D_R
OFF_GR = OFF_GA + D_MODEL
LATE_COLS = D_IN - OFF_YR
LATE_CHUNK = 256

SUBLANES = 8
LANES = 128
EXT = LANES
X_PITCH = D_MODEL // LANES
XE_PITCH = (D_MODEL + EXT) // LANES
TT = 512
TM = 256
DMA_UNROLL = 8
GATHER_AHEAD = 2
N_SLOTS = GATHER_AHEAD + 1
VMEM_LIMIT = 56 * 1024 * 1024

_BF = jnp.bfloat16
_F32 = jnp.float32

GELU_K1 = 0.7978845608028654
GELU_K2 = GELU_K1 * 0.044715


def _layer_norm(y, g, b):
    mu = jnp.mean(y, axis=-1, keepdims=True)
    yc = y - mu
    var = jnp.mean(yc * yc, axis=-1, keepdims=True)
    return yc * lax.rsqrt(var + LN_EPS) * g + b


def _first_index_of(vals, target):
    idx = jnp.full(target.shape, float(len(vals) - 1), _F32)
    for j in range(len(vals) - 2, -1, -1):
        idx = jnp.where(vals[j] == target, float(j), idx)
    return idx


def _route(logits_t):
    m = jnp.max(logits_t, axis=0, keepdims=True)
    e = jnp.exp(logits_t - m)
    p = e / jnp.sum(e, axis=0, keepdims=True)
    neg = jnp.full((1, logits_t.shape[1]), -jnp.inf, _F32)
    scores, m1s, i1s, m2s, i2s = [], [], [], [], []
    for g in range(N_GROUPS):
        v = [p[g * EXPERTS_PER_GROUP + j:g * EXPERTS_PER_GROUP + j + 1, :] for j in range(EXPERTS_PER_GROUP)]
        m1 = jnp.maximum(jnp.maximum(v[0], v[1]), jnp.maximum(v[2], v[3]))
        i1 = _first_index_of(v, m1)
        w = [jnp.where(i1 == float(j), neg, v[j]) for j in range(EXPERTS_PER_GROUP)]
        m2 = jnp.maximum(jnp.maximum(w[0], w[1]), jnp.maximum(w[2], w[3]))
        i2 = _first_index_of(w, m2)
        scores.append(m1 + m2)
        m1s.append(m1); i1s.append(i1); m2s.append(m2); i2s.append(i2)
    best = jnp.maximum(jnp.maximum(scores[0], scores[1]), jnp.maximum(scores[2], scores[3]))
    gsel = _first_index_of(scores, best)

    def pick(xs):
        out = xs[N_GROUPS - 1]
        for g in range(N_GROUPS - 2, -1, -1):
            out = jnp.where(gsel == float(g), xs[g], out)
        return out

    m1, i1, m2, i2 = pick(m1s), pick(i1s), pick(m2s), pick(i2s)
    den = m1 + m2
    gate1, gate2 = m1 / den, m2 / den
    first_is_lo = i1 < i2
    lo = jnp.where(first_is_lo, i1, i2)
    hi = jnp.where(first_is_lo, i2, i1)
    g_lo = jnp.where(first_is_lo, gate1, gate2)
    g_hi = jnp.where(first_is_lo, gate2, gate1)
    pair = lo * (7.0 - lo) * 0.5 + (hi - lo - 1.0)
    cls = gsel * float(N_PAIRS) + pair
    return cls, g_lo, g_hi


def _load_token_major(ref, n_rows, pitch, lead=()):
    cols = [ref[lead + (pl.ds(c, n_rows, stride=pitch), slice(None))] for c in range(D_MODEL // LANES)]
    return jnp.concatenate(cols, axis=1)


def _store_token_major(ref, val, pitch, lead=()):
    n_rows = val.shape[0]
    for c in range(val.shape[1] // LANES):
        ref[lead + (pl.ds(c, n_rows, stride=pitch), slice(None))] = val[:, c * LANES:(c + 1) * LANES]


def _causal_conv(u, hist_ref, w_ref, n_taps, out_ref, cols=slice(None), bias=None):
    tt = u.shape[0]

    def tap(d):
        return w_ref[n_taps - 1 - d:n_taps - d, cols]

    def tap0(val):
        out = val * tap(0)
        return out if bias is None else out + bias

    acc = tap0(u)
    for d in range(1, n_taps):
        acc = acc + pltpu.roll(u, d, 0) * tap(d)
    out_ref[:, cols] = acc
    head = u[0:SUBLANES, :]
    hist = hist_ref[:, cols]
    row = lax.broadcasted_iota(jnp.int32, head.shape, 0)
    acc_head = tap0(head)
    for d in range(1, n_taps):
        shifted = jnp.where(row < d, pltpu.roll(hist, d, 0), pltpu.roll(head, d, 0))
        acc_head = acc_head + shifted * tap(d)
    out_ref[pl.ds(0, SUBLANES), cols] = acc_head
    hist_ref[:, cols] = u[tt - SUBLANES:tt, :]


def _mix_kernel(alpha, x_token_major,
                x_ref, win_ref, caw_ref, wa_ref, crw_ref, crb_ref, wrg_ref, brga_ref, brgx_ref,
                lam_ref, wr_ref, wo_ref, g1_ref, b1_ref, wrt_hi_ref, wrt_lo_ref, brt_ref,
                xe_ref, route_ref,
                uhist, xrhist, cabuf, xcbuf, abuf, bbuf, hbuf, hcarry, latebuf, vabuf):
    t = pl.program_id(1)
    tt = route_ref.shape[1]

    @pl.when(t == 0)
    def _():
        uhist[...] = jnp.zeros_like(uhist)
        xrhist[...] = jnp.zeros_like(xrhist)
        hcarry[...] = jnp.zeros_like(hcarry)

    x = _load_token_major(x_ref, tt, X_PITCH) if x_token_major else x_ref[...]
    xb = x.astype(_BF)

    def proj(off, width):
        return jnp.dot(xb, win_ref[:, off:off + width], preferred_element_type=_F32)

    def mixer_a_chunk(c):
        cols = slice(c * LATE_CHUNK, (c + 1) * LATE_CHUNK)
        u = proj(OFF_CA + c * LATE_CHUNK, LATE_CHUNK) * proj(OFF_HA + c * LATE_CHUNK, LATE_CHUNK)
        _causal_conv(u, uhist, caw_ref, CONV_A_WIDTH, cabuf, cols=cols)
        vabuf[:, cols] = (proj(OFF_BA + c * LATE_CHUNK, LATE_CHUNK) * cabuf[:, cols]).astype(_BF)

    def late_chunk(j):
        latebuf[:, j * LATE_CHUNK:(j + 1) * LATE_CHUNK] = proj(OFF_YR + j * LATE_CHUNK, LATE_CHUNK)

    def xr_chunk(c):
        cols = slice(c * LATE_CHUNK, (c + 1) * LATE_CHUNK)
        _causal_conv(proj(OFF_XR + c * LATE_CHUNK, LATE_CHUNK), xrhist, crw_ref, CONV_R_WIDTH, xcbuf,
                     cols=cols, bias=crb_ref[:, cols])

    heads_per_chunk = LATE_CHUNK // RG_BLOCK
    z = -lam_ref[...]
    softplus_neg_lam = jnp.maximum(z, 0.0) + jnp.log1p(jnp.exp(-jnp.abs(z)))
    c_half = (-0.5 * RG_C) * softplus_neg_lam
    row = lax.broadcasted_iota(jnp.int32, (tt // SUBLANES, SUBLANES, RG_BLOCK), 1)
    side_work = [functools.partial(mixer_a_chunk, c) for c in range(D_A // LATE_CHUNK)]
    side_work += [functools.partial(late_chunk, j) for j in range(LATE_COLS // LATE_CHUNK)]
    side_cost = [3] * (D_A // LATE_CHUNK) + [1] * (LATE_COLS // LATE_CHUNK)
    per_head = sum(side_cost) / N_RG_BLOCKS
    done, spent = 0, 0.0
    for h in range(N_RG_BLOCKS):
        if h % heads_per_chunk == 0:
            xr_chunk(h // heads_per_chunk)
        while done < len(side_work) and (spent < (h + 1) * per_head or h == N_RG_BLOCKS - 1):
            side_work[done]()
            spent += side_cost[done]
            done += 1
        sl = slice(h * RG_BLOCK, (h + 1) * RG_BLOCK)
        xc = xcbuf[:, sl]
        gh = jnp.dot(xc.astype(_BF), wrg_ref[h], preferred_element_type=_F32)
        t_r = jnp.tanh(gh[:, :RG_BLOCK] + brga_ref[:, sl])
        t_i = jnp.tanh(gh[:, RG_BLOCK:] + brgx_ref[:, sl])
        log_a = c_half[:, sl] * t_r + c_half[:, sl]
        a_t = jnp.exp(log_a)
        b_t = jnp.exp(0.5 * jnp.log(1.0 - a_t * a_t)) * (t_i * xc + xc)
        a3 = a_t.reshape(tt // SUBLANES, SUBLANES, RG_BLOCK)
        b3 = b_t.reshape(tt // SUBLANES, SUBLANES, RG_BLOCK)
        for k in (1, 2, 4):
            keep = row >= k
            b3 = jnp.where(keep, a3 * pltpu.roll(b3, k, 1) + b3, b3)
            a3 = jnp.where(keep, a3 * pltpu.roll(a3, k, 1), a3)
        abuf[:, sl] = a3.reshape(tt, RG_BLOCK)
        bbuf[:, sl] = b3.reshape(tt, RG_BLOCK)

    h_prev = hcarry[...]
    for g in range(tt // SUBLANES):
        rows = pl.ds(g * SUBLANES, SUBLANES)
        hg = abuf[rows, :] * h_prev + bbuf[rows, :]
        hbuf[rows, :] = hg
        h_prev = hg[SUBLANES - 1:SUBLANES, :]
    hcarry[...] = h_prev
    ya = jnp.dot(vabuf[...], wa_ref[...], preferred_element_type=_F32)
    v = latebuf[:, :D_R]
    th = jnp.tanh(v * (GELU_K2 * (v * v) + GELU_K1))
    vr = (hbuf[...] * (v * th + v)).astype(_BF)
    yr = jnp.dot(vr, wr_ref[...], preferred_element_type=_F32)

    t_a = jnp.tanh(latebuf[:, OFF_GA - OFF_YR:OFF_GR - OFF_YR])
    t_g = jnp.tanh(latebuf[:, OFF_GR - OFF_YR:])
    merged2 = (t_a * ya + ya) + (t_g * yr + yr)
    o = jnp.dot(merged2.astype(_BF), wo_ref[...], preferred_element_type=_F32)
    x1 = _layer_norm(alpha * x + o, g1_ref[...], b1_ref[...])

    x1_hi = x1.astype(_BF)
    x1_lo = (x1 - x1_hi.astype(_F32)).astype(_BF)
    nt_dims = (((1,), (1,)), ((), ()))
    logits_t = (lax.dot_general(wrt_hi_ref[...], x1_hi, nt_dims, preferred_element_type=_F32)
                + lax.dot_general(wrt_hi_ref[...], x1_lo, nt_dims, preferred_element_type=_F32)
                + lax.dot_general(wrt_lo_ref[...], x1_hi, nt_dims, preferred_element_type=_F32)
                + brt_ref[...])
    cls, g_lo, g_hi = _route(logits_t)
    rsel = lax.broadcasted_iota(jnp.int32, (SUBLANES, tt), 0)
    route_ref[...] = jnp.where(rsel == 0, cls, jnp.where(rsel == 1, g_lo, jnp.where(rsel == 2, g_hi, 0.0)))
    esel = lax.broadcasted_iota(jnp.int32, (EXT, tt), 0)
    ext_t = jnp.where(esel == 0, cls, jnp.where(esel == 1, g_lo, jnp.where(esel == 2, g_hi, 0.0)))
    _store_token_major(xe_ref, jnp.concatenate([x1, ext_t.T], axis=1), XE_PITCH)


def _mix_call(alpha, layer, x2d, x_token_major, n_seq, n_tok, w):
    seq = n_tok // n_seq
    nt = seq // TT

    def layer_const(shape):
        zeros = (0,) * len(shape)
        return pl.BlockSpec((None,) + shape, lambda b, t: (layer,) + zeros, pipeline_mode=pl.Buffered(1))

    def const(shape):
        zeros = (0,) * len(shape)
        return pl.BlockSpec(shape, lambda b, t: zeros, pipeline_mode=pl.Buffered(1))

    in_specs = [
        (pl.BlockSpec((TT * X_PITCH, LANES), lambda b, t: (b * nt + t, 0)) if x_token_major
         else pl.BlockSpec((TT, D_MODEL), lambda b, t: (b * nt + t, 0))),
        layer_const((D_MODEL, D_IN)),
        layer_const((CONV_A_WIDTH, D_A)),
        layer_const((D_A, D_MODEL)),
        layer_const((CONV_R_WIDTH, D_R)),
        layer_const((1, D_R)),
        layer_const((N_RG_BLOCKS, RG_BLOCK, 2 * RG_BLOCK)),
        layer_const((1, D_R)),
        layer_const((1, D_R)),
        layer_const((1, D_R)),
        layer_const((D_R, D_MODEL)),
        layer_const((D_MODEL, D_MODEL)),
        layer_const((1, D_MODEL)),
        layer_const((1, D_MODEL)),
        const((N_EXPERTS, D_MODEL)),
        const((N_EXPERTS, D_MODEL)),
        const((N_EXPERTS, 1)),
    ]
    out_specs = [
        pl.BlockSpec((TT * XE_PITCH, LANES), lambda b, t: (b * nt + t, 0)),
        pl.BlockSpec((SUBLANES, TT), lambda b, t: (0, b * nt + t)),
    ]
    out_shape = [
        jax.ShapeDtypeStruct((n_tok * XE_PITCH, LANES), _F32),
        jax.ShapeDtypeStruct((SUBLANES, n_tok), _F32),
    ]
    scratch = [
        pltpu.VMEM((SUBLANES, D_A), _F32),
        pltpu.VMEM((SUBLANES, D_R), _F32),
        pltpu.VMEM((TT, D_A), _F32),
        pltpu.VMEM((TT, D_R), _F32),
        pltpu.VMEM((TT, D_R), _F32),
        pltpu.VMEM((TT, D_R), _F32),
        pltpu.VMEM((TT, D_R), _F32),
        pltpu.VMEM((1, D_R), _F32),
        pltpu.VMEM((TT, LATE_COLS), _F32),
        pltpu.VMEM((TT, D_A), _BF),
    ]
    return pl.pallas_call(
        functools.partial(_mix_kernel, alpha, x_token_major),
        grid=(n_seq, nt),
        in_specs=in_specs,
        out_specs=out_specs,
        out_shape=out_shape,
        scratch_shapes=scratch,
        compiler_params=pltpu.CompilerParams(
            dimension_semantics=("arbitrary", "arbitrary"), vmem_limit_bytes=VMEM_LIMIT),
        name="mix",
    )(x2d, *w)


def _moe_kernel(alpha, n_tiles, n_tok, out_token_major,
                elo_ref, ehi_ref, nv_ref, off_ref, order_ref,
                xe_hbm,
                wg_lo, wu_lo, wd_lo, wg_hi, wu_hi, wd_hi, g2_ref, b2_ref,
                out_hbm,
                gbuf0, gbuf1, gbuf2, obuf0, obuf1, obuf2, gsem, ssem, pend):
    gbuf = (gbuf0, gbuf1, gbuf2)
    obuf = (obuf0, obuf1, obuf2)
    i = pl.program_id(0)
    nv = nv_ref[i]
    o_pitch = X_PITCH if out_token_major else 1
    defer_scatter = out_token_major

    def gather_copy(tok, r, s):
        return pltpu.make_async_copy(xe_hbm.at[pl.ds(tok * XE_PITCH, XE_PITCH), :],
                                     gbuf[s].at[pl.ds(r * XE_PITCH, XE_PITCH), :], gsem.at[s])

    def scatter_copy(tok, r, s, n_rows=1, group=None):
        dst = out_hbm.at[pl.ds(tok * o_pitch, n_rows * o_pitch), :]
        if out_token_major:
            src = r * o_pitch if isinstance(r, int) else pl.multiple_of(r * o_pitch, o_pitch)
            return pltpu.make_async_copy(obuf[s].at[pl.ds(src, n_rows * o_pitch), :], dst, ssem.at[s])
        if group is None:
            group, r = lax.shift_right_logical(r, 3), jnp.bitwise_and(r, SUBLANES - 1)
        return pltpu.make_async_copy(obuf[s].at[group, pl.ds(r, n_rows), :], dst, ssem.at[s])

    def scatter_wait_copy(s, n_rows):
        if out_token_major:
            return scatter_copy(0, 0, s, n_rows=n_rows)
        if n_rows < SUBLANES:
            block = obuf[s].at[0, pl.ds(0, n_rows), :]
        else:
            block = obuf[s].at[pl.ds(0, n_rows // SUBLANES)]
        return pltpu.make_async_copy(block, block, ssem.at[s])

    def start_gather(tile, s):
        base = off_ref[tile]
        for r in range(TM):
            gather_copy(order_ref[base + r], r, s).start(priority=r % 2)

    def wait_gather(s):
        pltpu.make_async_copy(xe_hbm.at[pl.ds(0, TM * XE_PITCH), :], gbuf[s], gsem.at[s]).wait()

    def start_scatter(tile, s, n_rows):
        base = off_ref[tile]
        n_chunks = lax.shift_right_logical(n_rows, DMA_UNROLL.bit_length() - 1)

        def chunk(c, carry):
            r0 = c * DMA_UNROLL
            for j in range(DMA_UNROLL):
                if out_token_major:
                    copy = scatter_copy(order_ref[base + r0 + j], r0 + j, s)
                else:
                    copy = scatter_copy(order_ref[base + r0 + j], j, s, group=c)
                copy.start(priority=j % 2)
            return carry
        lax.fori_loop(0, n_chunks, chunk, 0)

        def tail(r, carry):
            scatter_copy(order_ref[base + r], r, s).start()
            return carry
        lax.fori_loop(n_chunks * DMA_UNROLL, n_rows, tail, 0)

    def start_scatter_full(tile, n_valid, s):
        base = off_ref[tile]
        for r in range(TM):
            tok = jnp.where(r < n_valid, order_ref[base + r], n_tok + s * TM + r)
            scatter_copy(tok, r, s).start(priority=r % 2)
        pend[s] = TM

    def wait_scatter(s):
        n_rows = pend[s]
        width = TM
        while width >= (TM if defer_scatter else 1):
            @pl.when(jnp.bitwise_and(n_rows, width) != 0)
            def _(width=width):
                scatter_wait_copy(s, width).wait()
            width //= 2
        pend[s] = 0

    prev = jnp.maximum(i - 1, 0)
    nv_prev = jnp.where(i > 0, nv_ref[prev], 0)

    @pl.when(i == 0)
    def _():
        for s in range(N_SLOTS):
            pend[s] = 0
        for ahead in range(GATHER_AHEAD):
            start_gather(ahead, ahead)
        if defer_scatter:
            obuf[N_SLOTS - 1][...] = jnp.zeros_like(obuf[N_SLOTS - 1])
            for s in range(N_SLOTS):
                fill = scatter_copy(n_tok + s * TM, 0, N_SLOTS - 1, n_rows=TM)
                fill.start()
                fill.wait()

    def tile_body(s):
        wait_gather(s)
        wait_scatter(s)
        xf = _load_token_major(gbuf[s], TM, XE_PITCH)
        ext = gbuf[s][pl.ds(D_MODEL // LANES, TM, stride=XE_PITCH), :]
        gate_lo = ext[:, 1:2]
        gate_hi = ext[:, 2:3]
        xb = xf.astype(_BF)
        start_gather(i + GATHER_AHEAD, (s + GATHER_AHEAD) % N_SLOTS)
        if defer_scatter:
            start_scatter_full(prev, nv_prev, (s + N_SLOTS - 1) % N_SLOTS)

        def expert(wg, wu, wd, gate):
            a2 = jnp.dot(xb, wg[...], preferred_element_type=_F32)
            u = jnp.dot(xb, wu[...], preferred_element_type=_F32)
            hgt = ((a2 * jnp.tanh(a2) + a2) * (u * gate)).astype(_BF)
            return jnp.dot(hgt, wd[...], preferred_element_type=_F32)

        moe = expert(wg_lo, wu_lo, wd_lo, gate_lo) + expert(wg_hi, wu_hi, wd_hi, gate_hi)
        y = _layer_norm(alpha * xf + moe, g2_ref[...], b2_ref[...])
        if out_token_major:
            _store_token_major(obuf[s], y, X_PITCH)
        else:
            obuf[s][...] = y.reshape(TM // SUBLANES, SUBLANES, D_MODEL)
        if not defer_scatter:
            start_scatter(i, s, nv)
            pend[s] = nv

    slot = lax.rem(i, N_SLOTS)
    issuer = jnp.maximum(i - GATHER_AHEAD, 0)
    gathered_unused = jnp.logical_and(nv == 0, jnp.logical_or(i < GATHER_AHEAD, nv_ref[issuer] > 0))
    for s in range(N_SLOTS):
        @pl.when(jnp.logical_and(nv > 0, slot == s))
        def _(s=s):
            tile_body(s)

        @pl.when(jnp.logical_and(gathered_unused, slot == s))
        def _(s=s):
            wait_gather(s)

        if defer_scatter:
            @pl.when(jnp.logical_and(jnp.logical_and(nv == 0, nv_prev > 0), slot == s))
            def _(s=s):
                start_scatter_full(prev, nv_prev, (s + N_SLOTS - 1) % N_SLOTS)

    @pl.when(i == n_tiles - 1)
    def _():
        for s in range(N_SLOTS):
            wait_scatter(s)


def _moe_call(alpha, layer, xe, plan, w, n_tok, out_token_major):
    elo, ehi, nvalid, off, order = plan
    n_tiles = elo.shape[0] - GATHER_AHEAD
    wg, wu, wd, g2, b2 = w
    if out_token_major:
        obuf_shape, out_rows, out_cols = (TM * X_PITCH, LANES), (n_tok + N_SLOTS * TM) * X_PITCH, LANES
    else:
        obuf_shape, out_rows, out_cols = (TM // SUBLANES, SUBLANES, D_MODEL), n_tok, D_MODEL

    def wspec(shape, which):
        def imap(i, elo_r, ehi_r, *_):
            return (layer, (elo_r, ehi_r)[which][i], 0, 0)
        return pl.BlockSpec((None, None) + shape, imap)

    grid_spec = pltpu.PrefetchScalarGridSpec(
        num_scalar_prefetch=5,
        grid=(n_tiles,),
        in_specs=[
            pl.BlockSpec(memory_space=pl.ANY),
            wspec((D_MODEL, D_FF), 0), wspec((D_MODEL, D_FF), 0), wspec((D_FF, D_MODEL), 0),
            wspec((D_MODEL, D_FF), 1), wspec((D_MODEL, D_FF), 1), wspec((D_FF, D_MODEL), 1),
            pl.BlockSpec((None, 1, D_MODEL), lambda i, *_: (layer, 0, 0)),
            pl.BlockSpec((None, 1, D_MODEL), lambda i, *_: (layer, 0, 0)),
        ],
        out_specs=pl.BlockSpec(memory_space=pl.ANY),
        scratch_shapes=[
            *[pltpu.VMEM((TM * XE_PITCH, LANES), _F32) for _ in range(N_SLOTS)],
            *[pltpu.VMEM(obuf_shape, _F32) for _ in range(N_SLOTS)],
            pltpu.SemaphoreType.DMA((N_SLOTS,)),
            pltpu.SemaphoreType.DMA((N_SLOTS,)),
            pltpu.SMEM((N_SLOTS,), jnp.int32),
        ],
    )
    return pl.pallas_call(
        functools.partial(_moe_kernel, alpha, n_tiles, n_tok, out_token_major),
        grid_spec=grid_spec,
        out_shape=jax.ShapeDtypeStruct((out_rows, out_cols), _F32),
        compiler_params=pltpu.CompilerParams(
            dimension_semantics=("arbitrary",), vmem_limit_bytes=VMEM_LIMIT),
        name="moe",
    )(elo, ehi, nvalid, off, order, xe, wg, wu, wd, wg, wu, wd, g2, b2)


_PAIR_LO = (0, 0, 0, 1, 1, 2)
_PAIR_HI = (1, 2, 3, 2, 3, 3)


def _plan(cls, n_tiles):
    n_tok = cls.shape[0]
    tok = jnp.arange(n_tok, dtype=jnp.int32)
    order = jnp.sort(cls * n_tok + tok) % n_tok
    order = jnp.concatenate([order, jnp.zeros((TM,), jnp.int32)])
    counts = jnp.sum((cls[:, None] == jnp.arange(N_CLASSES, dtype=jnp.int32)[None, :]).astype(jnp.int32), axis=0)
    cstart = jnp.cumsum(counts) - counts
    tiles = (counts + TM - 1) // TM
    tile_end = jnp.cumsum(tiles)
    tile_start = tile_end - tiles
    ti = jnp.arange(n_tiles, dtype=jnp.int32)
    n_used = tile_end[-1]
    tsel = jnp.minimum(ti, n_used - 1)
    tcls = jnp.sum((tile_end[None, :] <= tsel[:, None]).astype(jnp.int32), axis=1)
    within = (tsel - tile_start[tcls]) * TM
    nvalid = jnp.where(ti < n_used, jnp.clip(counts[tcls] - within, 0, TM), 0).astype(jnp.int32)
    off = (cstart[tcls] + within).astype(jnp.int32)
    grp = tcls // N_PAIRS
    pair = tcls % N_PAIRS
    elo = grp * EXPERTS_PER_GROUP + jnp.asarray(_PAIR_LO, jnp.int32)[pair]
    ehi = grp * EXPERTS_PER_GROUP + jnp.asarray(_PAIR_HI, jnp.int32)[pair]
    return elo.astype(jnp.int32), ehi.astype(jnp.int32), nvalid, off, order


def kernel(x, w_in, conv_a_w, w_branch_a, conv_r_w, conv_r_b, w_rg_a, b_rg_a, w_rg_x, b_rg_x, rg_lambda, w_branch_r, w_out, ln1_g, ln1_b, w_router, b_router, w_exp_gate, w_exp_up, w_exp_down, ln2_g, ln2_b):
    n_seq, seq, _ = x.shape
    depth = w_in.shape[0]
    n_tok = n_seq * seq
    assert seq % TT == 0 and n_tok % TM == 0
    n_tiles = (n_tok + N_CLASSES * (TM - 1)) // TM + GATHER_AHEAD
    alpha = (2.0 * depth) ** 0.25

    wrt = w_router.T.astype(_F32)
    wrt_hi = wrt.astype(_BF)
    wrt_lo = (wrt - wrt_hi.astype(_F32)).astype(_BF)
    brt = b_router.astype(_F32).reshape(N_EXPERTS, 1)

    in_scale = jnp.concatenate([jnp.ones((OFF_GA,), _F32), jnp.full((D_IN - OFF_GA,), 0.5, _F32)])
    mix_w = (
        (w_in * in_scale).astype(_BF),
        conv_a_w,
        w_branch_a.astype(_BF),
        conv_r_w,
        conv_r_b.reshape(depth, 1, D_R),
        (0.5 * jnp.concatenate([w_rg_a, w_rg_x], axis=-1)).astype(_BF),
        0.5 * b_rg_a.reshape(depth, 1, D_R),
        0.5 * b_rg_x.reshape(depth, 1, D_R),
        rg_lambda.reshape(depth, 1, D_R),
        (0.25 * w_branch_r).astype(_BF),
        (0.5 * w_out).astype(_BF),
        ln1_g.reshape(depth, 1, D_MODEL),
        ln1_b.reshape(depth, 1, D_MODEL),
        wrt_hi, wrt_lo, brt,
    )
    moe_w = (
        (0.5 * w_exp_gate).astype(_BF), w_exp_up.astype(_BF), w_exp_down.astype(_BF),
        ln2_g.reshape(depth, 1, D_MODEL), ln2_b.reshape(depth, 1, D_MODEL),
    )

    h = x.reshape(n_tok, D_MODEL)
    for l in range(depth):
        xe, route = _mix_call(alpha, l, h, l > 0, n_seq, n_tok, mix_w)
        plan = _plan(route[0].astype(jnp.int32), n_tiles + GATHER_AHEAD)
        h = _moe_call(alpha, l, xe, plan, moe_w, n_tok, l < depth - 1)
    return h.reshape(n_seq, seq, D_MODEL)
```

```python
import functools

import jax
import jax.numpy as jnp
from jax import lax
from jax.experimental import pallas as pl
from jax.experimental.pallas import tpu as pltpu

D_MODEL = 1024
D_A = 1024
D_R = 1280
N_RG_BLOCKS = 10
RG_BLOCK = 128
RG_C = 8.0
CONV_A_WIDTH = 3
CONV_R_WIDTH = 4
D_IN = 3 * D_A + 2 * D_R + 2 * D_MODEL
N_EXPERTS = 16
N_GROUPS = 4
EXPERTS_PER_GROUP = 4
N_PAIRS = 6
N_CLASSES = N_GROUPS * N_PAIRS
D_FF = 512
LN_EPS = 1e-5

OFF_HA, OFF_BA, OFF_CA = 0, D_A, 2 * D_A
OFF_XR = 3 * D_A
OFF_YR = OFF_XR + D_R
OFF_GA = OFF_YR + D_R
OFF_GR = OFF_GA + D_MODEL
LATE_COLS = D_IN - OFF_YR
LATE_CHUNK = 256

SUBLANES = 8
LANES = 128
EXT = LANES
X_PITCH = D_MODEL // LANES
XE_PITCH = (D_MODEL + EXT) // LANES
TT = 512
TM = 256
DMA_UNROLL = 8
GATHER_AHEAD = 2
N_SLOTS = GATHER_AHEAD + 1
VMEM_LIMIT = 56 * 1024 * 1024

_BF = jnp.bfloat16
_F32 = jnp.float32

GELU_K1 = 0.7978845608028654
GELU_K2 = GELU_K1 * 0.044715


def _layer_norm(y, g, b):
    mu = jnp.mean(y, axis=-1, keepdims=True)
    yc = y - mu
    var = jnp.mean(yc * yc, axis=-1, keepdims=True)
    return yc * lax.rsqrt(var + LN_EPS) * g + b


def _first_index_of(vals, target):
    idx = jnp.full(target.shape, float(len(vals) - 1), _F32)
    for j in range(len(vals) - 2, -1, -1):
        idx = jnp.where(vals[j] == target, float(j), idx)
    return idx


def _route(logits_t):
    m = jnp.max(logits_t, axis=0, keepdims=True)
    e = jnp.exp(logits_t - m)
    p = e / jnp.sum(e, axis=0, keepdims=True)
    neg = jnp.full((1, logits_t.shape[1]), -jnp.inf, _F32)
    scores, m1s, i1s, m2s, i2s = [], [], [], [], []
    for g in range(N_GROUPS):
        v = [p[g * EXPERTS_PER_GROUP + j:g * EXPERTS_PER_GROUP + j + 1, :] for j in range(EXPERTS_PER_GROUP)]
        m1 = jnp.maximum(jnp.maximum(v[0], v[1]), jnp.maximum(v[2], v[3]))
        i1 = _first_index_of(v, m1)
        w = [jnp.where(i1 == float(j), neg, v[j]) for j in range(EXPERTS_PER_GROUP)]
        m2 = jnp.maximum(jnp.maximum(w[0], w[1]), jnp.maximum(w[2], w[3]))
        i2 = _first_index_of(w, m2)
        scores.append(m1 + m2)
        m1s.append(m1); i1s.append(i1); m2s.append(m2); i2s.append(i2)
    best = jnp.maximum(jnp.maximum(scores[0], scores[1]), jnp.maximum(scores[2], scores[3]))
    gsel = _first_index_of(scores, best)

    def pick(xs):
        out = xs[N_GROUPS - 1]
        for g in range(N_GROUPS - 2, -1, -1):
            out = jnp.where(gsel == float(g), xs[g], out)
        return out

    m1, i1, m2, i2 = pick(m1s), pick(i1s), pick(m2s), pick(i2s)
    den = m1 + m2
    gate1, gate2 = m1 / den, m2 / den
    first_is_lo = i1 < i2
    lo = jnp.where(first_is_lo, i1, i2)
    hi = jnp.where(first_is_lo, i2, i1)
    g_lo = jnp.where(first_is_lo, gate1, gate2)
    g_hi = jnp.where(first_is_lo, gate2, gate1)
    pair = lo * (7.0 - lo) * 0.5 + (hi - lo - 1.0)
    cls = gsel * float(N_PAIRS) + pair
    return cls, g_lo, g_hi


def _load_token_major(ref, n_rows, pitch, lead=()):
    cols = [ref[lead + (pl.ds(c, n_rows, stride=pitch), slice(None))] for c in range(D_MODEL // LANES)]
    return jnp.concatenate(cols, axis=1)


def _store_token_major(ref, val, pitch, lead=()):
    n_rows = val.shape[0]
    for c in range(val.shape[1] // LANES):
        ref[lead + (pl.ds(c, n_rows, stride=pitch), slice(None))] = val[:, c * LANES:(c + 1) * LANES]


def _causal_conv(u, hist_ref, w_ref, n_taps, out_ref, cols=slice(None), bias=None):
    tt = u.shape[0]

    def tap(d):
        return w_ref[n_taps - 1 - d:n_taps - d, cols]

    def tap0(val):
        out = val * tap(0)
        return out if bias is None else out + bias

    acc = tap0(u)
    for d in range(1, n_taps):
        acc = acc + pltpu.roll(u, d, 0) * tap(d)
    out_ref[:, cols] = acc
    head = u[0:SUBLANES, :]
    hist = hist_ref[:, cols]
    row = lax.broadcasted_iota(jnp.int32, head.shape, 0)
    acc_head = tap0(head)
    for d in range(1, n_taps):
        shifted = jnp.where(row < d, pltpu.roll(hist, d, 0), pltpu.roll(head, d, 0))
        acc_head = acc_head + shifted * tap(d)
    out_ref[pl.ds(0, SUBLANES), cols] = acc_head
    hist_ref[:, cols] = u[tt - SUBLANES:tt, :]


def _mix_kernel(alpha, x_token_major,
                x_ref, win_ref, caw_ref, wa_ref, crw_ref, crb_ref, wrg_ref, brga_ref, brgx_ref,
                lam_ref, wr_ref, wo_ref, g1_ref, b1_ref, wrt_hi_ref, wrt_lo_ref, brt_ref,
                xe_ref, route_ref,
                uhist, xrhist, cabuf, xcbuf, hbuf, hcarry, latebuf, vabuf):
    t = pl.program_id(1)
    tt = route_ref.shape[1]

    @pl.when(t == 0)
    def _():
        uhist[...] = jnp.zeros_like(uhist)
        xrhist[...] = jnp.zeros_like(xrhist)
        hcarry[...] = jnp.zeros_like(hcarry)

    x = _load_token_major(x_ref, tt, X_PITCH) if x_token_major else x_ref[...]
    xb = x.astype(_BF)

    def proj(off, width):
        return jnp.dot(xb, win_ref[:, off:off + width], preferred_element_type=_F32)

    def mixer_a_chunk(c):
        cols = slice(c * LATE_CHUNK, (c + 1) * LATE_CHUNK)
        u = proj(OFF_CA + c * LATE_CHUNK, LATE_CHUNK) * proj(OFF_HA + c * LATE_CHUNK, LATE_CHUNK)
        _causal_conv(u, uhist, caw_ref, CONV_A_WIDTH, cabuf, cols=cols)
        vabuf[:, cols] = (proj(OFF_BA + c * LATE_CHUNK, LATE_CHUNK) * cabuf[:, cols]).astype(_BF)

    def late_chunk(j):
        latebuf[:, j * LATE_CHUNK:(j + 1) * LATE_CHUNK] = proj(OFF_YR + j * LATE_CHUNK, LATE_CHUNK)

    def xr_chunk(c):
        cols = slice(c * LATE_CHUNK, (c + 1) * LATE_CHUNK)
        _causal_conv(proj(OFF_XR + c * LATE_CHUNK, LATE_CHUNK), xrhist, crw_ref, CONV_R_WIDTH, xcbuf,
                     cols=cols, bias=crb_ref[:, cols])

    heads_per_chunk = LATE_CHUNK // RG_BLOCK
    z = -lam_ref[...]
    softplus_neg_lam = jnp.maximum(z, 0.0) + jnp.log1p(jnp.exp(-jnp.abs(z)))
    c_half = (-0.5 * RG_C) * softplus_neg_lam
    row = lax.broadcasted_iota(jnp.int32, (tt // SUBLANES, SUBLANES, RG_BLOCK), 1)
    side_work = [functools.partial(mixer_a_chunk, c) for c in range(D_A // LATE_CHUNK)]
    side_work += [functools.partial(late_chunk, j) for j in range(LATE_COLS // LATE_CHUNK)]
    side_cost = [3] * (D_A // LATE_CHUNK) + [1] * (LATE_COLS // LATE_CHUNK)
    per_head = sum(side_cost) / N_RG_BLOCKS
    done, spent = 0, 0.0
    for h in range(N_RG_BLOCKS):
        if h % heads_per_chunk == 0:
            xr_chunk(h // heads_per_chunk)
        while done < len(side_work) and (spent < (h + 1) * per_head or h == N_RG_BLOCKS - 1):
            side_work[done]()
            spent += side_cost[done]
            done += 1
        sl = slice(h * RG_BLOCK, (h + 1) * RG_BLOCK)
        xc = xcbuf[:, sl]
        gh = jnp.dot(xc.astype(_BF), wrg_ref[h], preferred_element_type=_F32)
        t_r = jnp.tanh(gh[:, :RG_BLOCK] + brga_ref[:, sl])
        t_i = jnp.tanh(gh[:, RG_BLOCK:] + brgx_ref[:, sl])
        log_a = c_half[:, sl] * t_r + c_half[:, sl]
        a_t = jnp.exp(log_a)
        b_t = jnp.exp(0.5 * jnp.log(1.0 - a_t * a_t)) * (t_i * xc + xc)
        a3 = a_t.reshape(tt // SUBLANES, SUBLANES, RG_BLOCK)
        b3 = b_t.reshape(tt // SUBLANES, SUBLANES, RG_BLOCK)
        for k in (1, 2, 4):
            keep = row >= k
            b3 = jnp.where(keep, a3 * pltpu.roll(b3, k, 1) + b3, b3)
            a3 = jnp.where(keep, a3 * pltpu.roll(a3, k, 1), a3)
        h_prev = hcarry[:, sl]
        for g in range(tt // SUBLANES):
            hg = a3[g] * h_prev + b3[g]
            hbuf[pl.ds(g * SUBLANES, SUBLANES), sl] = hg
            h_prev = hg[SUBLANES - 1:SUBLANES, :]
        hcarry[:, sl] = h_prev

    ya = jnp.dot(vabuf[...], wa_ref[...], preferred_element_type=_F32)
    v = latebuf[:, :D_R]
    th = jnp.tanh(v * (GELU_K2 * (v * v) + GELU_K1))
    vr = (hbuf[...] * (v * th + v)).astype(_BF)
    yr = jnp.dot(vr, wr_ref[...], preferred_element_type=_F32)

    t_a = jnp.tanh(latebuf[:, OFF_GA - OFF_YR:OFF_GR - OFF_YR])
    t_g = jnp.tanh(latebuf[:, OFF_GR - OFF_YR:])
    merged2 = (t_a * ya + ya) + (t_g * yr + yr)
    o = jnp.dot(merged2.astype(_BF), wo_ref[...], preferred_element_type=_F32)
    x1 = _layer_norm(alpha * x + o, g1_ref[...], b1_ref[...])

    x1_hi = x1.astype(_BF)
    x1_lo = (x1 - x1_hi.astype(_F32)).astype(_BF)
    nt_dims = (((1,), (1,)), ((), ()))
    logits_t = (lax.dot_general(wrt_hi_ref[...], x1_hi, nt_dims, preferred_element_type=_F32)
                + lax.dot_general(wrt_hi_ref[...], x1_lo, nt_dims, preferred_element_type=_F32)
                + lax.dot_general(wrt_lo_ref[...], x1_hi, nt_dims, preferred_element_type=_F32)
                + brt_ref[...])
    cls, g_lo, g_hi = _route(logits_t)
    rsel = lax.broadcasted_iota(jnp.int32, (SUBLANES, tt), 0)
    route_ref[...] = jnp.where(rsel == 0, cls, jnp.where(rsel == 1, g_lo, jnp.where(rsel == 2, g_hi, 0.0)))
    esel = lax.broadcasted_iota(jnp.int32, (EXT, tt), 0)
    ext_t = jnp.where(esel == 0, cls, jnp.where(esel == 1, g_lo, jnp.where(esel == 2, g_hi, 0.0)))
    _store_token_major(xe_ref, jnp.concatenate([x1, ext_t.T], axis=1), XE_PITCH)


def _mix_call(alpha, layer, x2d, x_token_major, n_seq, n_tok, w):
    seq = n_tok // n_seq
    nt = seq // TT

    def layer_const(shape):
        zeros = (0,) * len(shape)
        return pl.BlockSpec((None,) + shape, lambda b, t: (layer,) + zeros, pipeline_mode=pl.Buffered(1))

    def const(shape):
        zeros = (0,) * len(shape)
        return pl.BlockSpec(shape, lambda b, t: zeros, pipeline_mode=pl.Buffered(1))

    in_specs = [
        (pl.BlockSpec((TT * X_PITCH, LANES), lambda b, t: (b * nt + t, 0)) if x_token_major
         else pl.BlockSpec((TT, D_MODEL), lambda b, t: (b * nt + t, 0))),
        layer_const((D_MODEL, D_IN)),
        layer_const((CONV_A_WIDTH, D_A)),
        layer_const((D_A, D_MODEL)),
        layer_const((CONV_R_WIDTH, D_R)),
        layer_const((1, D_R)),
        layer_const((N_RG_BLOCKS, RG_BLOCK, 2 * RG_BLOCK)),
        layer_const((1, D_R)),
        layer_const((1, D_R)),
        layer_const((1, D_R)),
        layer_const((D_R, D_MODEL)),
        layer_const((D_MODEL, D_MODEL)),
        layer_const((1, D_MODEL)),
        layer_const((1, D_MODEL)),
        const((N_EXPERTS, D_MODEL)),
        const((N_EXPERTS, D_MODEL)),
        const((N_EXPERTS, 1)),
    ]
    out_specs = [
        pl.BlockSpec((TT * XE_PITCH, LANES), lambda b, t: (b * nt + t, 0)),
        pl.BlockSpec((SUBLANES, TT), lambda b, t: (0, b * nt + t)),
    ]
    out_shape = [
        jax.ShapeDtypeStruct((n_tok * XE_PITCH, LANES), _F32),
        jax.ShapeDtypeStruct((SUBLANES, n_tok), _F32),
    ]
    scratch = [
        pltpu.VMEM((SUBLANES, D_A), _F32),
        pltpu.VMEM((SUBLANES, D_R), _F32),
        pltpu.VMEM((TT, D_A), _F32),
        pltpu.VMEM((TT, D_R), _F32),
        pltpu.VMEM((TT, D_R), _F32),
        pltpu.VMEM((1, D_R), _F32),
        pltpu.VMEM((TT, LATE_COLS), _F32),
        pltpu.VMEM((TT, D_A), _BF),
    ]
    return pl.pallas_call(
        functools.partial(_mix_kernel, alpha, x_token_major),
        grid=(n_seq, nt),
        in_specs=in_specs,
        out_specs=out_specs,
        out_shape=out_shape,
        scratch_shapes=scratch,
        compiler_params=pltpu.CompilerParams(
            dimension_semantics=("arbitrary", "arbitrary"), vmem_limit_bytes=VMEM_LIMIT),
        name="mix",
    )(x2d, *w)


def _moe_kernel(alpha, n_tiles, n_tok, out_token_major,
                elo_ref, ehi_ref, nv_ref, off_ref, order_ref,
                xe_hbm,
                wg_lo, wu_lo, wd_lo, wg_hi, wu_hi, wd_hi, g2_ref, b2_ref,
                out_hbm,
                gbuf0, gbuf1, gbuf2, obuf0, obuf1, obuf2, gsem, ssem, pend):
    gbuf = (gbuf0, gbuf1, gbuf2)
    obuf = (obuf0, obuf1, obuf2)
    i = pl.program_id(0)
    nv = nv_ref[i]
    o_pitch = X_PITCH if out_token_major else 1
    defer_scatter = out_token_major

    def gather_copy(tok, r, s):
        return pltpu.make_async_copy(xe_hbm.at[pl.ds(tok * XE_PITCH, XE_PITCH), :],
                                     gbuf[s].at[pl.ds(r * XE_PITCH, XE_PITCH), :], gsem.at[s])

    def scatter_copy(tok, r, s, n_rows=1, group=None):
        dst = out_hbm.at[pl.ds(tok * o_pitch, n_rows * o_pitch), :]
        if out_token_major:
            src = r * o_pitch if isinstance(r, int) else pl.multiple_of(r * o_pitch, o_pitch)
            return pltpu.make_async_copy(obuf[s].at[pl.ds(src, n_rows * o_pitch), :], dst, ssem.at[s])
        if group is None:
            group, r = lax.shift_right_logical(r, 3), jnp.bitwise_and(r, SUBLANES - 1)
        return pltpu.make_async_copy(obuf[s].at[group, pl.ds(r, n_rows), :], dst, ssem.at[s])

    def scatter_wait_copy(s, n_rows):
        if out_token_major:
            return scatter_copy(0, 0, s, n_rows=n_rows)
        if n_rows < SUBLANES:
            block = obuf[s].at[0, pl.ds(0, n_rows), :]
        else:
            block = obuf[s].at[pl.ds(0, n_rows // SUBLANES)]
        return pltpu.make_async_copy(block, block, ssem.at[s])

    def start_gather(tile, s):
        base = off_ref[tile]
        for r in range(TM):
            gather_copy(order_ref[base + r], r, s).start(priority=r % 2)

    def wait_gather(s):
        pltpu.make_async_copy(xe_hbm.at[pl.ds(0, TM * XE_PITCH), :], gbuf[s], gsem.at[s]).wait()

    def start_scatter(tile, s, n_rows):
        base = off_ref[tile]
        n_chunks = lax.shift_right_logical(n_rows, DMA_UNROLL.bit_length() - 1)

        def chunk(c, carry):
            r0 = c * DMA_UNROLL
            for j in range(DMA_UNROLL):
                if out_token_major:
                    copy = scatter_copy(order_ref[base + r0 + j], r0 + j, s)
                else:
                    copy = scatter_copy(order_ref[base + r0 + j], j, s, group=c)
                copy.start(priority=j % 2)
            return carry
        lax.fori_loop(0, n_chunks, chunk, 0)

        def tail(r, carry):
            scatter_copy(order_ref[base + r], r, s).start()
            return carry
        lax.fori_loop(n_chunks * DMA_UNROLL, n_rows, tail, 0)

    def start_scatter_full(tile, n_valid, s):
        base = off_ref[tile]
        for r in range(TM):
            tok = jnp.where(r < n_valid, order_ref[base + r], n_tok + s * TM + r)
            scatter_copy(tok, r, s).start(priority=r % 2)
        pend[s] = TM

    def wait_scatter(s):
        n_rows = pend[s]
        width = TM
        while width >= (TM if defer_scatter else 1):
            @pl.when(jnp.bitwise_and(n_rows, width) != 0)
            def _(width=width):
                scatter_wait_copy(s, width).wait()
            width //= 2
        pend[s] = 0

    prev = jnp.maximum(i - 1, 0)
    nv_prev = jnp.where(i > 0, nv_ref[prev], 0)

    @pl.when(i == 0)
    def _():
        for s in range(N_SLOTS):
            pend[s] = 0
        for ahead in range(GATHER_AHEAD):
            start_gather(ahead, ahead)
        if defer_scatter:
            obuf[N_SLOTS - 1][...] = jnp.zeros_like(obuf[N_SLOTS - 1])
            for s in range(N_SLOTS):
                fill = scatter_copy(n_tok + s * TM, 0, N_SLOTS - 1, n_rows=TM)
                fill.start()
                fill.wait()

    def tile_body(s):
        wait_gather(s)
        wait_scatter(s)
        xf = _load_token_major(gbuf[s], TM, XE_PITCH)
        ext = gbuf[s][pl.ds(D_MODEL // LANES, TM, stride=XE_PITCH), :]
        gate_lo = ext[:, 1:2]
        gate_hi = ext[:, 2:3]
        xb = xf.astype(_BF)
        start_gather(i + GATHER_AHEAD, (s + GATHER_AHEAD) % N_SLOTS)
        if defer_scatter:
            start_scatter_full(prev, nv_prev, (s + N_SLOTS - 1) % N_SLOTS)

        def expert(wg, wu, wd, gate):
            a2 = jnp.dot(xb, wg[...], preferred_element_type=_F32)
            u = jnp.dot(xb, wu[...], preferred_element_type=_F32)
            hgt = ((a2 * jnp.tanh(a2) + a2) * (u * gate)).astype(_BF)
            return jnp.dot(hgt, wd[...], preferred_element_type=_F32)

        moe = expert(wg_lo, wu_lo, wd_lo, gate_lo) + expert(wg_hi, wu_hi, wd_hi, gate_hi)
        y = _layer_norm(alpha * xf + moe, g2_ref[...], b2_ref[...])
        if out_token_major:
            _store_token_major(obuf[s], y, X_PITCH)
        else:
            obuf[s][...] = y.reshape(TM // SUBLANES, SUBLANES, D_MODEL)
        if not defer_scatter:
            start_scatter(i, s, nv)
            pend[s] = nv

    slot = lax.rem(i, N_SLOTS)
    issuer = jnp.maximum(i - GATHER_AHEAD, 0)
    gathered_unused = jnp.logical_and(nv == 0, jnp.logical_or(i < GATHER_AHEAD, nv_ref[issuer] > 0))
    for s in range(N_SLOTS):
        @pl.when(jnp.logical_and(nv > 0, slot == s))
        def _(s=s):
            tile_body(s)

        @pl.when(jnp.logical_and(gathered_unused, slot == s))
        def _(s=s):
            wait_gather(s)

        if defer_scatter:
            @pl.when(jnp.logical_and(jnp.logical_and(nv == 0, nv_prev > 0), slot == s))
            def _(s=s):
                start_scatter_full(prev, nv_prev, (s + N_SLOTS - 1) % N_SLOTS)

    @pl.when(i == n_tiles - 1)
    def _():
        for s in range(N_SLOTS):
            wait_scatter(s)


def _moe_call(alpha, layer, xe, plan, w, n_tok, out_token_major):
    elo, ehi, nvalid, off, order = plan
    n_tiles = elo.shape[0] - GATHER_AHEAD
    wg, wu, wd, g2, b2 = w
    if out_token_major:
        obuf_shape, out_rows, out_cols = (TM * X_PITCH, LANES), (n_tok + N_SLOTS * TM) * X_PITCH, LANES
    else:
        obuf_shape, out_rows, out_cols = (TM // SUBLANES, SUBLANES, D_MODEL), n_tok, D_MODEL

    def wspec(shape, which):
        def imap(i, elo_r, ehi_r, *_):
            return (layer, (elo_r, ehi_r)[which][i], 0, 0)
        return pl.BlockSpec((None, None) + shape, imap)

    grid_spec = pltpu.PrefetchScalarGridSpec(
        num_scalar_prefetch=5,
        grid=(n_tiles,),
        in_specs=[
            pl.BlockSpec(memory_space=pl.ANY),
            wspec((D_MODEL, D_FF), 0), wspec((D_MODEL, D_FF), 0), wspec((D_FF, D_MODEL), 0),
            wspec((D_MODEL, D_FF), 1), wspec((D_MODEL, D_FF), 1), wspec((D_FF, D_MODEL), 1),
            pl.BlockSpec((None, 1, D_MODEL), lambda i, *_: (layer, 0, 0)),
            pl.BlockSpec((None, 1, D_MODEL), lambda i, *_: (layer, 0, 0)),
        ],
        out_specs=pl.BlockSpec(memory_space=pl.ANY),
        scratch_shapes=[
            *[pltpu.VMEM((TM * XE_PITCH, LANES), _F32) for _ in range(N_SLOTS)],
            *[pltpu.VMEM(obuf_shape, _F32) for _ in range(N_SLOTS)],
            pltpu.SemaphoreType.DMA((N_SLOTS,)),
            pltpu.SemaphoreType.DMA((N_SLOTS,)),
            pltpu.SMEM((N_SLOTS,), jnp.int32),
        ],
    )
    return pl.pallas_call(
        functools.partial(_moe_kernel, alpha, n_tiles, n_tok, out_token_major),
        grid_spec=grid_spec,
        out_shape=jax.ShapeDtypeStruct((out_rows, out_cols), _F32),
        compiler_params=pltpu.CompilerParams(
            dimension_semantics=("arbitrary",), vmem_limit_bytes=VMEM_LIMIT),
        name="moe",
    )(elo, ehi, nvalid, off, order, xe, wg, wu, wd, wg, wu, wd, g2, b2)


_PAIR_LO = (0, 0, 0, 1, 1, 2)
_PAIR_HI = (1, 2, 3, 2, 3, 3)


def _plan(cls, n_tiles):
    n_tok = cls.shape[0]
    tok = jnp.arange(n_tok, dtype=jnp.int32)
    order = jnp.sort(cls * n_tok + tok) % n_tok
    order = jnp.concatenate([order, jnp.zeros((TM,), jnp.int32)])
    counts = jnp.sum((cls[:, None] == jnp.arange(N_CLASSES, dtype=jnp.int32)[None, :]).astype(jnp.int32), axis=0)
    cstart = jnp.cumsum(counts) - counts
    tiles = (counts + TM - 1) // TM
    tile_end = jnp.cumsum(tiles)
    tile_start = tile_end - tiles
    ti = jnp.arange(n_tiles, dtype=jnp.int32)
    n_used = tile_end[-1]
    tsel = jnp.minimum(ti, n_used - 1)
    tcls = jnp.sum((tile_end[None, :] <= tsel[:, None]).astype(jnp.int32), axis=1)
    within = (tsel - tile_start[tcls]) * TM
    nvalid = jnp.where(ti < n_used, jnp.clip(counts[tcls] - within, 0, TM), 0).astype(jnp.int32)
    off = (cstart[tcls] + within).astype(jnp.int32)
    grp = tcls // N_PAIRS
    pair = tcls % N_PAIRS
    elo = grp * EXPERTS_PER_GROUP + jnp.asarray(_PAIR_LO, jnp.int32)[pair]
    ehi = grp * EXPERTS_PER_GROUP + jnp.asarray(_PAIR_HI, jnp.int32)[pair]
    return elo.astype(jnp.int32), ehi.astype(jnp.int32), nvalid, off, order


def kernel(x, w_in, conv_a_w, w_branch_a, conv_r_w, conv_r_b, w_rg_a, b_rg_a, w_rg_x, b_rg_x, rg_lambda, w_branch_r, w_out, ln1_g, ln1_b, w_router, b_router, w_exp_gate, w_exp_up, w_exp_down, ln2_g, ln2_b):
    n_seq, seq, _ = x.shape
    depth = w_in.shape[0]
    n_tok = n_seq * seq
    assert seq % TT == 0 and n_tok % TM == 0
    n_tiles = (n_tok + N_CLASSES * (TM - 1)) // TM + GATHER_AHEAD
    alpha = (2.0 * depth) ** 0.25

    wrt = w_router.T.astype(_F32)
    wrt_hi = wrt.astype(_BF)
    wrt_lo = (wrt - wrt_hi.astype(_F32)).astype(_BF)
    brt = b_router.astype(_F32).reshape(N_EXPERTS, 1)

    in_scale = jnp.concatenate([jnp.ones((OFF_GA,), _F32), jnp.full((D_IN - OFF_GA,), 0.5, _F32)])
    mix_w = (
        (w_in * in_scale).astype(_BF),
        conv_a_w,
        w_branch_a.astype(_BF),
        conv_r_w,
        conv_r_b.reshape(depth, 1, D_R),
        (0.5 * jnp.concatenate([w_rg_a, w_rg_x], axis=-1)).astype(_BF),
        0.5 * b_rg_a.reshape(depth, 1, D_R),
        0.5 * b_rg_x.reshape(depth, 1, D_R),
        rg_lambda.reshape(depth, 1, D_R),
        (0.25 * w_branch_r).astype(_BF),
        (0.5 * w_out).astype(_BF),
        ln1_g.reshape(depth, 1, D_MODEL),
        ln1_b.reshape(depth, 1, D_MODEL),
        wrt_hi, wrt_lo, brt,
    )
    moe_w = (
        (0.5 * w_exp_gate).astype(_BF), w_exp_up.astype(_BF), w_exp_down.astype(_BF),
        ln2_g.reshape(depth, 1, D_MODEL), ln2_b.reshape(depth, 1, D_MODEL),
    )

    h = x.reshape(n_tok, D_MODEL)
    for l in range(depth):
        xe, route = _mix_call(alpha, l, h, l > 0, n_seq, n_tok, mix_w)
        plan = _plan(route[0].astype(jnp.int32), n_tiles + GATHER_AHEAD)
        h = _moe_call(alpha, l, xe, plan, moe_w, n_tok, l < depth - 1)
    return h.reshape(n_seq, seq, D_MODEL)
```

```python
import functools

import jax
import jax.numpy as jnp
from jax import lax
from jax.experimental import pallas as pl
from jax.experimental.pallas import tpu as pltpu

D_MODEL = 1024
D_A = 1024
D_R = 1280
N_RG_BLOCKS = 10
RG_BLOCK = 128
RG_C = 8.0
CONV_A_WIDTH = 3
CONV_R_WIDTH = 4
D_IN = 3 * D_A + 2 * D_R + 2 * D_MODEL
N_EXPERTS = 16
N_GROUPS = 4
EXPERTS_PER_GROUP = 4
N_PAIRS = 6
N_CLASSES = N_GROUPS * N_PAIRS
D_FF = 512
LN_EPS = 1e-5

OFF_HA, OFF_BA, OFF_CA = 0, D_A, 2 * D_A
OFF_XR = 3 * D_A
OFF_YR = OFF_XR + D_R
OFF_GA = OFF_YR + D_R
OFF_GR = OFF_GA + D_MODEL
LATE_COLS = D_IN - OFF_YR
LATE_CHUNK = 256

SUBLANES = 8
LANES = 128
EXT = LANES
X_PITCH = D_MODEL // LANES
XE_PITCH = (D_MODEL + EXT) // LANES
TT = 512
TM = 256
DMA_UNROLL = 8
GATHER_AHEAD = 2
N_SLOTS = GATHER_AHEAD + 1
VMEM_LIMIT = 56 * 1024 * 1024

_BF = jnp.bfloat16
_F32 = jnp.float32

GELU_K1 = 0.7978845608028654
GELU_K2 = GELU_K1 * 0.044715


def _layer_norm(y, g, b):
    mu = jnp.mean(y, axis=-1, keepdims=True)
    yc = y - mu
    var = jnp.mean(yc * yc, axis=-1, keepdims=True)
    return yc * lax.rsqrt(var + LN_EPS) * g + b


def _first_index_of(vals, target):
    idx = jnp.full(target.shape, float(len(vals) - 1), _F32)
    for j in range(len(vals) - 2, -1, -1):
        idx = jnp.where(vals[j] == target, float(j), idx)
    return idx


def _route(logits_t):
    m = jnp.max(logits_t, axis=0, keepdims=True)
    e = jnp.exp(logits_t - m)
    p = e / jnp.sum(e, axis=0, keepdims=True)
    neg = jnp.full((1, logits_t.shape[1]), -jnp.inf, _F32)
    scores, m1s, i1s, m2s, i2s = [], [], [], [], []
    for g in range(N_GROUPS):
        v = [p[g * EXPERTS_PER_GROUP + j:g * EXPERTS_PER_GROUP + j + 1, :] for j in range(EXPERTS_PER_GROUP)]
        m1 = jnp.maximum(jnp.maximum(v[0], v[1]), jnp.maximum(v[2], v[3]))
        i1 = _first_index_of(v, m1)
        w = [jnp.where(i1 == float(j), neg, v[j]) for j in range(EXPERTS_PER_GROUP)]
        m2 = jnp.maximum(jnp.maximum(w[0], w[1]), jnp.maximum(w[2], w[3]))
        i2 = _first_index_of(w, m2)
        scores.append(m1 + m2)
        m1s.append(m1); i1s.append(i1); m2s.append(m2); i2s.append(i2)
    best = jnp.maximum(jnp.maximum(scores[0], scores[1]), jnp.maximum(scores[2], scores[3]))
    gsel = _first_index_of(scores, best)

    def pick(xs):
        out = xs[N_GROUPS - 1]
        for g in range(N_GROUPS - 2, -1, -1):
            out = jnp.where(gsel == float(g), xs[g], out)
        return out

    m1, i1, m2, i2 = pick(m1s), pick(i1s), pick(m2s), pick(i2s)
    den = m1 + m2
    gate1, gate2 = m1 / den, m2 / den
    first_is_lo = i1 < i2
    lo = jnp.where(first_is_lo, i1, i2)
    hi = jnp.where(first_is_lo, i2, i1)
    g_lo = jnp.where(first_is_lo, gate1, gate2)
    g_hi = jnp.where(first_is_lo, gate2, gate1)
    pair = lo * (7.0 - lo) * 0.5 + (hi - lo - 1.0)
    cls = gsel * float(N_PAIRS) + pair
    return cls, g_lo, g_hi


def _load_token_major(ref, n_rows, pitch, lead=()):
    cols = [ref[lead + (pl.ds(c, n_rows, stride=pitch), slice(None))] for c in range(D_MODEL // LANES)]
    return jnp.concatenate(cols, axis=1)


def _store_token_major(ref, val, pitch, lead=()):
    n_rows = val.shape[0]
    for c in range(val.shape[1] // LANES):
        ref[lead + (pl.ds(c, n_rows, stride=pitch), slice(None))] = val[:, c * LANES:(c + 1) * LANES]


def _causal_conv(u, hist_ref, w_ref, n_taps, out_ref, cols=slice(None), bias=None):
    tt = u.shape[0]

    def tap(d):
        return w_ref[n_taps - 1 - d:n_taps - d, cols]

    def tap0(val):
        out = val * tap(0)
        return out if bias is None else out + bias

    acc = tap0(u)
    for d in range(1, n_taps):
        acc = acc + pltpu.roll(u, d, 0) * tap(d)
    out_ref[:, cols] = acc
    head = u[0:SUBLANES, :]
    hist = hist_ref[:, cols]
    row = lax.broadcasted_iota(jnp.int32, head.shape, 0)
    acc_head = tap0(head)
    for d in range(1, n_taps):
        shifted = jnp.where(row < d, pltpu.roll(hist, d, 0), pltpu.roll(head, d, 0))
        acc_head = acc_head + shifted * tap(d)
    out_ref[pl.ds(0, SUBLANES), cols] = acc_head
    hist_ref[:, cols] = u[tt - SUBLANES:tt, :]


def _mix_kernel(alpha, x_token_major,
                x_ref, win_ref, caw_ref, wa_ref, crw_ref, crb_ref, wrg_ref, brga_ref, brgx_ref,
                lam_ref, wr_ref, wo_ref, g1_ref, b1_ref, wrt_hi_ref, wrt_lo_ref, brt_ref,
                wg_f32, wu_f32, wd_f32,
                xe_ref, route_ref, wg_bf, wu_bf, wd_bf,
                uhist, xrhist, cabuf, xcbuf, hbuf, hcarry, latebuf, vabuf):
    t = pl.program_id(1)
    tt = route_ref.shape[1]

    wg_bf[...] = (0.5 * wg_f32[...]).astype(_BF)
    wu_bf[...] = wu_f32[...].astype(_BF)
    wd_bf[...] = wd_f32[...].astype(_BF)

    @pl.when(t == 0)
    def _():
        uhist[...] = jnp.zeros_like(uhist)
        xrhist[...] = jnp.zeros_like(xrhist)
        hcarry[...] = jnp.zeros_like(hcarry)

    x = _load_token_major(x_ref, tt, X_PITCH) if x_token_major else x_ref[...]
    xb = x.astype(_BF)

    def proj(off, width):
        return jnp.dot(xb, win_ref[:, off:off + width], preferred_element_type=_F32)

    def mixer_a_chunk(c):
        cols = slice(c * LATE_CHUNK, (c + 1) * LATE_CHUNK)
        u = proj(OFF_CA + c * LATE_CHUNK, LATE_CHUNK) * proj(OFF_HA + c * LATE_CHUNK, LATE_CHUNK)
        _causal_conv(u, uhist, caw_ref, CONV_A_WIDTH, cabuf, cols=cols)
        vabuf[:, cols] = (proj(OFF_BA + c * LATE_CHUNK, LATE_CHUNK) * cabuf[:, cols]).astype(_BF)

    def late_chunk(j):
        latebuf[:, j * LATE_CHUNK:(j + 1) * LATE_CHUNK] = proj(OFF_YR + j * LATE_CHUNK, LATE_CHUNK)

    def xr_chunk(c):
        cols = slice(c * LATE_CHUNK, (c + 1) * LATE_CHUNK)
        _causal_conv(proj(OFF_XR + c * LATE_CHUNK, LATE_CHUNK), xrhist, crw_ref, CONV_R_WIDTH, xcbuf,
                     cols=cols, bias=crb_ref[:, cols])

    heads_per_chunk = LATE_CHUNK // RG_BLOCK
    z = -lam_ref[...]
    softplus_neg_lam = jnp.maximum(z, 0.0) + jnp.log1p(jnp.exp(-jnp.abs(z)))
    c_half = (-0.5 * RG_C) * softplus_neg_lam
    row = lax.broadcasted_iota(jnp.int32, (tt // SUBLANES, SUBLANES, RG_BLOCK), 1)
    side_work = [functools.partial(mixer_a_chunk, c) for c in range(D_A // LATE_CHUNK)]
    side_work += [functools.partial(late_chunk, j) for j in range(LATE_COLS // LATE_CHUNK)]
    side_cost = [3] * (D_A // LATE_CHUNK) + [1] * (LATE_COLS // LATE_CHUNK)
    per_head = sum(side_cost) / N_RG_BLOCKS
    done, spent = 0, 0.0
    for h in range(N_RG_BLOCKS):
        if h % heads_per_chunk == 0:
            xr_chunk(h // heads_per_chunk)
        while done < len(side_work) and (spent < (h + 1) * per_head or h == N_RG_BLOCKS - 1):
            side_work[done]()
            spent += side_cost[done]
            done += 1
        sl = slice(h * RG_BLOCK, (h + 1) * RG_BLOCK)
        xc = xcbuf[:, sl]
        gh = jnp.dot(xc.astype(_BF), wrg_ref[h], preferred_element_type=_F32)
        t_r = jnp.tanh(gh[:, :RG_BLOCK] + brga_ref[:, sl])
        t_i = jnp.tanh(gh[:, RG_BLOCK:] + brgx_ref[:, sl])
        log_a = c_half[:, sl] * t_r + c_half[:, sl]
        a_t = jnp.exp(log_a)
        b_t = jnp.exp(0.5 * jnp.log(1.0 - a_t * a_t)) * (t_i * xc + xc)
        a3 = a_t.reshape(tt // SUBLANES, SUBLANES, RG_BLOCK)
        b3 = b_t.reshape(tt // SUBLANES, SUBLANES, RG_BLOCK)
        for k in (1, 2, 4):
            keep = row >= k
            b3 = jnp.where(keep, a3 * pltpu.roll(b3, k, 1) + b3, b3)
            a3 = jnp.where(keep, a3 * pltpu.roll(a3, k, 1), a3)
        h_prev = hcarry[:, sl]
        for g in range(tt // SUBLANES):
            hg = a3[g] * h_prev + b3[g]
            hbuf[pl.ds(g * SUBLANES, SUBLANES), sl] = hg
            h_prev = hg[SUBLANES - 1:SUBLANES, :]
        hcarry[:, sl] = h_prev

    ya = jnp.dot(vabuf[...], wa_ref[...], preferred_element_type=_F32)
    v = latebuf[:, :D_R]
    th = jnp.tanh(v * (GELU_K2 * (v * v) + GELU_K1))
    vr = (hbuf[...] * (v * th + v)).astype(_BF)
    yr = jnp.dot(vr, wr_ref[...], preferred_element_type=_F32)

    t_a = jnp.tanh(latebuf[:, OFF_GA - OFF_YR:OFF_GR - OFF_YR])
    t_g = jnp.tanh(latebuf[:, OFF_GR - OFF_YR:])
    merged2 = (t_a * ya + ya) + (t_g * yr + yr)
    o = jnp.dot(merged2.astype(_BF), wo_ref[...], preferred_element_type=_F32)
    x1 = _layer_norm(alpha * x + o, g1_ref[...], b1_ref[...])

    x1_hi = x1.astype(_BF)
    x1_lo = (x1 - x1_hi.astype(_F32)).astype(_BF)
    nt_dims = (((1,), (1,)), ((), ()))
    logits_t = (lax.dot_general(wrt_hi_ref[...], x1_hi, nt_dims, preferred_element_type=_F32)
                + lax.dot_general(wrt_hi_ref[...], x1_lo, nt_dims, preferred_element_type=_F32)
                + lax.dot_general(wrt_lo_ref[...], x1_hi, nt_dims, preferred_element_type=_F32)
                + brt_ref[...])
    cls, g_lo, g_hi = _route(logits_t)
    rsel = lax.broadcasted_iota(jnp.int32, (SUBLANES, tt), 0)
    route_ref[...] = jnp.where(rsel == 0, cls, jnp.where(rsel == 1, g_lo, jnp.where(rsel == 2, g_hi, 0.0)))
    esel = lax.broadcasted_iota(jnp.int32, (EXT, tt), 0)
    ext_t = jnp.where(esel == 0, cls, jnp.where(esel == 1, g_lo, jnp.where(esel == 2, g_hi, 0.0)))
    _store_token_major(xe_ref, jnp.concatenate([x1, ext_t.T], axis=1), XE_PITCH)


def _mix_call(alpha, layer, x2d, x_token_major, n_seq, n_tok, w):
    seq = n_tok // n_seq
    nt = seq // TT

    def layer_const(shape):
        zeros = (0,) * len(shape)
        return pl.BlockSpec((None,) + shape, lambda b, t: (layer,) + zeros, pipeline_mode=pl.Buffered(1))

    def const(shape):
        zeros = (0,) * len(shape)
        return pl.BlockSpec(shape, lambda b, t: zeros, pipeline_mode=pl.Buffered(1))

    in_specs = [
        (pl.BlockSpec((TT * X_PITCH, LANES), lambda b, t: (b * nt + t, 0)) if x_token_major
         else pl.BlockSpec((TT, D_MODEL), lambda b, t: (b * nt + t, 0))),
        layer_const((D_MODEL, D_IN)),
        layer_const((CONV_A_WIDTH, D_A)),
        layer_const((D_A, D_MODEL)),
        layer_const((CONV_R_WIDTH, D_R)),
        layer_const((1, D_R)),
        layer_const((N_RG_BLOCKS, RG_BLOCK, 2 * RG_BLOCK)),
        layer_const((1, D_R)),
        layer_const((1, D_R)),
        layer_const((1, D_R)),
        layer_const((D_R, D_MODEL)),
        layer_const((D_MODEL, D_MODEL)),
        layer_const((1, D_MODEL)),
        layer_const((1, D_MODEL)),
        const((N_EXPERTS, D_MODEL)),
        const((N_EXPERTS, D_MODEL)),
        const((N_EXPERTS, 1)),
    ]
    n_steps = n_seq * nt
    slab_specs, slab_shapes = [], []
    for rows, cols in ((N_EXPERTS * D_MODEL, D_FF), (N_EXPERTS * D_MODEL, D_FF), (N_EXPERTS * D_FF, D_MODEL)):
        slab = rows // n_steps
        assert slab * n_steps == rows and slab % (2 * SUBLANES) == 0
        in_specs.append(pl.BlockSpec((None, slab, cols), lambda b, t: (layer, b * nt + t, 0)))
        slab_specs.append(pl.BlockSpec((slab, cols), lambda b, t: (b * nt + t, 0)))
        slab_shapes.append(jax.ShapeDtypeStruct((rows, cols), _BF))
    out_specs = [
        pl.BlockSpec((TT * XE_PITCH, LANES), lambda b, t: (b * nt + t, 0)),
        pl.BlockSpec((SUBLANES, TT), lambda b, t: (0, b * nt + t)),
    ] + slab_specs
    out_shape = [
        jax.ShapeDtypeStruct((n_tok * XE_PITCH, LANES), _F32),
        jax.ShapeDtypeStruct((SUBLANES, n_tok), _F32),
    ] + slab_shapes
    scratch = [
        pltpu.VMEM((SUBLANES, D_A), _F32),
        pltpu.VMEM((SUBLANES, D_R), _F32),
        pltpu.VMEM((TT, D_A), _F32),
        pltpu.VMEM((TT, D_R), _F32),
        pltpu.VMEM((TT, D_R), _F32),
        pltpu.VMEM((1, D_R), _F32),
        pltpu.VMEM((TT, LATE_COLS), _F32),
        pltpu.VMEM((TT, D_A), _BF),
    ]
    return pl.pallas_call(
        functools.partial(_mix_kernel, alpha, x_token_major),
        grid=(n_seq, nt),
        in_specs=in_specs,
        out_specs=out_specs,
        out_shape=out_shape,
        scratch_shapes=scratch,
        compiler_params=pltpu.CompilerParams(
            dimension_semantics=("arbitrary", "arbitrary"), vmem_limit_bytes=VMEM_LIMIT),
        name="mix",
    )(x2d, *w)


def _moe_kernel(alpha, n_tiles, n_tok, out_token_major,
                elo_ref, ehi_ref, nv_ref, off_ref, order_ref,
                xe_hbm,
                wg_lo, wu_lo, wd_lo, wg_hi, wu_hi, wd_hi, g2_ref, b2_ref,
                out_hbm,
                gbuf0, gbuf1, gbuf2, obuf0, obuf1, obuf2, gsem, ssem, pend):
    gbuf = (gbuf0, gbuf1, gbuf2)
    obuf = (obuf0, obuf1, obuf2)
    i = pl.program_id(0)
    nv = nv_ref[i]
    o_pitch = X_PITCH if out_token_major else 1
    defer_scatter = out_token_major

    def gather_copy(tok, r, s):
        return pltpu.make_async_copy(xe_hbm.at[pl.ds(tok * XE_PITCH, XE_PITCH), :],
                                     gbuf[s].at[pl.ds(r * XE_PITCH, XE_PITCH), :], gsem.at[s])

    def scatter_copy(tok, r, s, n_rows=1, group=None):
        dst = out_hbm.at[pl.ds(tok * o_pitch, n_rows * o_pitch), :]
        if out_token_major:
            src = r * o_pitch if isinstance(r, int) else pl.multiple_of(r * o_pitch, o_pitch)
            return pltpu.make_async_copy(obuf[s].at[pl.ds(src, n_rows * o_pitch), :], dst, ssem.at[s])
        if group is None:
            group, r = lax.shift_right_logical(r, 3), jnp.bitwise_and(r, SUBLANES - 1)
        return pltpu.make_async_copy(obuf[s].at[group, pl.ds(r, n_rows), :], dst, ssem.at[s])

    def scatter_wait_copy(s, n_rows):
        if out_token_major:
            return scatter_copy(0, 0, s, n_rows=n_rows)
        if n_rows < SUBLANES:
            block = obuf[s].at[0, pl.ds(0, n_rows), :]
        else:
            block = obuf[s].at[pl.ds(0, n_rows // SUBLANES)]
        return pltpu.make_async_copy(block, block, ssem.at[s])

    def start_gather(tile, s):
        base = off_ref[tile]
        for r in range(TM):
            gather_copy(order_ref[base + r], r, s).start(priority=r % 2)

    def wait_gather(s):
        pltpu.make_async_copy(xe_hbm.at[pl.ds(0, TM * XE_PITCH), :], gbuf[s], gsem.at[s]).wait()

    def start_scatter(tile, s, n_rows):
        base = off_ref[tile]
        n_chunks = lax.shift_right_logical(n_rows, DMA_UNROLL.bit_length() - 1)

        def chunk(c, carry):
            r0 = c * DMA_UNROLL
            for j in range(DMA_UNROLL):
                if out_token_major:
                    copy = scatter_copy(order_ref[base + r0 + j], r0 + j, s)
                else:
                    copy = scatter_copy(order_ref[base + r0 + j], j, s, group=c)
                copy.start(priority=j % 2)
            return carry
        lax.fori_loop(0, n_chunks, chunk, 0)

        def tail(r, carry):
            scatter_copy(order_ref[base + r], r, s).start()
            return carry
        lax.fori_loop(n_chunks * DMA_UNROLL, n_rows, tail, 0)

    def start_scatter_full(tile, n_valid, s):
        base = off_ref[tile]
        for r in range(TM):
            tok = jnp.where(r < n_valid, order_ref[base + r], n_tok + s * TM + r)
            scatter_copy(tok, r, s).start(priority=r % 2)
        pend[s] = TM

    def wait_scatter(s):
        n_rows = pend[s]
        width = TM
        while width >= (TM if defer_scatter else 1):
            @pl.when(jnp.bitwise_and(n_rows, width) != 0)
            def _(width=width):
                scatter_wait_copy(s, width).wait()
            width //= 2
        pend[s] = 0

    prev = jnp.maximum(i - 1, 0)
    nv_prev = jnp.where(i > 0, nv_ref[prev], 0)

    @pl.when(i == 0)
    def _():
        for s in range(N_SLOTS):
            pend[s] = 0
        for ahead in range(GATHER_AHEAD):
            start_gather(ahead, ahead)
        if defer_scatter:
            obuf[N_SLOTS - 1][...] = jnp.zeros_like(obuf[N_SLOTS - 1])
            for s in range(N_SLOTS):
                fill = scatter_copy(n_tok + s * TM, 0, N_SLOTS - 1, n_rows=TM)
                fill.start()
                fill.wait()

    def tile_body(s):
        wait_gather(s)
        wait_scatter(s)
        xf = _load_token_major(gbuf[s], TM, XE_PITCH)
        ext = gbuf[s][pl.ds(D_MODEL // LANES, TM, stride=XE_PITCH), :]
        gate_lo = ext[:, 1:2]
        gate_hi = ext[:, 2:3]
        xb = xf.astype(_BF)
        start_gather(i + GATHER_AHEAD, (s + GATHER_AHEAD) % N_SLOTS)
        if defer_scatter:
            start_scatter_full(prev, nv_prev, (s + N_SLOTS - 1) % N_SLOTS)

        def expert(wg, wu, wd, gate):
            a2 = jnp.dot(xb, wg[...], preferred_element_type=_F32)
            u = jnp.dot(xb, wu[...], preferred_element_type=_F32)
            hgt = ((a2 * jnp.tanh(a2) + a2) * (u * gate)).astype(_BF)
            return jnp.dot(hgt, wd[...], preferred_element_type=_F32)

        moe = expert(wg_lo, wu_lo, wd_lo, gate_lo) + expert(wg_hi, wu_hi, wd_hi, gate_hi)
        y = _layer_norm(alpha * xf + moe, g2_ref[...], b2_ref[...])
        if out_token_major:
            _store_token_major(obuf[s], y, X_PITCH)
        else:
            obuf[s][...] = y.reshape(TM // SUBLANES, SUBLANES, D_MODEL)
        if not defer_scatter:
            start_scatter(i, s, nv)
            pend[s] = nv

    slot = lax.rem(i, N_SLOTS)
    issuer = jnp.maximum(i - GATHER_AHEAD, 0)
    gathered_unused = jnp.logical_and(nv == 0, jnp.logical_or(i < GATHER_AHEAD, nv_ref[issuer] > 0))
    for s in range(N_SLOTS):
        @pl.when(jnp.logical_and(nv > 0, slot == s))
        def _(s=s):
            tile_body(s)

        @pl.when(jnp.logical_and(gathered_unused, slot == s))
        def _(s=s):
            wait_gather(s)

        if defer_scatter:
            @pl.when(jnp.logical_and(jnp.logical_and(nv == 0, nv_prev > 0), slot == s))
            def _(s=s):
                start_scatter_full(prev, nv_prev, (s + N_SLOTS - 1) % N_SLOTS)

    @pl.when(i == n_tiles - 1)
    def _():
        for s in range(N_SLOTS):
            wait_scatter(s)


def _moe_call(alpha, layer, xe, plan, w, n_tok, out_token_major):
    elo, ehi, nvalid, off, order = plan
    n_tiles = elo.shape[0] - GATHER_AHEAD
    wg, wu, wd, g2, b2 = w
    if out_token_major:
        obuf_shape, out_rows, out_cols = (TM * X_PITCH, LANES), (n_tok + N_SLOTS * TM) * X_PITCH, LANES
    else:
        obuf_shape, out_rows, out_cols = (TM // SUBLANES, SUBLANES, D_MODEL), n_tok, D_MODEL

    def wspec(shape, which):
        def imap(i, elo_r, ehi_r, *_):
            return ((elo_r, ehi_r)[which][i], 0, 0)
        return pl.BlockSpec((None,) + shape, imap)

    grid_spec = pltpu.PrefetchScalarGridSpec(
        num_scalar_prefetch=5,
        grid=(n_tiles,),
        in_specs=[
            pl.BlockSpec(memory_space=pl.ANY),
            wspec((D_MODEL, D_FF), 0), wspec((D_MODEL, D_FF), 0), wspec((D_FF, D_MODEL), 0),
            wspec((D_MODEL, D_FF), 1), wspec((D_MODEL, D_FF), 1), wspec((D_FF, D_MODEL), 1),
            pl.BlockSpec((None, 1, D_MODEL), lambda i, *_: (layer, 0, 0)),
            pl.BlockSpec((None, 1, D_MODEL), lambda i, *_: (layer, 0, 0)),
        ],
        out_specs=pl.BlockSpec(memory_space=pl.ANY),
        scratch_shapes=[
            *[pltpu.VMEM((TM * XE_PITCH, LANES), _F32) for _ in range(N_SLOTS)],
            *[pltpu.VMEM(obuf_shape, _F32) for _ in range(N_SLOTS)],
            pltpu.SemaphoreType.DMA((N_SLOTS,)),
            pltpu.SemaphoreType.DMA((N_SLOTS,)),
            pltpu.SMEM((N_SLOTS,), jnp.int32),
        ],
    )
    return pl.pallas_call(
        functools.partial(_moe_kernel, alpha, n_tiles, n_tok, out_token_major),
        grid_spec=grid_spec,
        out_shape=jax.ShapeDtypeStruct((out_rows, out_cols), _F32),
        compiler_params=pltpu.CompilerParams(
            dimension_semantics=("arbitrary",), vmem_limit_bytes=VMEM_LIMIT),
        name="moe",
    )(elo, ehi, nvalid, off, order, xe, wg, wu, wd, wg, wu, wd, g2, b2)


_PAIR_LO = (0, 0, 0, 1, 1, 2)
_PAIR_HI = (1, 2, 3, 2, 3, 3)


def _plan(cls, n_tiles):
    n_tok = cls.shape[0]
    tok = jnp.arange(n_tok, dtype=jnp.int32)
    order = jnp.sort(cls * n_tok + tok) % n_tok
    order = jnp.concatenate([order, jnp.zeros((TM,), jnp.int32)])
    counts = jnp.sum((cls[:, None] == jnp.arange(N_CLASSES, dtype=jnp.int32)[None, :]).astype(jnp.int32), axis=0)
    cstart = jnp.cumsum(counts) - counts
    tiles = (counts + TM - 1) // TM
    tile_end = jnp.cumsum(tiles)
    tile_start = tile_end - tiles
    ti = jnp.arange(n_tiles, dtype=jnp.int32)
    n_used = tile_end[-1]
    tsel = jnp.minimum(ti, n_used - 1)
    tcls = jnp.sum((tile_end[None, :] <= tsel[:, None]).astype(jnp.int32), axis=1)
    within = (tsel - tile_start[tcls]) * TM
    nvalid = jnp.where(ti < n_used, jnp.clip(counts[tcls] - within, 0, TM), 0).astype(jnp.int32)
    off = (cstart[tcls] + within).astype(jnp.int32)
    grp = tcls // N_PAIRS
    pair = tcls % N_PAIRS
    elo = grp * EXPERTS_PER_GROUP + jnp.asarray(_PAIR_LO, jnp.int32)[pair]
    ehi = grp * EXPERTS_PER_GROUP + jnp.asarray(_PAIR_HI, jnp.int32)[pair]
    return elo.astype(jnp.int32), ehi.astype(jnp.int32), nvalid, off, order


def kernel(x, w_in, conv_a_w, w_branch_a, conv_r_w, conv_r_b, w_rg_a, b_rg_a, w_rg_x, b_rg_x, rg_lambda, w_branch_r, w_out, ln1_g, ln1_b, w_router, b_router, w_exp_gate, w_exp_up, w_exp_down, ln2_g, ln2_b):
    n_seq, seq, _ = x.shape
    depth = w_in.shape[0]
    n_tok = n_seq * seq
    assert seq % TT == 0 and n_tok % TM == 0
    n_tiles = (n_tok + N_CLASSES * (TM - 1)) // TM + GATHER_AHEAD
    alpha = (2.0 * depth) ** 0.25

    wrt = w_router.T.astype(_F32)
    wrt_hi = wrt.astype(_BF)
    wrt_lo = (wrt - wrt_hi.astype(_F32)).astype(_BF)
    brt = b_router.astype(_F32).reshape(N_EXPERTS, 1)

    in_scale = jnp.concatenate([jnp.ones((OFF_GA,), _F32), jnp.full((D_IN - OFF_GA,), 0.5, _F32)])
    mix_w = (
        (w_in * in_scale).astype(_BF),
        conv_a_w,
        w_branch_a.astype(_BF),
        conv_r_w,
        conv_r_b.reshape(depth, 1, D_R),
        (0.5 * jnp.concatenate([w_rg_a, w_rg_x], axis=-1)).astype(_BF),
        0.5 * b_rg_a.reshape(depth, 1, D_R),
        0.5 * b_rg_x.reshape(depth, 1, D_R),
        rg_lambda.reshape(depth, 1, D_R),
        (0.25 * w_branch_r).astype(_BF),
        (0.5 * w_out).astype(_BF),
        ln1_g.reshape(depth, 1, D_MODEL),
        ln1_b.reshape(depth, 1, D_MODEL),
        wrt_hi, wrt_lo, brt,
        w_exp_gate.reshape(depth, N_EXPERTS * D_MODEL, D_FF),
        w_exp_up.reshape(depth, N_EXPERTS * D_MODEL, D_FF),
        w_exp_down.reshape(depth, N_EXPERTS * D_FF, D_MODEL),
    )
    ln2 = (ln2_g.reshape(depth, 1, D_MODEL), ln2_b.reshape(depth, 1, D_MODEL))

    h = x.reshape(n_tok, D_MODEL)
    for l in range(depth):
        xe, route, wg, wu, wd = _mix_call(alpha, l, h, l > 0, n_seq, n_tok, mix_w)
        plan = _plan(route[0].astype(jnp.int32), n_tiles + GATHER_AHEAD)
        moe_w = (wg.reshape(N_EXPERTS, D_MODEL, D_FF), wu.reshape(N_EXPERTS, D_MODEL, D_FF),
                 wd.reshape(N_EXPERTS, D_FF, D_MODEL)) + ln2
        h = _moe_call(alpha, l, xe, plan, moe_w, n_tok, l < depth - 1)
    return h.reshape(n_seq, seq, D_MODEL)
```

```python
import functools

import jax
import jax.numpy as jnp
from jax import lax
from jax.experimental import pallas as pl
from jax.experimental.pallas import tpu as pltpu

D_MODEL = 1024
D_A = 1024
D_R = 1280
N_RG_BLOCKS = 10
RG_BLOCK = 128
RG_C = 8.0
CONV_A_WIDTH = 3
CONV_R_WIDTH = 4
D_IN = 3 * D_A + 2 * D_R + 2 * D_MODEL
N_EXPERTS = 16
N_GROUPS = 4
EXPERTS_PER_GROUP = 4
N_PAIRS = 6
N_CLASSES = N_GROUPS * N_PAIRS
D_FF = 512
LN_EPS = 1e-5

OFF_HA, OFF_BA, OFF_CA = 0, D_A, 2 * D_A
OFF_XR = 3 * D_A
OFF_YR = OFF_XR + D_R
OFF_GA = OFF_YR + D_R
OFF_GR = OFF_GA + D_MODEL
LATE_COLS = D_IN - OFF_YR
LATE_CHUNK = 256

SUBLANES = 8
LANES = 128
EXT = LANES
X_PITCH = D_MODEL // LANES
XE_PITCH = (D_MODEL + EXT) // LANES
TT = 512
TM = 256
DMA_UNROLL = 8
GATHER_AHEAD = 2
N_SLOTS = GATHER_AHEAD + 1
VMEM_LIMIT = 56 * 1024 * 1024

_BF = jnp.bfloat16
_F32 = jnp.float32

GELU_K1 = 0.7978845608028654
GELU_K2 = GELU_K1 * 0.044715


def _layer_norm(y, g, b):
    mu = jnp.mean(y, axis=-1, keepdims=True)
    yc = y - mu
    var = jnp.mean(yc * yc, axis=-1, keepdims=True)
    return yc * lax.rsqrt(var + LN_EPS) * g + b


def _first_index_of(vals, target):
    idx = jnp.full(target.shape, float(len(vals) - 1), _F32)
    for j in range(len(vals) - 2, -1, -1):
        idx = jnp.where(vals[j] == target, float(j), idx)
    return idx


def _route(logits_t):
    m = jnp.max(logits_t, axis=0, keepdims=True)
    e = jnp.exp(logits_t - m)
    p = e / jnp.sum(e, axis=0, keepdims=True)
    neg = jnp.full((1, logits_t.shape[1]), -jnp.inf, _F32)
    scores, m1s, i1s, m2s, i2s = [], [], [], [], []
    for g in range(N_GROUPS):
        v = [p[g * EXPERTS_PER_GROUP + j:g * EXPERTS_PER_GROUP + j + 1, :] for j in range(EXPERTS_PER_GROUP)]
        m1 = jnp.maximum(jnp.maximum(v[0], v[1]), jnp.maximum(v[2], v[3]))
        i1 = _first_index_of(v, m1)
        w = [jnp.where(i1 == float(j), neg, v[j]) for j in range(EXPERTS_PER_GROUP)]
        m2 = jnp.maximum(jnp.maximum(w[0], w[1]), jnp.maximum(w[2], w[3]))
        i2 = _first_index_of(w, m2)
        scores.append(m1 + m2)
        m1s.append(m1); i1s.append(i1); m2s.append(m2); i2s.append(i2)
    best = jnp.maximum(jnp.maximum(scores[0], scores[1]), jnp.maximum(scores[2], scores[3]))
    gsel = _first_index_of(scores, best)

    def pick(xs):
        out = xs[N_GROUPS - 1]
        for g in range(N_GROUPS - 2, -1, -1):
            out = jnp.where(gsel == float(g), xs[g], out)
        return out

    m1, i1, m2, i2 = pick(m1s), pick(i1s), pick(m2s), pick(i2s)
    den = m1 + m2
    gate1, gate2 = m1 / den, m2 / den
    first_is_lo = i1 < i2
    lo = jnp.where(first_is_lo, i1, i2)
    hi = jnp.where(first_is_lo, i2, i1)
    g_lo = jnp.where(first_is_lo, gate1, gate2)
    g_hi = jnp.where(first_is_lo, gate2, gate1)
    pair = lo * (7.0 - lo) * 0.5 + (hi - lo - 1.0)
    cls = gsel * float(N_PAIRS) + pair
    return cls, g_lo, g_hi


def _load_token_major(ref, n_rows, pitch, lead=()):
    cols = [ref[lead + (pl.ds(c, n_rows, stride=pitch), slice(None))] for c in range(D_MODEL // LANES)]
    return jnp.concatenate(cols, axis=1)


def _store_token_major(ref, val, pitch, lead=()):
    n_rows = val.shape[0]
    for c in range(val.shape[1] // LANES):
        ref[lead + (pl.ds(c, n_rows, stride=pitch), slice(None))] = val[:, c * LANES:(c + 1) * LANES]


def _causal_conv(u, hist_ref, w_ref, n_taps, out_ref, cols=slice(None), bias=None):
    tt = u.shape[0]

    def tap(d):
        return w_ref[n_taps - 1 - d:n_taps - d, cols]

    def tap0(val):
        out = val * tap(0)
        return out if bias is None else out + bias

    acc = tap0(u)
    for d in range(1, n_taps):
        acc = acc + pltpu.roll(u, d, 0) * tap(d)
    out_ref[:, cols] = acc
    head = u[0:SUBLANES, :]
    hist = hist_ref[:, cols]
    row = lax.broadcasted_iota(jnp.int32, head.shape, 0)
    acc_head = tap0(head)
    for d in range(1, n_taps):
        shifted = jnp.where(row < d, pltpu.roll(hist, d, 0), pltpu.roll(head, d, 0))
        acc_head = acc_head + shifted * tap(d)
    out_ref[pl.ds(0, SUBLANES), cols] = acc_head
    hist_ref[:, cols] = u[tt - SUBLANES:tt, :]


def _mix_kernel(alpha, x_token_major, has_next,
                x_ref, win_ref, caw_ref, wa_ref, crw_ref, crb_ref, wrg_ref, brga_ref, brgx_ref,
                lam_ref, wr_ref, wo_ref, g1_ref, b1_ref, wrt_hi_ref, wrt_lo_ref, brt_ref,
                wg_f32, wu_f32, wd_f32, *rest):
    if has_next:
        nxt_in_f32, nxt_a_f32, nxt_o_f32, in_scale_ref = rest[:4]
        rest = rest[4:]
    xe_ref, route_ref, wg_bf, wu_bf, wd_bf = rest[:5]
    rest = rest[5:]
    if has_next:
        nxt_in_bf, nxt_a_bf, nxt_o_bf = rest[:3]
        rest = rest[3:]
        nxt_in_bf[...] = (nxt_in_f32[...] * in_scale_ref[...]).astype(_BF)
        nxt_a_bf[...] = nxt_a_f32[...].astype(_BF)
        nxt_o_bf[...] = (0.5 * nxt_o_f32[...]).astype(_BF)
    uhist, xrhist, cabuf, xcbuf, hbuf, hcarry, latebuf, vabuf = rest
    t = pl.program_id(1)
    tt = route_ref.shape[1]

    wg_bf[...] = (0.5 * wg_f32[...]).astype(_BF)
    wu_bf[...] = wu_f32[...].astype(_BF)
    wd_bf[...] = wd_f32[...].astype(_BF)

    @pl.when(t == 0)
    def _():
        uhist[...] = jnp.zeros_like(uhist)
        xrhist[...] = jnp.zeros_like(xrhist)
        hcarry[...] = jnp.zeros_like(hcarry)

    x = _load_token_major(x_ref, tt, X_PITCH) if x_token_major else x_ref[...]
    xb = x.astype(_BF)

    def proj(off, width):
        return jnp.dot(xb, win_ref[:, off:off + width], preferred_element_type=_F32)

    def mixer_a_chunk(c):
        cols = slice(c * LATE_CHUNK, (c + 1) * LATE_CHUNK)
        u = proj(OFF_CA + c * LATE_CHUNK, LATE_CHUNK) * proj(OFF_HA + c * LATE_CHUNK, LATE_CHUNK)
        _causal_conv(u, uhist, caw_ref, CONV_A_WIDTH, cabuf, cols=cols)
        vabuf[:, cols] = (proj(OFF_BA + c * LATE_CHUNK, LATE_CHUNK) * cabuf[:, cols]).astype(_BF)

    def late_chunk(j):
        latebuf[:, j * LATE_CHUNK:(j + 1) * LATE_CHUNK] = proj(OFF_YR + j * LATE_CHUNK, LATE_CHUNK)

    def xr_chunk(c):
        cols = slice(c * LATE_CHUNK, (c + 1) * LATE_CHUNK)
        _causal_conv(proj(OFF_XR + c * LATE_CHUNK, LATE_CHUNK), xrhist, crw_ref, CONV_R_WIDTH, xcbuf,
                     cols=cols, bias=crb_ref[:, cols])

    heads_per_chunk = LATE_CHUNK // RG_BLOCK
    z = -lam_ref[...]
    softplus_neg_lam = jnp.maximum(z, 0.0) + jnp.log1p(jnp.exp(-jnp.abs(z)))
    c_half = (-0.5 * RG_C) * softplus_neg_lam
    row = lax.broadcasted_iota(jnp.int32, (tt // SUBLANES, SUBLANES, RG_BLOCK), 1)
    side_work = [functools.partial(mixer_a_chunk, c) for c in range(D_A // LATE_CHUNK)]
    side_work += [functools.partial(late_chunk, j) for j in range(LATE_COLS // LATE_CHUNK)]
    side_cost = [3] * (D_A // LATE_CHUNK) + [1] * (LATE_COLS // LATE_CHUNK)
    per_head = sum(side_cost) / N_RG_BLOCKS
    done, spent = 0, 0.0
    for h in range(N_RG_BLOCKS):
        if h % heads_per_chunk == 0:
            xr_chunk(h // heads_per_chunk)
        while done < len(side_work) and (spent < (h + 1) * per_head or h == N_RG_BLOCKS - 1):
            side_work[done]()
            spent += side_cost[done]
            done += 1
        sl = slice(h * RG_BLOCK, (h + 1) * RG_BLOCK)
        xc = xcbuf[:, sl]
        gh = jnp.dot(xc.astype(_BF), wrg_ref[h], preferred_element_type=_F32)
        t_r = jnp.tanh(gh[:, :RG_BLOCK] + brga_ref[:, sl])
        t_i = jnp.tanh(gh[:, RG_BLOCK:] + brgx_ref[:, sl])
        log_a = c_half[:, sl] * t_r + c_half[:, sl]
        a_t = jnp.exp(log_a)
        b_t = jnp.exp(0.5 * jnp.log(1.0 - a_t * a_t)) * (t_i * xc + xc)
        a3 = a_t.reshape(tt // SUBLANES, SUBLANES, RG_BLOCK)
        b3 = b_t.reshape(tt // SUBLANES, SUBLANES, RG_BLOCK)
        for k in (1, 2, 4):
            keep = row >= k
            b3 = jnp.where(keep, a3 * pltpu.roll(b3, k, 1) + b3, b3)
            a3 = jnp.where(keep, a3 * pltpu.roll(a3, k, 1), a3)
        h_prev = hcarry[:, sl]
        for g in range(tt // SUBLANES):
            hg = a3[g] * h_prev + b3[g]
            hbuf[pl.ds(g * SUBLANES, SUBLANES), sl] = hg
            h_prev = hg[SUBLANES - 1:SUBLANES, :]
        hcarry[:, sl] = h_prev

    ya = jnp.dot(vabuf[...], wa_ref[...], preferred_element_type=_F32)
    v = latebuf[:, :D_R]
    th = jnp.tanh(v * (GELU_K2 * (v * v) + GELU_K1))
    vr = (hbuf[...] * (v * th + v)).astype(_BF)
    yr = jnp.dot(vr, wr_ref[...], preferred_element_type=_F32)

    t_a = jnp.tanh(latebuf[:, OFF_GA - OFF_YR:OFF_GR - OFF_YR])
    t_g = jnp.tanh(latebuf[:, OFF_GR - OFF_YR:])
    merged2 = (t_a * ya + ya) + (t_g * yr + yr)
    o = jnp.dot(merged2.astype(_BF), wo_ref[...], preferred_element_type=_F32)
    x1 = _layer_norm(alpha * x + o, g1_ref[...], b1_ref[...])

    x1_hi = x1.astype(_BF)
    x1_lo = (x1 - x1_hi.astype(_F32)).astype(_BF)
    nt_dims = (((1,), (1,)), ((), ()))
    logits_t = (lax.dot_general(wrt_hi_ref[...], x1_hi, nt_dims, preferred_element_type=_F32)
                + lax.dot_general(wrt_hi_ref[...], x1_lo, nt_dims, preferred_element_type=_F32)
                + lax.dot_general(wrt_lo_ref[...], x1_hi, nt_dims, preferred_element_type=_F32)
                + brt_ref[...])
    cls, g_lo, g_hi = _route(logits_t)
    rsel = lax.broadcasted_iota(jnp.int32, (SUBLANES, tt), 0)
    route_ref[...] = jnp.where(rsel == 0, cls, jnp.where(rsel == 1, g_lo, jnp.where(rsel == 2, g_hi, 0.0)))
    esel = lax.broadcasted_iota(jnp.int32, (EXT, tt), 0)
    ext_t = jnp.where(esel == 0, cls, jnp.where(esel == 1, g_lo, jnp.where(esel == 2, g_hi, 0.0)))
    _store_token_major(xe_ref, jnp.concatenate([x1, ext_t.T], axis=1), XE_PITCH)


def _mix_call(alpha, layer, x2d, x_token_major, n_seq, n_tok, w, big3, nxt):
    seq = n_tok // n_seq
    nt = seq // TT
    has_next = nxt is not None

    def layer_const(shape):
        zeros = (0,) * len(shape)
        return pl.BlockSpec((None,) + shape, lambda b, t: (layer,) + zeros, pipeline_mode=pl.Buffered(1))

    def const(shape):
        zeros = (0,) * len(shape)
        return pl.BlockSpec(shape, lambda b, t: zeros, pipeline_mode=pl.Buffered(1))

    w = list(w)
    w[0], w[2], w[10] = big3
    in_specs = [
        (pl.BlockSpec((TT * X_PITCH, LANES), lambda b, t: (b * nt + t, 0)) if x_token_major
         else pl.BlockSpec((TT, D_MODEL), lambda b, t: (b * nt + t, 0))),
        const((D_MODEL, D_IN)),
        layer_const((CONV_A_WIDTH, D_A)),
        const((D_A, D_MODEL)),
        layer_const((CONV_R_WIDTH, D_R)),
        layer_const((1, D_R)),
        layer_const((N_RG_BLOCKS, RG_BLOCK, 2 * RG_BLOCK)),
        layer_const((1, D_R)),
        layer_const((1, D_R)),
        layer_const((1, D_R)),
        layer_const((D_R, D_MODEL)),
        const((D_MODEL, D_MODEL)),
        layer_const((1, D_MODEL)),
        layer_const((1, D_MODEL)),
        const((N_EXPERTS, D_MODEL)),
        const((N_EXPERTS, D_MODEL)),
        const((N_EXPERTS, 1)),
    ]
    n_steps = n_seq * nt
    slab_specs, slab_shapes = [], []
    for rows, cols in ((N_EXPERTS * D_MODEL, D_FF), (N_EXPERTS * D_MODEL, D_FF), (N_EXPERTS * D_FF, D_MODEL)):
        slab = rows // n_steps
        assert slab * n_steps == rows and slab % (2 * SUBLANES) == 0
        in_specs.append(pl.BlockSpec((None, slab, cols), lambda b, t: (layer, b * nt + t, 0)))
        slab_specs.append(pl.BlockSpec((slab, cols), lambda b, t: (b * nt + t, 0)))
        slab_shapes.append(jax.ShapeDtypeStruct((rows, cols), _BF))
    if has_next:
        for rows, cols in ((D_MODEL, D_IN), (D_A, D_MODEL), (D_MODEL, D_MODEL)):
            slab = rows // n_steps
            assert slab * n_steps == rows and slab % (2 * SUBLANES) == 0
            in_specs.append(pl.BlockSpec((None, slab, cols), lambda b, t: (layer + 1, b * nt + t, 0)))
            slab_specs.append(pl.BlockSpec((slab, cols), lambda b, t: (b * nt + t, 0)))
            slab_shapes.append(jax.ShapeDtypeStruct((rows, cols), _BF))
        in_specs.append(const((1, D_IN)))
        w = w + list(nxt)
    out_specs = [
        pl.BlockSpec((TT * XE_PITCH, LANES), lambda b, t: (b * nt + t, 0)),
        pl.BlockSpec((SUBLANES, TT), lambda b, t: (0, b * nt + t)),
    ] + slab_specs
    out_shape = [
        jax.ShapeDtypeStruct((n_tok * XE_PITCH, LANES), _F32),
        jax.ShapeDtypeStruct((SUBLANES, n_tok), _F32),
    ] + slab_shapes
    scratch = [
        pltpu.VMEM((SUBLANES, D_A), _F32),
        pltpu.VMEM((SUBLANES, D_R), _F32),
        pltpu.VMEM((TT, D_A), _F32),
        pltpu.VMEM((TT, D_R), _F32),
        pltpu.VMEM((TT, D_R), _F32),
        pltpu.VMEM((1, D_R), _F32),
        pltpu.VMEM((TT, LATE_COLS), _F32),
        pltpu.VMEM((TT, D_A), _BF),
    ]
    return pl.pallas_call(
        functools.partial(_mix_kernel, alpha, x_token_major, has_next),
        grid=(n_seq, nt),
        in_specs=in_specs,
        out_specs=out_specs,
        out_shape=out_shape,
        scratch_shapes=scratch,
        compiler_params=pltpu.CompilerParams(
            dimension_semantics=("arbitrary", "arbitrary"), vmem_limit_bytes=VMEM_LIMIT),
        name="mix",
    )(x2d, *w)


def _moe_kernel(alpha, n_tiles, n_tok, out_token_major,
                elo_ref, ehi_ref, nv_ref, off_ref, order_ref,
                xe_hbm,
                wg_lo, wu_lo, wd_lo, wg_hi, wu_hi, wd_hi, g2_ref, b2_ref,
                out_hbm,
                gbuf0, gbuf1, gbuf2, obuf0, obuf1, obuf2, gsem, ssem, pend):
    gbuf = (gbuf0, gbuf1, gbuf2)
    obuf = (obuf0, obuf1, obuf2)
    i = pl.program_id(0)
    nv = nv_ref[i]
    o_pitch = X_PITCH if out_token_major else 1
    defer_scatter = out_token_major

    def gather_copy(tok, r, s):
        return pltpu.make_async_copy(xe_hbm.at[pl.ds(tok * XE_PITCH, XE_PITCH), :],
                                     gbuf[s].at[pl.ds(r * XE_PITCH, XE_PITCH), :], gsem.at[s])

    def scatter_copy(tok, r, s, n_rows=1, group=None):
        dst = out_hbm.at[pl.ds(tok * o_pitch, n_rows * o_pitch), :]
        if out_token_major:
            src = r * o_pitch if isinstance(r, int) else pl.multiple_of(r * o_pitch, o_pitch)
            return pltpu.make_async_copy(obuf[s].at[pl.ds(src, n_rows * o_pitch), :], dst, ssem.at[s])
        if group is None:
            group, r = lax.shift_right_logical(r, 3), jnp.bitwise_and(r, SUBLANES - 1)
        return pltpu.make_async_copy(obuf[s].at[group, pl.ds(r, n_rows), :], dst, ssem.at[s])

    def scatter_wait_copy(s, n_rows):
        if out_token_major:
            return scatter_copy(0, 0, s, n_rows=n_rows)
        if n_rows < SUBLANES:
            block = obuf[s].at[0, pl.ds(0, n_rows), :]
        else:
            block = obuf[s].at[pl.ds(0, n_rows // SUBLANES)]
        return pltpu.make_async_copy(block, block, ssem.at[s])

    def start_gather(tile, s):
        base = off_ref[tile]
        for r in range(TM):
            gather_copy(order_ref[base + r], r, s).start(priority=r % 2)

    def wait_gather(s):
        pltpu.make_async_copy(xe_hbm.at[pl.ds(0, TM * XE_PITCH), :], gbuf[s], gsem.at[s]).wait()

    def start_scatter(tile, s, n_rows):
        base = off_ref[tile]
        n_chunks = lax.shift_right_logical(n_rows, DMA_UNROLL.bit_length() - 1)

        def chunk(c, carry):
            r0 = c * DMA_UNROLL
            for j in range(DMA_UNROLL):
                if out_token_major:
                    copy = scatter_copy(order_ref[base + r0 + j], r0 + j, s)
                else:
                    copy = scatter_copy(order_ref[base + r0 + j], j, s, group=c)
                copy.start(priority=j % 2)
            return carry
        lax.fori_loop(0, n_chunks, chunk, 0)

        def tail(r, carry):
            scatter_copy(order_ref[base + r], r, s).start()
            return carry
        lax.fori_loop(n_chunks * DMA_UNROLL, n_rows, tail, 0)

    def start_scatter_full(tile, n_valid, s):
        base = off_ref[tile]
        for r in range(TM):
            tok = jnp.where(r < n_valid, order_ref[base + r], n_tok + s * TM + r)
            scatter_copy(tok, r, s).start(priority=r % 2)
        pend[s] = TM

    def wait_scatter(s):
        n_rows = pend[s]
        width = TM
        while width >= (TM if defer_scatter else 1):
            @pl.when(jnp.bitwise_and(n_rows, width) != 0)
            def _(width=width):
                scatter_wait_copy(s, width).wait()
            width //= 2
        pend[s] = 0

    prev = jnp.maximum(i - 1, 0)
    nv_prev = jnp.where(i > 0, nv_ref[prev], 0)

    @pl.when(i == 0)
    def _():
        for s in range(N_SLOTS):
            pend[s] = 0
        for ahead in range(GATHER_AHEAD):
            start_gather(ahead, ahead)
        if defer_scatter:
            obuf[N_SLOTS - 1][...] = jnp.zeros_like(obuf[N_SLOTS - 1])
            for s in range(N_SLOTS):
                fill = scatter_copy(n_tok + s * TM, 0, N_SLOTS - 1, n_rows=TM)
                fill.start()
                fill.wait()

    def tile_body(s):
        wait_gather(s)
        wait_scatter(s)
        xf = _load_token_major(gbuf[s], TM, XE_PITCH)
        ext = gbuf[s][pl.ds(D_MODEL // LANES, TM, stride=XE_PITCH), :]
        gate_lo = ext[:, 1:2]
        gate_hi = ext[:, 2:3]
        xb = xf.astype(_BF)
        start_gather(i + GATHER_AHEAD, (s + GATHER_AHEAD) % N_SLOTS)
        if defer_scatter:
            start_scatter_full(prev, nv_prev, (s + N_SLOTS - 1) % N_SLOTS)

        def expert(wg, wu, wd, gate):
            a2 = jnp.dot(xb, wg[...], preferred_element_type=_F32)
            u = jnp.dot(xb, wu[...], preferred_element_type=_F32)
            hgt = ((a2 * jnp.tanh(a2) + a2) * (u * gate)).astype(_BF)
            return jnp.dot(hgt, wd[...], preferred_element_type=_F32)

        moe = expert(wg_lo, wu_lo, wd_lo, gate_lo) + expert(wg_hi, wu_hi, wd_hi, gate_hi)
        y = _layer_norm(alpha * xf + moe, g2_ref[...], b2_ref[...])
        if out_token_major:
            _store_token_major(obuf[s], y, X_PITCH)
        else:
            obuf[s][...] = y.reshape(TM // SUBLANES, SUBLANES, D_MODEL)
        if not defer_scatter:
            start_scatter(i, s, nv)
            pend[s] = nv

    slot = lax.rem(i, N_SLOTS)
    issuer = jnp.maximum(i - GATHER_AHEAD, 0)
    gathered_unused = jnp.logical_and(nv == 0, jnp.logical_or(i < GATHER_AHEAD, nv_ref[issuer] > 0))
    for s in range(N_SLOTS):
        @pl.when(jnp.logical_and(nv > 0, slot == s))
        def _(s=s):
            tile_body(s)

        @pl.when(jnp.logical_and(gathered_unused, slot == s))
        def _(s=s):
            wait_gather(s)

        if defer_scatter:
            @pl.when(jnp.logical_and(jnp.logical_and(nv == 0, nv_prev > 0), slot == s))
            def _(s=s):
                start_scatter_full(prev, nv_prev, (s + N_SLOTS - 1) % N_SLOTS)

    @pl.when(i == n_tiles - 1)
    def _():
        for s in range(N_SLOTS):
            wait_scatter(s)


def _moe_call(alpha, layer, xe, plan, w, n_tok, out_token_major):
    elo, ehi, nvalid, off, order = plan
    n_tiles = elo.shape[0] - GATHER_AHEAD
    wg, wu, wd, g2, b2 = w
    if out_token_major:
        obuf_shape, out_rows, out_cols = (TM * X_PITCH, LANES), (n_tok + N_SLOTS * TM) * X_PITCH, LANES
    else:
        obuf_shape, out_rows, out_cols = (TM // SUBLANES, SUBLANES, D_MODEL), n_tok, D_MODEL

    def wspec(shape, which):
        def imap(i, elo_r, ehi_r, *_):
            return ((elo_r, ehi_r)[which][i], 0, 0)
        return pl.BlockSpec((None,) + shape, imap)

    grid_spec = pltpu.PrefetchScalarGridSpec(
        num_scalar_prefetch=5,
        grid=(n_tiles,),
        in_specs=[
            pl.BlockSpec(memory_space=pl.ANY),
            wspec((D_MODEL, D_FF), 0), wspec((D_MODEL, D_FF), 0), wspec((D_FF, D_MODEL), 0),
            wspec((D_MODEL, D_FF), 1), wspec((D_MODEL, D_FF), 1), wspec((D_FF, D_MODEL), 1),
            pl.BlockSpec((None, 1, D_MODEL), lambda i, *_: (layer, 0, 0)),
            pl.BlockSpec((None, 1, D_MODEL), lambda i, *_: (layer, 0, 0)),
        ],
        out_specs=pl.BlockSpec(memory_space=pl.ANY),
        scratch_shapes=[
            *[pltpu.VMEM((TM * XE_PITCH, LANES), _F32) for _ in range(N_SLOTS)],
            *[pltpu.VMEM(obuf_shape, _F32) for _ in range(N_SLOTS)],
            pltpu.SemaphoreType.DMA((N_SLOTS,)),
            pltpu.SemaphoreType.DMA((N_SLOTS,)),
            pltpu.SMEM((N_SLOTS,), jnp.int32),
        ],
    )
    return pl.pallas_call(
        functools.partial(_moe_kernel, alpha, n_tiles, n_tok, out_token_major),
        grid_spec=grid_spec,
        out_shape=jax.ShapeDtypeStruct((out_rows, out_cols), _F32),
        compiler_params=pltpu.CompilerParams(
            dimension_semantics=("arbitrary",), vmem_limit_bytes=VMEM_LIMIT),
        name="moe",
    )(elo, ehi, nvalid, off, order, xe, wg, wu, wd, wg, wu, wd, g2, b2)


_PAIR_LO = (0, 0, 0, 1, 1, 2)
_PAIR_HI = (1, 2, 3, 2, 3, 3)


def _plan(cls, n_tiles):
    n_tok = cls.shape[0]
    tok = jnp.arange(n_tok, dtype=jnp.int32)
    order = jnp.sort(cls * n_tok + tok) % n_tok
    order = jnp.concatenate([order, jnp.zeros((TM,), jnp.int32)])
    counts = jnp.sum((cls[:, None] == jnp.arange(N_CLASSES, dtype=jnp.int32)[None, :]).astype(jnp.int32), axis=0)
    cstart = jnp.cumsum(counts) - counts
    tiles = (counts + TM - 1) // TM
    tile_end = jnp.cumsum(tiles)
    tile_start = tile_end - tiles
    ti = jnp.arange(n_tiles, dtype=jnp.int32)
    n_used = tile_end[-1]
    tsel = jnp.minimum(ti, n_used - 1)
    tcls = jnp.sum((tile_end[None, :] <= tsel[:, None]).astype(jnp.int32), axis=1)
    within = (tsel - tile_start[tcls]) * TM
    nvalid = jnp.where(ti < n_used, jnp.clip(counts[tcls] - within, 0, TM), 0).astype(jnp.int32)
    off = (cstart[tcls] + within).astype(jnp.int32)
    grp = tcls // N_PAIRS
    pair = tcls % N_PAIRS
    elo = grp * EXPERTS_PER_GROUP + jnp.asarray(_PAIR_LO, jnp.int32)[pair]
    ehi = grp * EXPERTS_PER_GROUP + jnp.asarray(_PAIR_HI, jnp.int32)[pair]
    return elo.astype(jnp.int32), ehi.astype(jnp.int32), nvalid, off, order


def kernel(x, w_in, conv_a_w, w_branch_a, conv_r_w, conv_r_b, w_rg_a, b_rg_a, w_rg_x, b_rg_x, rg_lambda, w_branch_r, w_out, ln1_g, ln1_b, w_router, b_router, w_exp_gate, w_exp_up, w_exp_down, ln2_g, ln2_b):
    n_seq, seq, _ = x.shape
    depth = w_in.shape[0]
    n_tok = n_seq * seq
    assert seq % TT == 0 and n_tok % TM == 0
    n_tiles = (n_tok + N_CLASSES * (TM - 1)) // TM + GATHER_AHEAD
    alpha = (2.0 * depth) ** 0.25

    wrt = w_router.T.astype(_F32)
    wrt_hi = wrt.astype(_BF)
    wrt_lo = (wrt - wrt_hi.astype(_F32)).astype(_BF)
    brt = b_router.astype(_F32).reshape(N_EXPERTS, 1)

    in_scale = jnp.concatenate([jnp.ones((OFF_GA,), _F32), jnp.full((D_IN - OFF_GA,), 0.5, _F32)])
    big3 = ((w_in[0] * in_scale).astype(_BF), w_branch_a[0].astype(_BF), (0.5 * w_out[0]).astype(_BF))
    mix_w = (
        None,
        conv_a_w,
        None,
        conv_r_w,
        conv_r_b.reshape(depth, 1, D_R),
        (0.5 * jnp.concatenate([w_rg_a, w_rg_x], axis=-1)).astype(_BF),
        0.5 * b_rg_a.reshape(depth, 1, D_R),
        0.5 * b_rg_x.reshape(depth, 1, D_R),
        rg_lambda.reshape(depth, 1, D_R),
        (0.25 * w_branch_r).astype(_BF),
        None,
        ln1_g.reshape(depth, 1, D_MODEL),
        ln1_b.reshape(depth, 1, D_MODEL),
        wrt_hi, wrt_lo, brt,
        w_exp_gate.reshape(depth, N_EXPERTS * D_MODEL, D_FF),
        w_exp_up.reshape(depth, N_EXPERTS * D_MODEL, D_FF),
        w_exp_down.reshape(depth, N_EXPERTS * D_FF, D_MODEL),
    )
    ln2 = (ln2_g.reshape(depth, 1, D_MODEL), ln2_b.reshape(depth, 1, D_MODEL))

    h = x.reshape(n_tok, D_MODEL)
    for l in range(depth):
        nxt = (w_in, w_branch_a, w_out, in_scale.reshape(1, D_IN)) if l + 1 < depth else None
        xe, route, wg, wu, wd, *cast_next = _mix_call(alpha, l, h, l > 0, n_seq, n_tok, mix_w, big3, nxt)
        big3 = tuple(cast_next)
        plan = _plan(route[0].astype(jnp.int32), n_tiles + GATHER_AHEAD)
        moe_w = (wg.reshape(N_EXPERTS, D_MODEL, D_FF), wu.reshape(N_EXPERTS, D_MODEL, D_FF),
                 wd.reshape(N_EXPERTS, D_FF, D_MODEL)) + ln2
        h = _moe_call(alpha, l, xe, plan, moe_w, n_tok, l < depth - 1)
    return h.reshape(n_seq, seq, D_MODEL)
```

```python
import functools

import jax
import jax.numpy as jnp
from jax import lax
from jax.experimental import pallas as pl
from jax.experimental.pallas import tpu as pltpu

D_MODEL = 1024
D_A = 1024
D_R = 1280
N_RG_BLOCKS = 10
RG_BLOCK = 128
RG_C = 8.0
CONV_A_WIDTH = 3
CONV_R_WIDTH = 4
D_IN = 3 * D_A + 2 * D_R + 2 * D_MODEL
N_EXPERTS = 16
N_GROUPS = 4
EXPERTS_PER_GROUP = 4
N_PAIRS = 6
N_CLASSES = N_GROUPS * N_PAIRS
D_FF = 512
LN_EPS = 1e-5

OFF_HA, OFF_BA, OFF_CA = 0, D_A, 2 * D_A
OFF_XR = 3 * D_A
OFF_YR = OFF_XR + D_R
OFF_GA = OFF_YR + D_R
OFF_GR = OFF_GA + D_MODEL
LATE_COLS = D_IN - OFF_YR
LATE_CHUNK = 256

SUBLANES = 8
LANES = 128
EXT = LANES
X_PITCH = D_MODEL // LANES
XE_PITCH = (D_MODEL + EXT) // LANES
TT = 512
TM = 256
DMA_UNROLL = 8
GATHER_AHEAD = 2
N_SLOTS = GATHER_AHEAD + 1
VMEM_LIMIT = 56 * 1024 * 1024

_BF = jnp.bfloat16
_F32 = jnp.float32

GELU_K1 = 0.7978845608028654
GELU_K2 = GELU_K1 * 0.044715


def _layer_norm(y, g, b):
    mu = jnp.mean(y, axis=-1, keepdims=True)
    yc = y - mu
    var = jnp.mean(yc * yc, axis=-1, keepdims=True)
    return yc * lax.rsqrt(var + LN_EPS) * g + b


def _first_index_of(vals, target):
    idx = jnp.full(target.shape, float(len(vals) - 1), _F32)
    for j in range(len(vals) - 2, -1, -1):
        idx = jnp.where(vals[j] == target, float(j), idx)
    return idx


def _route(logits_t):
    m = jnp.max(logits_t, axis=0, keepdims=True)
    e = jnp.exp(logits_t - m)
    p = e / jnp.sum(e, axis=0, keepdims=True)
    neg = jnp.full((1, logits_t.shape[1]), -jnp.inf, _F32)
    scores, m1s, i1s, m2s, i2s = [], [], [], [], []
    for g in range(N_GROUPS):
        v = [p[g * EXPERTS_PER_GROUP + j:g * EXPERTS_PER_GROUP + j + 1, :] for j in range(EXPERTS_PER_GROUP)]
        m1 = jnp.maximum(jnp.maximum(v[0], v[1]), jnp.maximum(v[2], v[3]))
        i1 = _first_index_of(v, m1)
        w = [jnp.where(i1 == float(j), neg, v[j]) for j in range(EXPERTS_PER_GROUP)]
        m2 = jnp.maximum(jnp.maximum(w[0], w[1]), jnp.maximum(w[2], w[3]))
        i2 = _first_index_of(w, m2)
        scores.append(m1 + m2)
        m1s.append(m1); i1s.append(i1); m2s.append(m2); i2s.append(i2)
    best = jnp.maximum(jnp.maximum(scores[0], scores[1]), jnp.maximum(scores[2], scores[3]))
    gsel = _first_index_of(scores, best)

    def pick(xs):
        out = xs[N_GROUPS - 1]
        for g in range(N_GROUPS - 2, -1, -1):
            out = jnp.where(gsel == float(g), xs[g], out)
        return out

    m1, i1, m2, i2 = pick(m1s), pick(i1s), pick(m2s), pick(i2s)
    den = m1 + m2
    gate1, gate2 = m1 / den, m2 / den
    first_is_lo = i1 < i2
    lo = jnp.where(first_is_lo, i1, i2)
    hi = jnp.where(first_is_lo, i2, i1)
    g_lo = jnp.where(first_is_lo, gate1, gate2)
    g_hi = jnp.where(first_is_lo, gate2, gate1)
    pair = lo * (7.0 - lo) * 0.5 + (hi - lo - 1.0)
    cls = gsel * float(N_PAIRS) + pair
    return cls, g_lo, g_hi


def _load_token_major(ref, n_rows, pitch, lead=()):
    cols = [ref[lead + (pl.ds(c, n_rows, stride=pitch), slice(None))] for c in range(D_MODEL // LANES)]
    return jnp.concatenate(cols, axis=1)


def _store_token_major(ref, val, pitch, lead=()):
    n_rows = val.shape[0]
    for c in range(val.shape[1] // LANES):
        ref[lead + (pl.ds(c, n_rows, stride=pitch), slice(None))] = val[:, c * LANES:(c + 1) * LANES]


def _causal_conv(u, hist_ref, w_ref, n_taps, out_ref, cols=slice(None), bias=None):
    tt = u.shape[0]

    def tap(d):
        return w_ref[n_taps - 1 - d:n_taps - d, cols]

    def tap0(val):
        out = val * tap(0)
        return out if bias is None else out + bias

    acc = tap0(u)
    for d in range(1, n_taps):
        acc = acc + pltpu.roll(u, d, 0) * tap(d)
    out_ref[:, cols] = acc
    head = u[0:SUBLANES, :]
    hist = hist_ref[:, cols]
    row = lax.broadcasted_iota(jnp.int32, head.shape, 0)
    acc_head = tap0(head)
    for d in range(1, n_taps):
        shifted = jnp.where(row < d, pltpu.roll(hist, d, 0), pltpu.roll(head, d, 0))
        acc_head = acc_head + shifted * tap(d)
    out_ref[pl.ds(0, SUBLANES), cols] = acc_head
    hist_ref[:, cols] = u[tt - SUBLANES:tt, :]


def _mix_kernel(alpha, x_token_major,
                x_ref, win_ref, caw_ref, wa_ref, crw_ref, crb_ref, wrg_ref, brga_ref, brgx_ref,
                lam_ref, wr_ref, wo_ref, g1_ref, b1_ref, wrt_hi_ref, wrt_lo_ref, brt_ref,
                wg_f32, wu_f32, wd_f32,
                xe_ref, route_ref, wg_bf, wu_bf, wd_bf,
                uhist, xrhist, cabuf, xcbuf, hbuf, hcarry, latebuf, vabuf):
    t = pl.program_id(1)
    tt = route_ref.shape[1]

    wg_bf[...] = (0.5 * wg_f32[...]).astype(_BF)
    wu_bf[...] = wu_f32[...].astype(_BF)
    wd_bf[...] = wd_f32[...].astype(_BF)

    @pl.when(t == 0)
    def _():
        uhist[...] = jnp.zeros_like(uhist)
        xrhist[...] = jnp.zeros_like(xrhist)
        hcarry[...] = jnp.zeros_like(hcarry)

    x = _load_token_major(x_ref, tt, X_PITCH) if x_token_major else x_ref[...]
    xb = x.astype(_BF)

    def proj(off, width):
        return jnp.dot(xb, win_ref[:, off:off + width], preferred_element_type=_F32)

    def mixer_a_chunk(c):
        cols = slice(c * LATE_CHUNK, (c + 1) * LATE_CHUNK)
        u = proj(OFF_CA + c * LATE_CHUNK, LATE_CHUNK) * proj(OFF_HA + c * LATE_CHUNK, LATE_CHUNK)
        _causal_conv(u, uhist, caw_ref, CONV_A_WIDTH, cabuf, cols=cols)
        vabuf[:, cols] = (proj(OFF_BA + c * LATE_CHUNK, LATE_CHUNK) * cabuf[:, cols]).astype(_BF)

    def late_chunk(j):
        latebuf[:, j * LATE_CHUNK:(j + 1) * LATE_CHUNK] = proj(OFF_YR + j * LATE_CHUNK, LATE_CHUNK)

    def xr_chunk(c):
        cols = slice(c * LATE_CHUNK, (c + 1) * LATE_CHUNK)
        _causal_conv(proj(OFF_XR + c * LATE_CHUNK, LATE_CHUNK), xrhist, crw_ref, CONV_R_WIDTH, xcbuf,
                     cols=cols, bias=crb_ref[:, cols])

    heads_per_chunk = LATE_CHUNK // RG_BLOCK
    z = -lam_ref[...]
    softplus_neg_lam = jnp.maximum(z, 0.0) + jnp.log1p(jnp.exp(-jnp.abs(z)))
    c_half = (-0.5 * RG_C) * softplus_neg_lam
    row = lax.broadcasted_iota(jnp.int32, (tt // SUBLANES, SUBLANES, RG_BLOCK), 1)
    side_work = [functools.partial(mixer_a_chunk, c) for c in range(D_A // LATE_CHUNK)]
    side_work += [functools.partial(late_chunk, j) for j in range(LATE_COLS // LATE_CHUNK)]
    side_cost = [3] * (D_A // LATE_CHUNK) + [1] * (LATE_COLS // LATE_CHUNK)
    per_head = sum(side_cost) / N_RG_BLOCKS
    done, spent = 0, 0.0
    for h in range(N_RG_BLOCKS):
        if h % heads_per_chunk == 0:
            xr_chunk(h // heads_per_chunk)
        while done < len(side_work) and (spent < (h + 1) * per_head or h == N_RG_BLOCKS - 1):
            side_work[done]()
            spent += side_cost[done]
            done += 1
        sl = slice(h * RG_BLOCK, (h + 1) * RG_BLOCK)
        xc = xcbuf[:, sl]
        gh = jnp.dot(xc.astype(_BF), wrg_ref[h], preferred_element_type=_F32)
        t_r = jnp.tanh(gh[:, :RG_BLOCK] + brga_ref[:, sl])
        t_i = jnp.tanh(gh[:, RG_BLOCK:] + brgx_ref[:, sl])
        log_a = c_half[:, sl] * t_r + c_half[:, sl]
        a_t = jnp.exp(log_a)
        b_t = jnp.exp(0.5 * jnp.log(1.0 - a_t * a_t)) * (t_i * xc + xc)
        a3 = a_t.reshape(tt // SUBLANES, SUBLANES, RG_BLOCK)
        b3 = b_t.reshape(tt // SUBLANES, SUBLANES, RG_BLOCK)
        for k in (1, 2, 4):
            keep = row >= k
            b3 = jnp.where(keep, a3 * pltpu.roll(b3, k, 1) + b3, b3)
            a3 = jnp.where(keep, a3 * pltpu.roll(a3, k, 1), a3)
        h_prev = hcarry[:, sl]
        for g in range(tt // SUBLANES):
            hg = a3[g] * h_prev + b3[g]
            hbuf[pl.ds(g * SUBLANES, SUBLANES), sl] = hg
            h_prev = hg[SUBLANES - 1:SUBLANES, :]
        hcarry[:, sl] = h_prev

    ya = jnp.dot(vabuf[...], wa_ref[...], preferred_element_type=_F32)
    v = latebuf[:, :D_R]
    th = jnp.tanh(v * (GELU_K2 * (v * v) + GELU_K1))
    vr = (hbuf[...] * (v * th + v)).astype(_BF)
    yr = jnp.dot(vr, wr_ref[...], preferred_element_type=_F32)

    t_a = jnp.tanh(latebuf[:, OFF_GA - OFF_YR:OFF_GR - OFF_YR])
    t_g = jnp.tanh(latebuf[:, OFF_GR - OFF_YR:])
    merged2 = (t_a * ya + ya) + (t_g * yr + yr)
    o = jnp.dot(merged2.astype(_BF), wo_ref[...], preferred_element_type=_F32)
    x1 = _layer_norm(alpha * x + o, g1_ref[...], b1_ref[...])

    x1_hi = x1.astype(_BF)
    x1_lo = (x1 - x1_hi.astype(_F32)).astype(_BF)
    nt_dims = (((1,), (1,)), ((), ()))
    logits_t = (lax.dot_general(wrt_hi_ref[...], x1_hi, nt_dims, preferred_element_type=_F32)
                + lax.dot_general(wrt_hi_ref[...], x1_lo, nt_dims, preferred_element_type=_F32)
                + lax.dot_general(wrt_lo_ref[...], x1_hi, nt_dims, preferred_element_type=_F32)
                + brt_ref[...])
    cls, g_lo, g_hi = _route(logits_t)
    rsel = lax.broadcasted_iota(jnp.int32, (SUBLANES, tt), 0)
    route_ref[...] = jnp.where(rsel == 0, cls, jnp.where(rsel == 1, g_lo, jnp.where(rsel == 2, g_hi, 0.0)))
    esel = lax.broadcasted_iota(jnp.int32, (EXT, tt), 0)
    ext_t = jnp.where(esel == 0, cls, jnp.where(esel == 1, g_lo, jnp.where(esel == 2, g_hi, 0.0)))
    _store_token_major(xe_ref, jnp.concatenate([x1, ext_t.T], axis=1), XE_PITCH)


def _mix_call(alpha, layer, x2d, x_token_major, n_seq, n_tok, w):
    seq = n_tok // n_seq
    nt = seq // TT

    def layer_const(shape):
        zeros = (0,) * len(shape)
        return pl.BlockSpec((None,) + shape, lambda b, t: (layer,) + zeros, pipeline_mode=pl.Buffered(1))

    def const(shape):
        zeros = (0,) * len(shape)
        return pl.BlockSpec(shape, lambda b, t: zeros, pipeline_mode=pl.Buffered(1))

    in_specs = [
        (pl.BlockSpec((TT * X_PITCH, LANES), lambda b, t: (b * nt + t, 0)) if x_token_major
         else pl.BlockSpec((TT, D_MODEL), lambda b, t: (b * nt + t, 0))),
        layer_const((D_MODEL, D_IN)),
        layer_const((CONV_A_WIDTH, D_A)),
        layer_const((D_A, D_MODEL)),
        layer_const((CONV_R_WIDTH, D_R)),
        layer_const((1, D_R)),
        layer_const((N_RG_BLOCKS, RG_BLOCK, 2 * RG_BLOCK)),
        layer_const((1, D_R)),
        layer_const((1, D_R)),
        layer_const((1, D_R)),
        layer_const((D_R, D_MODEL)),
        layer_const((D_MODEL, D_MODEL)),
        layer_const((1, D_MODEL)),
        layer_const((1, D_MODEL)),
        const((N_EXPERTS, D_MODEL)),
        const((N_EXPERTS, D_MODEL)),
        const((N_EXPERTS, 1)),
    ]
    n_steps = n_seq * nt
    slab_specs, slab_shapes = [], []
    for rows, cols in ((N_EXPERTS * D_MODEL, D_FF), (N_EXPERTS * D_MODEL, D_FF), (N_EXPERTS * D_FF, D_MODEL)):
        slab = rows // n_steps
        assert slab * n_steps == rows and slab % (2 * SUBLANES) == 0
        in_specs.append(pl.BlockSpec((None, slab, cols), lambda b, t: (layer, b * nt + t, 0)))
        slab_specs.append(pl.BlockSpec((slab, cols), lambda b, t: (b * nt + t, 0)))
        slab_shapes.append(jax.ShapeDtypeStruct((rows, cols), _BF))
    out_specs = [
        pl.BlockSpec((TT * XE_PITCH, LANES), lambda b, t: (b * nt + t, 0)),
        pl.BlockSpec((SUBLANES, TT), lambda b, t: (0, b * nt + t)),
    ] + slab_specs
    out_shape = [
        jax.ShapeDtypeStruct((n_tok * XE_PITCH, LANES), _F32),
        jax.ShapeDtypeStruct((SUBLANES, n_tok), _F32),
    ] + slab_shapes
    scratch = [
        pltpu.VMEM((SUBLANES, D_A), _F32),
        pltpu.VMEM((SUBLANES, D_R), _F32),
        pltpu.VMEM((TT, D_A), _F32),
        pltpu.VMEM((TT, D_R), _F32),
        pltpu.VMEM((TT, D_R), _F32),
        pltpu.VMEM((1, D_R), _F32),
        pltpu.VMEM((TT, LATE_COLS), _F32),
        pltpu.VMEM((TT, D_A), _BF),
    ]
    return pl.pallas_call(
        functools.partial(_mix_kernel, alpha, x_token_major),
        grid=(n_seq, nt),
        in_specs=in_specs,
        out_specs=out_specs,
        out_shape=out_shape,
        scratch_shapes=scratch,
        compiler_params=pltpu.CompilerParams(
            dimension_semantics=("arbitrary", "arbitrary"), vmem_limit_bytes=VMEM_LIMIT,
            allow_input_fusion=[k in (1, 3, 6, 10, 11) for k in range(len(in_specs))]),
        name="mix",
    )(x2d, *w)


def _moe_kernel(alpha, n_tiles, n_tok, out_token_major,
                elo_ref, ehi_ref, nv_ref, off_ref, order_ref,
                xe_hbm,
                wg_lo, wu_lo, wd_lo, wg_hi, wu_hi, wd_hi, g2_ref, b2_ref,
                out_hbm,
                gbuf0, gbuf1, gbuf2, obuf0, obuf1, obuf2, gsem, ssem, pend):
    gbuf = (gbuf0, gbuf1, gbuf2)
    obuf = (obuf0, obuf1, obuf2)
    i = pl.program_id(0)
    nv = nv_ref[i]
    o_pitch = X_PITCH if out_token_major else 1
    defer_scatter = out_token_major

    def gather_copy(tok, r, s):
        return pltpu.make_async_copy(xe_hbm.at[pl.ds(tok * XE_PITCH, XE_PITCH), :],
                                     gbuf[s].at[pl.ds(r * XE_PITCH, XE_PITCH), :], gsem.at[s])

    def scatter_copy(tok, r, s, n_rows=1, group=None):
        dst = out_hbm.at[pl.ds(tok * o_pitch, n_rows * o_pitch), :]
        if out_token_major:
            src = r * o_pitch if isinstance(r, int) else pl.multiple_of(r * o_pitch, o_pitch)
            return pltpu.make_async_copy(obuf[s].at[pl.ds(src, n_rows * o_pitch), :], dst, ssem.at[s])
        if group is None:
            group, r = lax.shift_right_logical(r, 3), jnp.bitwise_and(r, SUBLANES - 1)
        return pltpu.make_async_copy(obuf[s].at[group, pl.ds(r, n_rows), :], dst, ssem.at[s])

    def scatter_wait_copy(s, n_rows):
        if out_token_major:
            return scatter_copy(0, 0, s, n_rows=n_rows)
        if n_rows < SUBLANES:
            block = obuf[s].at[0, pl.ds(0, n_rows), :]
        else:
            block = obuf[s].at[pl.ds(0, n_rows // SUBLANES)]
        return pltpu.make_async_copy(block, block, ssem.at[s])

    def start_gather(tile, s):
        base = off_ref[tile]
        for r in range(TM):
            gather_copy(order_ref[base + r], r, s).start(priority=r % 2)

    def wait_gather(s):
        pltpu.make_async_copy(xe_hbm.at[pl.ds(0, TM * XE_PITCH), :], gbuf[s], gsem.at[s]).wait()

    def start_scatter(tile, s, n_rows):
        base = off_ref[tile]
        n_chunks = lax.shift_right_logical(n_rows, DMA_UNROLL.bit_length() - 1)

        def chunk(c, carry):
            r0 = c * DMA_UNROLL
            for j in range(DMA_UNROLL):
                if out_token_major:
                    copy = scatter_copy(order_ref[base + r0 + j], r0 + j, s)
                else:
                    copy = scatter_copy(order_ref[base + r0 + j], j, s, group=c)
                copy.start(priority=j % 2)
            return carry
        lax.fori_loop(0, n_chunks, chunk, 0)

        def tail(r, carry):
            scatter_copy(order_ref[base + r], r, s).start()
            return carry
        lax.fori_loop(n_chunks * DMA_UNROLL, n_rows, tail, 0)

    def start_scatter_full(tile, n_valid, s):
        base = off_ref[tile]
        for r in range(TM):
            tok = jnp.where(r < n_valid, order_ref[base + r], n_tok + s * TM + r)
            scatter_copy(tok, r, s).start(priority=r % 2)
        pend[s] = TM

    def wait_scatter(s):
        n_rows = pend[s]
        width = TM
        while width >= (TM if defer_scatter else 1):
            @pl.when(jnp.bitwise_and(n_rows, width) != 0)
            def _(width=width):
                scatter_wait_copy(s, width).wait()
            width //= 2
        pend[s] = 0

    prev = jnp.maximum(i - 1, 0)
    nv_prev = jnp.where(i > 0, nv_ref[prev], 0)

    @pl.when(i == 0)
    def _():
        for s in range(N_SLOTS):
            pend[s] = 0
        for ahead in range(GATHER_AHEAD):
            start_gather(ahead, ahead)
        if defer_scatter:
            obuf[N_SLOTS - 1][...] = jnp.zeros_like(obuf[N_SLOTS - 1])
            for s in range(N_SLOTS):
                fill = scatter_copy(n_tok + s * TM, 0, N_SLOTS - 1, n_rows=TM)
                fill.start()
                fill.wait()

    def tile_body(s):
        wait_gather(s)
        wait_scatter(s)
        xf = _load_token_major(gbuf[s], TM, XE_PITCH)
        ext = gbuf[s][pl.ds(D_MODEL // LANES, TM, stride=XE_PITCH), :]
        gate_lo = ext[:, 1:2]
        gate_hi = ext[:, 2:3]
        xb = xf.astype(_BF)
        start_gather(i + GATHER_AHEAD, (s + GATHER_AHEAD) % N_SLOTS)
        if defer_scatter:
            start_scatter_full(prev, nv_prev, (s + N_SLOTS - 1) % N_SLOTS)

        def expert(wg, wu, wd, gate):
            a2 = jnp.dot(xb, wg[...], preferred_element_type=_F32)
            u = jnp.dot(xb, wu[...], preferred_element_type=_F32)
            hgt = ((a2 * jnp.tanh(a2) + a2) * (u * gate)).astype(_BF)
            return jnp.dot(hgt, wd[...], preferred_element_type=_F32)

        moe = expert(wg_lo, wu_lo, wd_lo, gate_lo) + expert(wg_hi, wu_hi, wd_hi, gate_hi)
        y = _layer_norm(alpha * xf + moe, g2_ref[...], b2_ref[...])
        if out_token_major:
            _store_token_major(obuf[s], y, X_PITCH)
        else:
            obuf[s][...] = y.reshape(TM // SUBLANES, SUBLANES, D_MODEL)
        if not defer_scatter:
            start_scatter(i, s, nv)
            pend[s] = nv

    slot = lax.rem(i, N_SLOTS)
    issuer = jnp.maximum(i - GATHER_AHEAD, 0)
    gathered_unused = jnp.logical_and(nv == 0, jnp.logical_or(i < GATHER_AHEAD, nv_ref[issuer] > 0))
    for s in range(N_SLOTS):
        @pl.when(jnp.logical_and(nv > 0, slot == s))
        def _(s=s):
            tile_body(s)

        @pl.when(jnp.logical_and(gathered_unused, slot == s))
        def _(s=s):
            wait_gather(s)

        if defer_scatter:
            @pl.when(jnp.logical_and(jnp.logical_and(nv == 0, nv_prev > 0), slot == s))
            def _(s=s):
                start_scatter_full(prev, nv_prev, (s + N_SLOTS - 1) % N_SLOTS)

    @pl.when(i == n_tiles - 1)
    def _():
        for s in range(N_SLOTS):
            wait_scatter(s)


def _moe_call(alpha, layer, xe, plan, w, n_tok, out_token_major):
    elo, ehi, nvalid, off, order = plan
    n_tiles = elo.shape[0] - GATHER_AHEAD
    wg, wu, wd, g2, b2 = w
    if out_token_major:
        obuf_shape, out_rows, out_cols = (TM * X_PITCH, LANES), (n_tok + N_SLOTS * TM) * X_PITCH, LANES
    else:
        obuf_shape, out_rows, out_cols = (TM // SUBLANES, SUBLANES, D_MODEL), n_tok, D_MODEL

    def wspec(shape, which):
        def imap(i, elo_r, ehi_r, *_):
            return ((elo_r, ehi_r)[which][i], 0, 0)
        return pl.BlockSpec((None,) + shape, imap)

    grid_spec = pltpu.PrefetchScalarGridSpec(
        num_scalar_prefetch=5,
        grid=(n_tiles,),
        in_specs=[
            pl.BlockSpec(memory_space=pl.ANY),
            wspec((D_MODEL, D_FF), 0), wspec((D_MODEL, D_FF), 0), wspec((D_FF, D_MODEL), 0),
            wspec((D_MODEL, D_FF), 1), wspec((D_MODEL, D_FF), 1), wspec((D_FF, D_MODEL), 1),
            pl.BlockSpec((None, 1, D_MODEL), lambda i, *_: (layer, 0, 0)),
            pl.BlockSpec((None, 1, D_MODEL), lambda i, *_: (layer, 0, 0)),
        ],
        out_specs=pl.BlockSpec(memory_space=pl.ANY),
        scratch_shapes=[
            *[pltpu.VMEM((TM * XE_PITCH, LANES), _F32) for _ in range(N_SLOTS)],
            *[pltpu.VMEM(obuf_shape, _F32) for _ in range(N_SLOTS)],
            pltpu.SemaphoreType.DMA((N_SLOTS,)),
            pltpu.SemaphoreType.DMA((N_SLOTS,)),
            pltpu.SMEM((N_SLOTS,), jnp.int32),
        ],
    )
    return pl.pallas_call(
        functools.partial(_moe_kernel, alpha, n_tiles, n_tok, out_token_major),
        grid_spec=grid_spec,
        out_shape=jax.ShapeDtypeStruct((out_rows, out_cols), _F32),
        compiler_params=pltpu.CompilerParams(
            dimension_semantics=("arbitrary",), vmem_limit_bytes=VMEM_LIMIT),
        name="moe",
    )(elo, ehi, nvalid, off, order, xe, wg, wu, wd, wg, wu, wd, g2, b2)


_PAIR_LO = (0, 0, 0, 1, 1, 2)
_PAIR_HI = (1, 2, 3, 2, 3, 3)


def _plan(cls, n_tiles):
    n_tok = cls.shape[0]
    tok = jnp.arange(n_tok, dtype=jnp.int32)
    order = jnp.sort(cls * n_tok + tok) % n_tok
    order = jnp.concatenate([order, jnp.zeros((TM,), jnp.int32)])
    counts = jnp.sum((cls[:, None] == jnp.arange(N_CLASSES, dtype=jnp.int32)[None, :]).astype(jnp.int32), axis=0)
    cstart = jnp.cumsum(counts) - counts
    tiles = (counts + TM - 1) // TM
    tile_end = jnp.cumsum(tiles)
    tile_start = tile_end - tiles
    ti = jnp.arange(n_tiles, dtype=jnp.int32)
    n_used = tile_end[-1]
    tsel = jnp.minimum(ti, n_used - 1)
    tcls = jnp.sum((tile_end[None, :] <= tsel[:, None]).astype(jnp.int32), axis=1)
    within = (tsel - tile_start[tcls]) * TM
    nvalid = jnp.where(ti < n_used, jnp.clip(counts[tcls] - within, 0, TM), 0).astype(jnp.int32)
    off = (cstart[tcls] + within).astype(jnp.int32)
    grp = tcls // N_PAIRS
    pair = tcls % N_PAIRS
    elo = grp * EXPERTS_PER_GROUP + jnp.asarray(_PAIR_LO, jnp.int32)[pair]
    ehi = grp * EXPERTS_PER_GROUP + jnp.asarray(_PAIR_HI, jnp.int32)[pair]
    return elo.astype(jnp.int32), ehi.astype(jnp.int32), nvalid, off, order


def kernel(x, w_in, conv_a_w, w_branch_a, conv_r_w, conv_r_b, w_rg_a, b_rg_a, w_rg_x, b_rg_x, rg_lambda, w_branch_r, w_out, ln1_g, ln1_b, w_router, b_router, w_exp_gate, w_exp_up, w_exp_down, ln2_g, ln2_b):
    n_seq, seq, _ = x.shape
    depth = w_in.shape[0]
    n_tok = n_seq * seq
    assert seq % TT == 0 and n_tok % TM == 0
    n_tiles = (n_tok + N_CLASSES * (TM - 1)) // TM + GATHER_AHEAD
    alpha = (2.0 * depth) ** 0.25

    wrt = w_router.T.astype(_F32)
    wrt_hi = wrt.astype(_BF)
    wrt_lo = (wrt - wrt_hi.astype(_F32)).astype(_BF)
    brt = b_router.astype(_F32).reshape(N_EXPERTS, 1)

    in_scale = jnp.concatenate([jnp.ones((OFF_GA,), _F32), jnp.full((D_IN - OFF_GA,), 0.5, _F32)])
    mix_w = (
        (w_in * in_scale).astype(_BF),
        conv_a_w,
        w_branch_a.astype(_BF),
        conv_r_w,
        conv_r_b.reshape(depth, 1, D_R),
        (0.5 * jnp.concatenate([w_rg_a, w_rg_x], axis=-1)).astype(_BF),
        0.5 * b_rg_a.reshape(depth, 1, D_R),
        0.5 * b_rg_x.reshape(depth, 1, D_R),
        rg_lambda.reshape(depth, 1, D_R),
        (0.25 * w_branch_r).astype(_BF),
        (0.5 * w_out).astype(_BF),
        ln1_g.reshape(depth, 1, D_MODEL),
        ln1_b.reshape(depth, 1, D_MODEL),
        wrt_hi, wrt_lo, brt,
        w_exp_gate.reshape(depth, N_EXPERTS * D_MODEL, D_FF),
        w_exp_up.reshape(depth, N_EXPERTS * D_MODEL, D_FF),
        w_exp_down.reshape(depth, N_EXPERTS * D_FF, D_MODEL),
    )
    ln2 = (ln2_g.reshape(depth, 1, D_MODEL), ln2_b.reshape(depth, 1, D_MODEL))

    h = x.reshape(n_tok, D_MODEL)
    for l in range(depth):
        xe, route, wg, wu, wd = _mix_call(alpha, l, h, l > 0, n_seq, n_tok, mix_w)
        plan = _plan(route[0].astype(jnp.int32), n_tiles + GATHER_AHEAD)
        moe_w = (wg.reshape(N_EXPERTS, D_MODEL, D_FF), wu.reshape(N_EXPERTS, D_MODEL, D_FF),
                 wd.reshape(N_EXPERTS, D_FF, D_MODEL)) + ln2
        h = _moe_call(alpha, l, xe, plan, moe_w, n_tok, l < depth - 1)
    return h.reshape(n_seq, seq, D_MODEL)
```
